```python
import math
import jax, jax.numpy as jnp
from jax import lax
import numpy as np

D_MODEL = 1024
BATCH = 4
SEQ = 4096
DEPTH = 1

CHUNK = 64
Q_BLOCK = 128
EPS = 1e-6

SB_HEADS = 8
SB_HEAD_DIM = 64
SB_WIDTH = SB_HEADS * SB_HEAD_DIM

MLA_HEADS = 8
MLA_NOPE_DIM = 64
MLA_ROPE_DIM = 32
MLA_QK_DIM = MLA_NOPE_DIM + MLA_ROPE_DIM
MLA_V_DIM = 64
MLA_Q_LORA = 384
MLA_KV_LORA = 256
MLA_WIDTH = MLA_HEADS * MLA_V_DIM
ROPE_THETA = 10000.0

IN_COLS = (SB_WIDTH, SB_WIDTH, SB_WIDTH,
           MLA_Q_LORA, MLA_KV_LORA, MLA_ROPE_DIM,
           D_MODEL, D_MODEL)
IN_WIDTH = sum(IN_COLS)
IN_SPLITS = tuple(int(c) for c in np.cumsum(IN_COLS)[:-1])

N_EXPERTS = 32
TOP_K = 4
D_FF = 1024
SWIGLU_LIMIT = 7.0
SWIGLU_ALPHA = 1.702
EXPERT_BLOCK = 256

kernel_name = "hybrid_stickbreak_mla_moe_block"


def rmsnorm(x, g):
    xf = x.astype(jnp.float32)
    xf = xf * lax.rsqrt(jnp.mean(xf * xf, axis=-1, keepdims=True) + EPS)
    return (xf * g.astype(jnp.float32)).astype(x.dtype)


def apply_rope(x, positions):
    half = MLA_ROPE_DIM // 2
    inv_freq = ROPE_THETA ** (-jnp.arange(half, dtype=jnp.float32) / half)
    ang = positions.astype(jnp.float32)[:, :, None, None] * inv_freq
    cos = jnp.cos(ang).astype(x.dtype)
    sin = jnp.sin(ang).astype(x.dtype)
    x1, x2 = x[..., :half], x[..., half:]
    return jnp.concatenate([x1 * cos - x2 * sin, x2 * cos + x1 * sin], axis=-1)


def stick_breaking_attention(q, k, v):
    seq = q.shape[2]
    scale = 1.0 / math.sqrt(q.shape[-1])
    outs = []
    for i in range(seq // Q_BLOCK):
        q0 = i * Q_BLOCK
        kn = q0 + Q_BLOCK
        z = jnp.einsum('bhqd,bhkd->bhqk', q[:, :, q0:kn], k[:, :, :kn]).astype(jnp.float32) * scale
        t_idx = q0 + jnp.arange(Q_BLOCK)[:, None]
        s_idx = jnp.arange(kn)[None, :]
        strict = s_idx < t_idx
        log_beta = jax.nn.log_sigmoid(z)
        log_stay = jnp.where(strict, log_beta - z, 0.0)
        later = lax.cumsum(log_stay, axis=3, reverse=True) - log_stay
        a = jnp.where(strict, jnp.exp(log_beta + later), 0.0)
        outs.append(jnp.einsum('bhqk,bhkd->bhqd', a.astype(v.dtype), v[:, :, :kn]))
    return jnp.concatenate(outs, axis=2)


def chunk_causal_softmax_attention(q, k, v):
    seq = q.shape[2]
    scale = 1.0 / math.sqrt(q.shape[-1])
    outs = []
    for i in range(seq // Q_BLOCK):
        q0 = i * Q_BLOCK
        kn = q0 + Q_BLOCK
        s = jnp.einsum('bhqd,bhkd->bhqk', q[:, :, q0:kn], k[:, :, :kn]).astype(jnp.float32) * scale
        q_chunk = (q0 + jnp.arange(Q_BLOCK))[:, None] // CHUNK
        k_chunk = jnp.arange(kn)[None, :] // CHUNK
        s = jnp.where(k_chunk <= q_chunk, s, -jnp.inf)
        p = jax.nn.softmax(s, axis=-1)
        outs.append(jnp.einsum('bhqk,bhkd->bhqd', p.astype(v.dtype), v[:, :, :kn]))
    return jnp.concatenate(outs, axis=2)


def moe_ffn(h, w_router, b_router, w_up, b_up, w_down, b_down):
    n_tok, d = h.shape
    logits = (h @ w_router + b_router).astype(jnp.float32)
    top_logit, top_idx = lax.top_k(logits, TOP_K)
    top_w = jax.nn.softmax(top_logit, axis=-1)
    n_assign = n_tok * TOP_K
    e_flat = top_idx.reshape(n_assign).astype(jnp.int32)
    tok_flat = jnp.repeat(jnp.arange(n_tok, dtype=jnp.int32), TOP_K)
    w_flat = top_w.reshape(n_assign)
    order = jnp.argsort(e_flat)
    e_sorted = e_flat[order]
    counts = jnp.bincount(e_flat, length=N_EXPERTS)
    padded = (counts + EXPERT_BLOCK - 1) // EXPERT_BLOCK * EXPERT_BLOCK
    starts = jnp.cumsum(counts) - counts
    pad_starts = jnp.cumsum(padded) - padded
    dest = pad_starts[e_sorted] + jnp.arange(n_assign, dtype=jnp.int32) - starts[e_sorted]
    n_blocks = -(-(n_assign + N_EXPERTS * (EXPERT_BLOCK - 1)) // EXPERT_BLOCK)
    n_rows = n_blocks * EXPERT_BLOCK
    row_tok = jnp.full((n_rows,), n_tok, jnp.int32).at[dest].set(tok_flat[order])
    row_w = jnp.zeros((n_rows,), jnp.float32).at[dest].set(w_flat[order])
    block_start = jnp.arange(n_blocks, dtype=jnp.int32) * EXPERT_BLOCK
    block_expert = jnp.minimum(jnp.searchsorted(pad_starts + padded, block_start, side='right'),
                               N_EXPERTS - 1).astype(jnp.int32)
    h_pad = jnp.concatenate([h, jnp.zeros((1, d), h.dtype)], axis=0)
    xb = h_pad[row_tok].reshape(n_blocks, EXPERT_BLOCK, d)

    def expert_block(args):
        xblk, e = args
        gu = xblk @ w_up[e] + b_up[e]
        x_glu = jnp.minimum(gu[:, 0::2], SWIGLU_LIMIT)
        x_lin = jnp.clip(gu[:, 1::2], -SWIGLU_LIMIT, SWIGLU_LIMIT)
        act = x_glu * jax.nn.sigmoid(SWIGLU_ALPHA * x_glu) * (x_lin + 1.0)
        return act @ w_down[e] + b_down[e]

    yb = lax.map(expert_block, (xb, block_expert))
    y_rows = yb.reshape(n_rows, d) * row_w[:, None].astype(yb.dtype)
    return jax.ops.segment_sum(y_rows, row_tok, num_segments=n_tok + 1)[:n_tok].astype(h.dtype)


def setup_inputs(seed: int = 0) -> dict:
    key = jax.random.key(seed)
    ks = jax.random.split(key, 22)

    def dense(k, shape, fan_in):
        return jax.random.normal(k, shape, jnp.float32) * (fan_in ** -0.5)

    def gain(k, shape):
        return 1.0 + 0.05 * jax.random.normal(k, shape, jnp.float32)

    def small(k, shape):
        return 0.01 * jax.random.normal(k, shape, jnp.float32)

    offsets = jax.random.randint(ks[1], (BATCH, 1), 0, 1024, dtype=jnp.int32)
    positions = jnp.arange(SEQ, dtype=jnp.int32)[None, :] + offsets
    return {
        "x": jax.random.normal(ks[0], (BATCH, SEQ, D_MODEL), jnp.float32),
        "positions": positions,
        "g_mix": gain(ks[2], (DEPTH, D_MODEL)),
        "w_in": dense(ks[3], (DEPTH, D_MODEL, IN_WIDTH), D_MODEL),
        "g_cq": gain(ks[4], (DEPTH, MLA_Q_LORA)),
        "w_uq": dense(ks[5], (DEPTH, MLA_Q_LORA, MLA_HEADS * MLA_QK_DIM), MLA_Q_LORA),
        "g_ckv": gain(ks[6], (DEPTH, MLA_KV_LORA)),
        "w_ukv": dense(ks[7], (DEPTH, MLA_KV_LORA, MLA_HEADS * (MLA_NOPE_DIM + MLA_V_DIM)), MLA_KV_LORA),
        "g_qnorm": gain(ks[8], (DEPTH, MLA_QK_DIM)),
        "g_knorm": gain(ks[9], (DEPTH, MLA_QK_DIM)),
        "w_o_sb": dense(ks[10], (DEPTH, SB_WIDTH, D_MODEL), SB_WIDTH),
        "w_o_mla": dense(ks[11], (DEPTH, MLA_WIDTH, D_MODEL), MLA_WIDTH),
        "w_out": dense(ks[12], (DEPTH, D_MODEL, D_MODEL), D_MODEL),
        "g_ffn": gain(ks[13], (DEPTH, D_MODEL)),
        "w_router": dense(ks[14], (DEPTH, D_MODEL, N_EXPERTS), D_MODEL),
        "b_router": small(ks[15], (DEPTH, N_EXPERTS)),
        "w_up": dense(ks[16], (DEPTH, N_EXPERTS, D_MODEL, 2 * D_FF), D_MODEL),
        "b_up": small(ks[17], (DEPTH, N_EXPERTS, 2 * D_FF)),
        "w_down": dense(ks[18], (DEPTH, N_EXPERTS, D_FF, D_MODEL), D_FF),
        "b_down": small(ks[19], (DEPTH, N_EXPERTS, D_MODEL)),
    }


def reference(x, positions, g_mix, w_in, g_cq, w_uq, g_ckv, w_ukv, g_qnorm, g_knorm,
              w_o_sb, w_o_mla, w_out, g_ffn, w_router, b_router, w_up, b_up, w_down, b_down):
    bsz, seq, d = x.shape
    for l in range(DEPTH):
        h = rmsnorm(x, g_mix[l])
        proj = h @ w_in[l]
        sb_q, sb_k, sb_v, c_q, c_kv, k_rope, gate_sb, gate_mla = jnp.split(proj, IN_SPLITS, axis=-1)

        def sb_heads(t):
            return t.reshape(bsz, seq, SB_HEADS, SB_HEAD_DIM).transpose(0, 2, 1, 3)
        o_sb = stick_breaking_attention(sb_heads(sb_q), sb_heads(sb_k), sb_heads(sb_v))
        o_sb = o_sb.transpose(0, 2, 1, 3).reshape(bsz, seq, SB_WIDTH)

        q = (rmsnorm(c_q, g_cq[l]) @ w_uq[l]).reshape(bsz, seq, MLA_HEADS, MLA_QK_DIM)
        q = rmsnorm(q, g_qnorm[l])
        q = jnp.concatenate([q[..., :MLA_NOPE_DIM], apply_rope(q[..., MLA_NOPE_DIM:], positions)], axis=-1)
        kv = (rmsnorm(c_kv, g_ckv[l]) @ w_ukv[l]).reshape(bsz, seq, MLA_HEADS, MLA_NOPE_DIM + MLA_V_DIM)
        k_nope, v = kv[..., :MLA_NOPE_DIM], kv[..., MLA_NOPE_DIM:]
        k_r = jnp.broadcast_to(k_rope[:, :, None, :], (bsz, seq, MLA_HEADS, MLA_ROPE_DIM))
        k = rmsnorm(jnp.concatenate([k_nope, k_r], axis=-1), g_knorm[l])
        k = jnp.concatenate([k[..., :MLA_NOPE_DIM], apply_rope(k[..., MLA_NOPE_DIM:], positions)], axis=-1)
        o_mla = chunk_causal_softmax_attention(q.transpose(0, 2, 1, 3), k.transpose(0, 2, 1, 3),
                                               v.transpose(0, 2, 1, 3))
        o_mla = o_mla.transpose(0, 2, 1, 3).reshape(bsz, seq, MLA_WIDTH)

        mixed = (jax.nn.sigmoid(gate_sb) * (o_sb @ w_o_sb[l])
                 + jax.nn.sigmoid(gate_mla) * (o_mla @ w_o_mla[l]))
        x = x + mixed @ w_out[l]

        h2 = rmsnorm(x, g_ffn[l]).reshape(bsz * seq, d)
        y = moe_ffn(h2, w_router[l], b_router[l], w_up[l], b_up[l], w_down[l], b_down[l])
        x = x + y.reshape(bsz, seq, d)
    return x
```

```python
import functools
import math

import jax
import jax.numpy as jnp
from jax import lax
from jax.experimental import pallas as pl
from jax.experimental.pallas import tpu as pltpu

F32 = jnp.float32
BF16 = jnp.bfloat16

EPS = 1e-6
CHUNK = 64
SB_HEADS = 8
SB_DIM = 64
SB_WIDTH = SB_HEADS * SB_DIM
MLA_HEADS = 8
MLA_NOPE = 64
MLA_ROPE = 32
MLA_QK = MLA_NOPE + MLA_ROPE
MLA_V = 64
MLA_Q_LORA = 384
MLA_KV_LORA = 256
ROPE_THETA = 10000.0
N_EXPERTS = 32
TOP_K = 4
SWIGLU_LIMIT = 7.0
SWIGLU_ALPHA = 1.702

LANES = 128
HEAD_PAD = LANES
TOKEN_TILE = 256
ATTN_TILE = 256
EXPERT_TILE = 256
SB_DEAD = -110.0
VMEM_LIMIT = 48 * 1024 * 1024


def _nt_dot(a, b):
    return lax.dot_general(a, b, (((1,), (1,)), ((), ())), preferred_element_type=F32)


def _dot(a, b):
    return jnp.dot(a, b, preferred_element_type=F32)


def _rms(t, g):
    return t * lax.rsqrt(jnp.mean(t * t, axis=-1, keepdims=True) + EPS) * g


def _pre_kernel(x_ref, pos_ref, gmix_ref, w1_ref, w2_ref, w3_ref, gcq_ref, gckv_ref, wuq_ref, wuk_ref,
                wuv_ref, gq_ref, gk_ref, invf_ref, sgn_ref,
                sb_ref, qm_ref, km_ref, vm_ref, gate_ref):
    h = _rms(x_ref[...], gmix_ref[...]).astype(BF16)

    sb_ref[:, :SB_WIDTH] = (_dot(h, w1_ref[:, :SB_WIDTH]) * (1.0 / math.sqrt(SB_DIM))).astype(BF16)
    sb_ref[:, SB_WIDTH:] = _dot(h, w1_ref[:, SB_WIDTH:]).astype(BF16)

    gate_ref[...] = jax.nn.sigmoid(_dot(h, w3_ref[...])).astype(BF16)

    lat = _dot(h, w2_ref[...])
    cqn = _rms(lat[:, :MLA_Q_LORA], gcq_ref[...]).astype(BF16)
    ckvn = _rms(lat[:, MLA_Q_LORA:MLA_Q_LORA + MLA_KV_LORA], gckv_ref[...]).astype(BF16)
    kr = lat[:, MLA_Q_LORA + MLA_KV_LORA:]
    vm_ref[...] = _dot(ckvn, wuv_ref[...]).astype(BF16)
    qf = _dot(cqn, wuq_ref[...])
    kf = _dot(ckvn, wuk_ref[...])

    ang = pos_ref[...].astype(F32) * invf_ref[...]
    cos = jnp.cos(ang)
    sin = jnp.sin(ang) * sgn_ref[...]
    lane = lax.broadcasted_iota(jnp.int32, (1, LANES), 1)
    first_half = lane < MLA_NOPE + MLA_ROPE // 2

    def rope(t):
        partner = jnp.where(first_half, pltpu.roll(t, LANES - MLA_ROPE // 2, 1), pltpu.roll(t, MLA_ROPE // 2, 1))
        return t * cos + partner * sin

    def head_norm(t, g):
        return t * lax.rsqrt(jnp.sum(t * t, axis=-1, keepdims=True) * (1.0 / MLA_QK) + EPS) * g

    mla_scale = 1.0 / math.sqrt(MLA_QK)
    for hd in range(MLA_HEADS):
        sl = slice(hd * HEAD_PAD, (hd + 1) * HEAD_PAD)
        qm_ref[:, sl] = (rope(head_norm(qf[:, sl], gq_ref[:, sl])) * mla_scale).astype(BF16)
        km_ref[:, sl] = rope(head_norm(kf[:, sl] + kr, gk_ref[:, sl])).astype(BF16)


def _pre_call(x2, pos2, gmix, w1, w2, w3, gcq, gckv, wuq, wuk, wuv, gq, gk, invf, sgn):
    n, d = x2.shape
    tm = TOKEN_TILE
    const = lambda i: (0, 0)
    full = lambda a: pl.BlockSpec(a.shape, const)
    return pl.pallas_call(
        _pre_kernel,
        grid=(n // tm,),
        in_specs=[pl.BlockSpec((tm, d), lambda i: (i, 0)), pl.BlockSpec((tm, 1), lambda i: (i, 0)),
                  full(gmix), full(w1), full(w2), full(w3), full(gcq), full(gckv), full(wuq), full(wuk),
                  full(wuv), full(gq), full(gk), full(invf), full(sgn)],
        out_specs=[pl.BlockSpec((tm, 3 * SB_WIDTH), lambda i: (i, 0)),
                   pl.BlockSpec((tm, MLA_HEADS * HEAD_PAD), lambda i: (i, 0)),
                   pl.BlockSpec((tm, MLA_HEADS * HEAD_PAD), lambda i: (i, 0)),
                   pl.BlockSpec((tm, MLA_HEADS * MLA_V), lambda i: (i, 0)),
                   pl.BlockSpec((tm, 2 * d), lambda i: (i, 0))],
        out_shape=[jax.ShapeDtypeStruct((n, 3 * SB_WIDTH), BF16),
                   jax.ShapeDtypeStruct((n, MLA_HEADS * HEAD_PAD), BF16),
                   jax.ShapeDtypeStruct((n, MLA_HEADS * HEAD_PAD), BF16),
                   jax.ShapeDtypeStruct((n, MLA_HEADS * MLA_V), BF16),
                   jax.ShapeDtypeStruct((n, 2 * d), BF16)],
        compiler_params=pltpu.CompilerParams(dimension_semantics=("arbitrary",), vmem_limit_bytes=VMEM_LIMIT),
        name="pre",
    )(x2, pos2, gmix, w1, w2, w3, gcq, gckv, wuq, wuk, wuv, gq, gk, invf, sgn)


def _sb_kernel(q_ref, k_ref, v_ref, o_ref, acc_ref, car_ref):
    t = ATTN_TILE
    i = pl.program_id(2)
    lane = lax.broadcasted_iota(jnp.int32, (1, LANES), 1)
    lo = lane < SB_DIM
    q2 = q_ref[...]
    zero = jnp.zeros_like(q2)
    qs = (jnp.where(lo, q2, zero), jnp.where(lo, zero, q2))
    rr = lax.broadcasted_iota(jnp.int32, (t, t), 0)
    cc = lax.broadcasted_iota(jnp.int32, (t, t), 1)
    later_sum = (rr > cc).astype(BF16)
    strict = cc < rr

    acc_ref[...] = jnp.zeros_like(acc_ref)
    car_ref[...] = jnp.zeros_like(car_ref)

    def tile(j, masked):
        start = pl.multiple_of(j * t, t)
        kt = k_ref[pl.ds(start, t), :]
        vt = v_ref[pl.ds(start, t), :]
        outs = []
        for hh in range(2):
            z = _nt_dot(qs[hh], kt)
            sp = jnp.log1p(jnp.exp(-jnp.abs(z)))
            log_beta = jnp.minimum(z, 0.0) - sp
            log_stay = log_beta - z
            if masked:
                log_stay = jnp.where(strict, log_stay, 0.0)
            hi = log_stay.astype(BF16)
            lo_part = (log_stay - hi.astype(F32)).astype(BF16)
            later = _dot(hi, later_sum) + _dot(lo_part, later_sum)
            carry = car_ref[hh]
            a = jnp.exp(log_beta + later + carry)
            if masked:
                a = jnp.where(strict, a, 0.0)
            outs.append(_dot(a.astype(BF16), vt))
            car_ref[hh] = carry + jnp.sum(log_stay, axis=1, keepdims=True)
        acc_ref[...] += jnp.where(lo, outs[0], outs[1])

    tile(i, True)

    def cond(c):
        j, alive = c
        return jnp.logical_and(j >= 0, alive > SB_DEAD)

    def body(c):
        j, _ = c
        tile(j, False)
        return j - 1, jnp.max(car_ref[...])

    lax.while_loop(cond, body, (i - 1, jnp.max(car_ref[...])))
    o_ref[...] = acc_ref[...].astype(o_ref.dtype)


def _sb_call(sb, bsz, seq):
    t = ATTN_TILE
    nq = seq // t
    pairs = SB_HEADS // 2
    return pl.pallas_call(
        _sb_kernel,
        grid=(bsz, pairs, nq),
        in_specs=[pl.BlockSpec((t, LANES), lambda b, p, i: (b * nq + i, p)),
                  pl.BlockSpec((seq, LANES), lambda b, p, i: (b, pairs + p)),
                  pl.BlockSpec((seq, LANES), lambda b, p, i: (b, 2 * pairs + p))],
        out_specs=pl.BlockSpec((t, LANES), lambda b, p, i: (b * nq + i, p)),
        out_shape=jax.ShapeDtypeStruct((bsz * seq, SB_WIDTH), BF16),
        scratch_shapes=[pltpu.VMEM((t, LANES), F32), pltpu.VMEM((2, t, 1), F32)],
        compiler_params=pltpu.CompilerParams(dimension_semantics=("arbitrary",) * 3, vmem_limit_bytes=VMEM_LIMIT),
        name="sb_attn",
    )(sb, sb, sb)


def _mla_kernel(q_ref, k_ref, v_ref, o_ref, m_ref, l_ref, acc_ref):
    t = ATTN_TILE
    i = pl.program_id(2)
    lane = lax.broadcasted_iota(jnp.int32, (1, LANES), 1)
    lo = lane < MLA_V
    rr = lax.broadcasted_iota(jnp.int32, (t, t), 0)
    cc = lax.broadcasted_iota(jnp.int32, (t, t), 1)
    visible = (cc // CHUNK) <= (rr // CHUNK)

    m_ref[...] = jnp.full_like(m_ref, -jnp.inf)
    l_ref[...] = jnp.zeros_like(l_ref)
    acc_ref[...] = jnp.zeros_like(acc_ref)

    def tile(j, masked):
        start = pl.multiple_of(j * t, t)
        kt = k_ref[pl.ds(start, t), :]
        vt = v_ref[pl.ds(start, t), :]
        for hh in range(2):
            sl = slice(hh * HEAD_PAD, (hh + 1) * HEAD_PAD)
            s = _nt_dot(q_ref[:, sl], kt[:, sl])
            if masked:
                s = jnp.where(visible, s, -jnp.inf)
            m_old = m_ref[hh]
            m_new = jnp.maximum(m_old, jnp.max(s, axis=-1, keepdims=True))
            alpha = jnp.exp(m_old - m_new)
            p = jnp.exp(s - m_new)
            l_ref[hh] = alpha * l_ref[hh] + jnp.sum(p, axis=-1, keepdims=True)
            acc_ref[hh] = alpha * acc_ref[hh] + _dot(p.astype(BF16), vt)
            m_ref[hh] = m_new

    def body(j, c):
        tile(j, False)
        return c

    lax.fori_loop(0, i, body, 0)
    tile(i, True)
    o_ref[...] = jnp.where(lo, acc_ref[0] / l_ref[0], acc_ref[1] / l_ref[1]).astype(o_ref.dtype)


def _mla_call(qm, km, vm, bsz, seq):
    t = ATTN_TILE
    nq = seq // t
    pairs = MLA_HEADS // 2
    return pl.pallas_call(
        _mla_kernel,
        grid=(bsz, pairs, nq),
        in_specs=[pl.BlockSpec((t, 2 * HEAD_PAD), lambda b, p, i: (b * nq + i, p)),
                  pl.BlockSpec((seq, 2 * HEAD_PAD), lambda b, p, i: (b, p)),
                  pl.BlockSpec((seq, LANES), lambda b, p, i: (b, p))],
        out_specs=pl.BlockSpec((t, LANES), lambda b, p, i: (b * nq + i, p)),
        out_shape=jax.ShapeDtypeStruct((bsz * seq, MLA_HEADS * MLA_V), BF16),
        scratch_shapes=[pltpu.VMEM((2, t, 1), F32), pltpu.VMEM((2, t, 1), F32), pltpu.VMEM((2, t, LANES), F32)],
        compiler_params=pltpu.CompilerParams(dimension_semantics=("arbitrary",) * 3, vmem_limit_bytes=VMEM_LIMIT),
        name="mla_attn",
    )(qm, km, vm)


def _post_kernel(osb_ref, omla_ref, gate_ref, x_ref, wosb_ref, womla_ref, wout_ref, gffn_ref, wrh_ref, wrl_ref,
                 br_ref, x1_ref, h2_ref, topw_ref, tope_ref, pos_ref, cnt_ref, run_ref):
    tm = TOKEN_TILE
    d = x_ref.shape[1]

    @pl.when(pl.program_id(0) == 0)
    def _():
        run_ref[...] = jnp.zeros_like(run_ref)

    a = _dot(osb_ref[...], wosb_ref[...])
    b = _dot(omla_ref[...], womla_ref[...])
    mixed = gate_ref[:, :d].astype(F32) * a + gate_ref[:, d:].astype(F32) * b
    x1 = x_ref[...] + _dot(mixed.astype(BF16), wout_ref[...])
    x1_ref[...] = x1
    h2 = _rms(x1, gffn_ref[...])
    h2_ref[...] = h2

    h_hi = h2.astype(BF16)
    h_lo = (h2 - h_hi.astype(F32)).astype(BF16)
    logits = (_nt_dot(wrh_ref[...], h_hi) + _nt_dot(wrh_ref[...], h_lo) + _nt_dot(wrl_ref[...], h_hi)
              + br_ref[...])

    eidx = lax.broadcasted_iota(jnp.int32, (N_EXPERTS, tm), 0)
    work = logits
    tops, sels = [], []
    for _ in range(TOP_K):
        mk = jnp.max(work, axis=0, keepdims=True)
        ik = jnp.min(jnp.where(work == mk, eidx, N_EXPERTS), axis=0, keepdims=True)
        sel = eidx == ik
        work = jnp.where(sel, -jnp.inf, work)
        tops.append(mk)
        sels.append(sel)
    exps = [jnp.exp(mk - tops[0]) for mk in tops]
    denom = exps[0] + exps[1] + exps[2] + exps[3]

    chosen = jnp.zeros((N_EXPERTS, tm), F32)
    for sel in sels:
        chosen = chosen + sel.astype(F32)
    rr = lax.broadcasted_iota(jnp.int32, (tm, tm), 0)
    cc = lax.broadcasted_iota(jnp.int32, (tm, tm), 1)
    before = (rr < cc).astype(BF16)
    rank = _dot(chosen.astype(BF16), before) + run_ref[...]
    for k in range(TOP_K):
        topw_ref[k:k + 1, :] = exps[k] / denom
        tope_ref[k:k + 1, :] = jnp.sum(jnp.where(sels[k], eidx, 0), axis=0, keepdims=True)
        pos_ref[k:k + 1, :] = jnp.sum(jnp.where(sels[k], rank, 0.0), axis=0, keepdims=True).astype(jnp.int32)
    run_ref[...] = run_ref[...] + jnp.sum(chosen, axis=1, keepdims=True)
    cnt_ref[...] = jnp.broadcast_to(run_ref[...], cnt_ref.shape)


def _post_call(osb, omla, gates, x2, wosb, womla, wout, gffn, wrh, wrl, br):
    n, d = x2.shape
    tm = TOKEN_TILE
    const = lambda i: (0, 0)
    full = lambda a: pl.BlockSpec(a.shape, const)
    row = lambda w: pl.BlockSpec((tm, w), lambda i: (i, 0))
    col = lambda: pl.BlockSpec((TOP_K, tm), lambda i: (0, i))
    return pl.pallas_call(
        _post_kernel,
        grid=(n // tm,),
        in_specs=[row(osb.shape[1]), row(omla.shape[1]), row(2 * d), row(d), full(wosb), full(womla), full(wout),
                  full(gffn), full(wrh), full(wrl), full(br)],
        out_specs=[row(d), row(d), col(), col(), col(), pl.BlockSpec((N_EXPERTS, LANES), const)],
        out_shape=[jax.ShapeDtypeStruct((n, d), F32), jax.ShapeDtypeStruct((n, d), F32),
                   jax.ShapeDtypeStruct((TOP_K, n), F32), jax.ShapeDtypeStruct((TOP_K, n), jnp.int32),
                   jax.ShapeDtypeStruct((TOP_K, n), jnp.int32), jax.ShapeDtypeStruct((N_EXPERTS, LANES), F32)],
        scratch_shapes=[pltpu.VMEM((N_EXPERTS, 1), F32)],
        compiler_params=pltpu.CompilerParams(dimension_semantics=("arbitrary",), vmem_limit_bytes=VMEM_LIMIT),
        name="post",
    )(osb, omla, gates, x2, wosb, womla, wout, gffn, wrh, wrl, br)


def _row_copy(src_hbm, row, dst_buf, slot, r, sem):
    return pltpu.make_async_copy(src_hbm.at[pl.ds(row, 1)], dst_buf.at[slot, pl.ds(r, 1)], sem.at[slot])


def _moe_kernel(be_ref, idx_ref, nxt_ref, h2_hbm, wup_ref, bup_ref, wdn_ref, bdn_ref, roww_ref, y_ref, xbuf, sem):
    t = EXPERT_TILE
    g = pl.program_id(0)
    nb = pl.num_programs(0)
    slot = g % 2
    ff = wdn_ref.shape[0]

    def start_gather(ref, dst_slot):
        def body(r, c):
            _row_copy(h2_hbm, ref[0, 0, r], xbuf, dst_slot, r, sem).start()
            return c
        lax.fori_loop(0, t, body, 0)

    @pl.when(g == 0)
    def _():
        start_gather(idx_ref, 0)

    @pl.when(g + 1 < nb)
    def _():
        start_gather(nxt_ref, 1 - slot)

    def wait_body(r, c):
        _row_copy(h2_hbm, 0, xbuf, slot, r, sem).wait()
        return c
    lax.fori_loop(0, t, wait_body, 0)

    x = xbuf[slot].astype(BF16)
    gu = _dot(x, wup_ref[...]) + bup_ref[...]
    x_glu = jnp.minimum(gu[:, :ff], SWIGLU_LIMIT)
    x_lin = jnp.clip(gu[:, ff:], -SWIGLU_LIMIT, SWIGLU_LIMIT)
    act = x_glu * jax.nn.sigmoid(SWIGLU_ALPHA * x_glu) * (x_lin + 1.0)
    y = _dot(act.astype(BF16), wdn_ref[...]) + bdn_ref[...]
    y_ref[...] = y * roww_ref[...]


def _moe_call(block_expert, row_tok, row_w, h2, wup, bup, wdn, bdn):
    t = EXPERT_TILE
    n_rows = row_tok.shape[0]
    nb = n_rows // t
    d = h2.shape[1]
    ff = wdn.shape[1]
    idx3 = row_tok.reshape(nb, 1, t)
    grid_spec = pltpu.PrefetchScalarGridSpec(
        num_scalar_prefetch=1,
        grid=(nb,),
        in_specs=[pl.BlockSpec((1, 1, t), lambda g, be: (g, 0, 0), memory_space=pltpu.SMEM),
                  pl.BlockSpec((1, 1, t), lambda g, be: (jnp.minimum(g + 1, nb - 1), 0, 0), memory_space=pltpu.SMEM),
                  pl.BlockSpec(memory_space=pl.ANY),
                  pl.BlockSpec((None, d, 2 * ff), lambda g, be: (be[g], 0, 0)),
                  pl.BlockSpec((None, 1, 2 * ff), lambda g, be: (be[g], 0, 0)),
                  pl.BlockSpec((None, ff, d), lambda g, be: (be[g], 0, 0)),
                  pl.BlockSpec((None, 1, d), lambda g, be: (be[g], 0, 0)),
                  pl.BlockSpec((t, 1), lambda g, be: (g, 0))],
        out_specs=pl.BlockSpec((t, d), lambda g, be: (g, 0)),
        scratch_shapes=[pltpu.VMEM((2, t, d), F32), pltpu.SemaphoreType.DMA((2,))],
    )
    return pl.pallas_call(
        _moe_kernel,
        grid_spec=grid_spec,
        out_shape=jax.ShapeDtypeStruct((n_rows, d), F32),
        compiler_params=pltpu.CompilerParams(dimension_semantics=("arbitrary",), vmem_limit_bytes=VMEM_LIMIT),
        name="moe",
    )(block_expert, idx3, idx3, h2, wup, bup, wdn, bdn, row_w.reshape(n_rows, 1))


def _comb_kernel(idx_ref, nxt_ref, y_hbm, x1_ref, o_ref, ybuf, sem):
    tm = TOKEN_TILE
    g = pl.program_id(0)
    nb = pl.num_programs(0)
    slot = g % 2
    rows = TOP_K * tm

    def copy(row, dst_slot, r):
        return pltpu.make_async_copy(y_hbm.at[pl.ds(row, 1)], ybuf.at[dst_slot, pl.ds(r, 1)], sem.at[dst_slot])

    def start_gather(ref, dst_slot):
        def body(r, c):
            copy(ref[0, 0, r], dst_slot, r).start()
            return c
        lax.fori_loop(0, rows, body, 0)

    @pl.when(g == 0)
    def _():
        start_gather(idx_ref, 0)

    @pl.when(g + 1 < nb)
    def _():
        start_gather(nxt_ref, 1 - slot)

    def wait_body(r, c):
        copy(0, slot, r).wait()
        return c
    lax.fori_loop(0, rows, wait_body, 0)

    acc = x1_ref[...]
    for k in range(TOP_K):
        acc = acc + ybuf[slot, k * tm:(k + 1) * tm, :]
    o_ref[...] = acc


def _comb_call(dest, y, x1):
    n, d = x1.shape
    tm = TOKEN_TILE
    nb = n // tm
    idx3 = dest.reshape(TOP_K, nb, tm).transpose(1, 0, 2).reshape(nb, 1, TOP_K * tm)
    return pl.pallas_call(
        _comb_kernel,
        grid=(nb,),
        in_specs=[pl.BlockSpec((1, 1, TOP_K * tm), lambda g: (g, 0, 0), memory_space=pltpu.SMEM),
                  pl.BlockSpec((1, 1, TOP_K * tm), lambda g: (jnp.minimum(g + 1, nb - 1), 0, 0),
                               memory_space=pltpu.SMEM),
                  pl.BlockSpec(memory_space=pl.ANY),
                  pl.BlockSpec((tm, d), lambda g: (g, 0))],
        out_specs=pl.BlockSpec((tm, d), lambda g: (g, 0)),
        out_shape=jax.ShapeDtypeStruct((n, d), F32),
        scratch_shapes=[pltpu.VMEM((2, TOP_K * tm, d), F32), pltpu.SemaphoreType.DMA((2,))],
        compiler_params=pltpu.CompilerParams(dimension_semantics=("arbitrary",), vmem_limit_bytes=VMEM_LIMIT),
        name="comb",
    )(idx3, idx3, y, x1)


def _layer(x2, pos2, bsz, seq, g_mix, w_in, g_cq, w_uq, g_ckv, w_ukv, g_qnorm, g_knorm, w_o_sb, w_o_mla, w_out,
           g_ffn, w_router, b_router, w_up, b_up, w_down, b_down):
    n, d = x2.shape
    c_sb = 3 * SB_WIDTH
    c_q = c_sb + MLA_Q_LORA
    c_kv = c_q + MLA_KV_LORA
    c_kr = c_kv + MLA_ROPE
    w1 = w_in[:, :c_sb].astype(BF16)
    w_kr = jnp.pad(w_in[:, c_kv:c_kr], ((0, 0), (MLA_NOPE, HEAD_PAD - MLA_QK)))
    w2 = jnp.concatenate([w_in[:, c_sb:c_kv], w_kr], axis=1).astype(BF16)
    w3 = w_in[:, c_kr:].astype(BF16)
    wuq = jnp.pad(w_uq.reshape(MLA_Q_LORA, MLA_HEADS, MLA_QK), ((0, 0), (0, 0), (0, HEAD_PAD - MLA_QK)))
    wuq = wuq.reshape(MLA_Q_LORA, MLA_HEADS * HEAD_PAD).astype(BF16)
    wukv = w_ukv.reshape(MLA_KV_LORA, MLA_HEADS, MLA_NOPE + MLA_V)
    wuk = jnp.pad(wukv[:, :, :MLA_NOPE], ((0, 0), (0, 0), (0, HEAD_PAD - MLA_NOPE)))
    wuk = wuk.reshape(MLA_KV_LORA, MLA_HEADS * HEAD_PAD).astype(BF16)
    wuv = wukv[:, :, MLA_NOPE:].reshape(MLA_KV_LORA, MLA_HEADS * MLA_V).astype(BF16)
    gq = jnp.tile(jnp.pad(g_qnorm, (0, HEAD_PAD - MLA_QK)), MLA_HEADS)[None, :]
    gk = jnp.tile(jnp.pad(g_knorm, (0, HEAD_PAD - MLA_QK)), MLA_HEADS)[None, :]
    half = MLA_ROPE // 2
    inv_freq = ROPE_THETA ** (-jnp.arange(half, dtype=F32) / half)
    invf = jnp.pad(jnp.concatenate([inv_freq, inv_freq]), (MLA_NOPE, HEAD_PAD - MLA_QK))[None, :]
    sgn = jnp.pad(jnp.concatenate([-jnp.ones((half,), F32), jnp.ones((half,), F32)]),
                  (MLA_NOPE, HEAD_PAD - MLA_QK))[None, :]

    sb, qm, km, vm, gates = _pre_call(x2, pos2, g_mix[None, :], w1, w2, w3, g_cq[None, :], g_ckv[None, :],
                                      wuq, wuk, wuv, gq, gk, invf, sgn)
    o_sb = _sb_call(sb, bsz, seq)
    o_mla = _mla_call(qm, km, vm, bsz, seq)

    wr_t = w_router.T
    wr_hi = wr_t.astype(BF16)
    wr_lo = (wr_t - wr_hi.astype(F32)).astype(BF16)
    x1, h2, top_w, top_e, pos, cnt = _post_call(
        o_sb, o_mla, gates, x2, w_o_sb.astype(BF16), w_o_mla.astype(BF16), w_out.astype(BF16), g_ffn[None, :],
        wr_hi, wr_lo, b_router[:, None])

    t = EXPERT_TILE
    counts = cnt[:, 0].astype(jnp.int32)
    padded = (counts + t - 1) // t * t
    ends = jnp.cumsum(padded)
    pad_starts = ends - padded
    dest = pad_starts[top_e] + pos
    n_blocks = -(-(n * TOP_K + N_EXPERTS * (t - 1)) // t)
    n_rows = n_blocks * t
    tok = jnp.broadcast_to(jnp.arange(n, dtype=jnp.int32)[None, :], (TOP_K, n))
    row_tok = jnp.zeros((n_rows,), jnp.int32).at[dest.reshape(-1)].set(tok.reshape(-1))
    row_w = jnp.zeros((n_rows,), F32).at[dest.reshape(-1)].set(top_w.reshape(-1))
    block_start = jnp.arange(n_blocks, dtype=jnp.int32) * t
    block_expert = jnp.minimum(jnp.searchsorted(ends, block_start, side="right"), N_EXPERTS - 1).astype(jnp.int32)

    ff = w_down.shape[1]
    wup = jnp.concatenate([w_up[:, :, 0::2], w_up[:, :, 1::2]], axis=2).astype(BF16)
    bup = jnp.concatenate([b_up[:, 0::2], b_up[:, 1::2]], axis=1)[:, None, :]
    y = _moe_call(block_expert, row_tok, row_w, h2, wup, bup, w_down.astype(BF16), b_down[:, None, :])
    del ff
    return _comb_call(dest, y, x1)


def kernel(x, positions, g_mix, w_in, g_cq, w_uq, g_ckv, w_ukv, g_qnorm, g_knorm, w_o_sb, w_o_mla, w_out, g_ffn,
           w_router, b_router, w_up, b_up, w_down, b_down):
    bsz, seq, d = x.shape
    x2 = x.reshape(bsz * seq, d)
    pos2 = positions.reshape(bsz * seq, 1)
    for l in range(g_mix.shape[0]):
        x2 = _layer(x2, pos2, bsz, seq, g_mix[l], w_in[l], g_cq[l], w_uq[l], g_ckv[l], w_ukv[l], g_qnorm[l],
                    g_knorm[l], w_o_sb[l], w_o_mla[l], w_out[l], g_ffn[l], w_router[l], b_router[l], w_up[l],
                    b_up[l], w_down[l], b_down[l])
    return x2.reshape(bsz, seq, d)
```

```python
import functools
import math

import jax
import jax.numpy as jnp
from jax import lax
from jax.experimental import pallas as pl
from jax.experimental.pallas import tpu as pltpu

F32 = jnp.float32
BF16 = jnp.bfloat16

EPS = 1e-6
CHUNK = 64
SB_HEADS = 8
SB_DIM = 64
SB_WIDTH = SB_HEADS * SB_DIM
MLA_HEADS = 8
MLA_NOPE = 64
MLA_ROPE = 32
MLA_QK = MLA_NOPE + MLA_ROPE
MLA_V = 64
MLA_Q_LORA = 384
MLA_KV_LORA = 256
ROPE_THETA = 10000.0
N_EXPERTS = 32
TOP_K = 4
SWIGLU_LIMIT = 7.0
SWIGLU_ALPHA = 1.702

LANES = 128
HEAD_PAD = LANES
TOKEN_TILE = 256
ATTN_TILE = 256
EXPERT_TILE = 256
MLA_AHEAD = 4
SB_SKEW = 1
LOG2E = math.log2(math.e)
SB_DEAD = -160.0
VMEM_LIMIT = 48 * 1024 * 1024


def _nt_dot(a, b):
    return lax.dot_general(a, b, (((1,), (1,)), ((), ())), preferred_element_type=F32)


def _dot(a, b):
    return jnp.dot(a, b, preferred_element_type=F32)


def _rms(t, g):
    return t * lax.rsqrt(jnp.mean(t * t, axis=-1, keepdims=True) + EPS) * g


def _pre_kernel(x_ref, pos_ref, gmix_ref, w1_ref, w1v_ref, w2_ref, w3_ref, gcq_ref, gckv_ref, wuq_ref, wuk_ref,
                wuv_ref, gq_ref, gk_ref, invf_ref, sgn_ref,
                sb_ref, sbv_ref, qm_ref, km_ref, vm_ref, gate_ref):
    h = _rms(x_ref[...], gmix_ref[...]).astype(BF16)

    lat = _dot(h, w2_ref[...])
    cqn = _rms(lat[:, :MLA_Q_LORA], gcq_ref[...]).astype(BF16)
    ckvn = _rms(lat[:, MLA_Q_LORA:MLA_Q_LORA + MLA_KV_LORA], gckv_ref[...]).astype(BF16)
    kr = lat[:, MLA_Q_LORA + MLA_KV_LORA:]
    qf = _dot(cqn, wuq_ref[...])
    kf = _dot(ckvn, wuk_ref[...])
    vm_ref[...] = _nt_dot(wuv_ref[...], ckvn).astype(BF16)

    sb_ref[:, :SB_WIDTH] = (_dot(h, w1_ref[:, :SB_WIDTH]) * (1.0 / math.sqrt(SB_DIM))).astype(BF16)
    sb_ref[:, SB_WIDTH:] = _dot(h, w1_ref[:, SB_WIDTH:]).astype(BF16)
    sbv_ref[...] = _nt_dot(w1v_ref[...], h).astype(BF16)

    gate_ref[...] = jax.nn.sigmoid(_dot(h, w3_ref[...])).astype(BF16)

    ang = pos_ref[...].astype(F32) * invf_ref[...]
    cos = jnp.cos(ang)
    sin = jnp.sin(ang) * sgn_ref[...]
    lane = lax.broadcasted_iota(jnp.int32, (1, LANES), 1)
    first_half = lane < MLA_NOPE + MLA_ROPE // 2

    def rope(t):
        partner = jnp.where(first_half, pltpu.roll(t, LANES - MLA_ROPE // 2, 1), pltpu.roll(t, MLA_ROPE // 2, 1))
        return t * cos + partner * sin

    def head_norm(t, g):
        return t * lax.rsqrt(jnp.sum(t * t, axis=-1, keepdims=True) * (1.0 / MLA_QK) + EPS) * g

    mla_scale = LOG2E / math.sqrt(MLA_QK)
    for hd in range(MLA_HEADS):
        sl = slice(hd * HEAD_PAD, (hd + 1) * HEAD_PAD)
        qm_ref[:, sl] = (rope(head_norm(qf[:, sl], gq_ref[:, sl])) * mla_scale).astype(BF16)
        km_ref[:, sl] = rope(head_norm(kf[:, sl] + kr, gk_ref[:, sl])).astype(BF16)


def _pre_call(x2, pos2, bsz, seq, gmix, w1, w1v, w2, w3, gcq, gckv, wuq, wuk, wuv, gq, gk, invf, sgn):
    n, d = x2.shape
    tm = TOKEN_TILE
    assert tm == ATTN_TILE and seq % tm == 0
    nt = seq // tm
    const = lambda i: (0, 0)
    full = lambda a: pl.BlockSpec(a.shape, const)
    row = lambda w: pl.BlockSpec((tm, w), lambda i: (i, 0))
    vt_spec = lambda w: pl.BlockSpec((None, None, w, tm), lambda i: (i // nt, i % nt, 0, 0))
    return pl.pallas_call(
        _pre_kernel,
        grid=(n // tm,),
        in_specs=[row(d), row(1), full(gmix), full(w1), full(w1v), full(w2), full(w3), full(gcq), full(gckv),
                  full(wuq), full(wuk), full(wuv), full(gq), full(gk), full(invf), full(sgn)],
        out_specs=[row(2 * SB_WIDTH), vt_spec(SB_WIDTH), row(MLA_HEADS * HEAD_PAD), row(MLA_HEADS * HEAD_PAD),
                   vt_spec(MLA_HEADS * MLA_V), row(2 * d)],
        out_shape=[jax.ShapeDtypeStruct((n, 2 * SB_WIDTH), BF16),
                   jax.ShapeDtypeStruct((bsz, nt, SB_WIDTH, tm), BF16),
                   jax.ShapeDtypeStruct((n, MLA_HEADS * HEAD_PAD), BF16),
                   jax.ShapeDtypeStruct((n, MLA_HEADS * HEAD_PAD), BF16),
                   jax.ShapeDtypeStruct((bsz, nt, MLA_HEADS * MLA_V, tm), BF16),
                   jax.ShapeDtypeStruct((n, 2 * d), BF16)],
        compiler_params=pltpu.CompilerParams(dimension_semantics=("arbitrary",), vmem_limit_bytes=VMEM_LIMIT),
        name="pre",
    )(x2, pos2, gmix, w1, w1v, w2, w3, gcq, gckv, wuq, wuk, wuv, gq, gk, invf, sgn)


def _sb_kernel(q_ref, k_ref, v_ref, o_ref, acc_ref, car_ref):
    t = ATTN_TILE
    i = pl.program_id(1)
    lane = lax.broadcasted_iota(jnp.int32, (1, LANES), 1)
    lo = lane < SB_DIM
    rr = lax.broadcasted_iota(jnp.int32, (t, t), 0)
    cc = lax.broadcasted_iota(jnp.int32, (t, t), 1)
    later_sum = (cc > rr).astype(BF16)
    strict = rr < cc

    acc_ref[...] = jnp.zeros_like(acc_ref)
    car_ref[...] = jnp.zeros_like(car_ref)

    def tile(j, masked):
        krows = pl.ds(pl.multiple_of(j * t, t), t)

        def scores(hd):
            pair = slice((hd // 2) * LANES, (hd // 2 + 1) * LANES)
            q2 = q_ref[:, pair]
            zero = jnp.zeros_like(q2)
            qh = jnp.where(lo, q2, zero) if hd % 2 == 0 else jnp.where(lo, zero, q2)
            return _nt_dot(k_ref[krows, pair], qh)

        def stay(z):
            z = z * LOG2E
            sp = jnp.log2(1.0 + jnp.exp2(-jnp.abs(z)))
            log_beta = jnp.minimum(z, 0.0) - sp
            log_stay = log_beta - z
            if masked:
                log_stay = jnp.where(strict, log_stay, 0.0)
            hi = log_stay.astype(BF16)
            lo_part = (log_stay - hi.astype(F32)).astype(BF16)
            later = _dot(later_sum, hi) + _dot(later_sum, lo_part)
            return log_beta, later, jnp.sum(log_stay, axis=0, keepdims=True)

        def weigh(hd, log_beta, later, total):
            carry = car_ref[hd]
            a = jnp.exp2(log_beta + later + carry)
            if masked:
                a = jnp.where(strict, a, 0.0)
            rows = slice(hd * SB_DIM, (hd + 1) * SB_DIM)
            acc_ref[rows, :] += _dot(v_ref[j, rows, :], a.astype(BF16))
            car_ref[hd] = carry + total

        zs, mids = {}, {}
        for step in range(SB_HEADS + 2 * SB_SKEW):
            if step < SB_HEADS:
                zs[step] = scores(step)
            hd = step - SB_SKEW
            if 0 <= hd < SB_HEADS:
                mids[hd] = stay(zs.pop(hd))
            hd = step - 2 * SB_SKEW
            if 0 <= hd < SB_HEADS:
                weigh(hd, *mids.pop(hd))

    tile(i, True)

    def cond(c):
        j, alive = c
        return jnp.logical_and(j >= 0, alive > SB_DEAD)

    def body(c):
        j, _ = c
        tile(j, False)
        return j - 1, jnp.max(car_ref[...])

    lax.while_loop(cond, body, (i - 1, jnp.max(car_ref[...])))
    o_ref[...] = acc_ref[...].astype(o_ref.dtype)


def _sb_call(sb, sbv, bsz, seq):
    t = ATTN_TILE
    nq = seq // t
    return pl.pallas_call(
        _sb_kernel,
        grid=(bsz, nq),
        in_specs=[pl.BlockSpec((t, SB_WIDTH), lambda b, i: (b * nq + i, 0)),
                  pl.BlockSpec((seq, SB_WIDTH), lambda b, i: (b, 1)),
                  pl.BlockSpec((None, nq, SB_WIDTH, t), lambda b, i: (b, 0, 0, 0))],
        out_specs=pl.BlockSpec((None, SB_WIDTH, t), lambda b, i: (b, 0, i)),
        out_shape=jax.ShapeDtypeStruct((bsz, SB_WIDTH, seq), BF16),
        scratch_shapes=[pltpu.VMEM((SB_WIDTH, t), F32), pltpu.VMEM((SB_HEADS, 1, t), F32)],
        compiler_params=pltpu.CompilerParams(dimension_semantics=("arbitrary",) * 2, vmem_limit_bytes=VMEM_LIMIT),
        name="sb_attn",
    )(sb, sb, sbv)


def _mla_kernel(q_ref, k_ref, v_ref, o_ref, m_ref, l_ref, acc_ref):
    t = ATTN_TILE
    i = pl.program_id(1)
    rr = lax.broadcasted_iota(jnp.int32, (t, t), 0)
    cc = lax.broadcasted_iota(jnp.int32, (t, t), 1)
    visible = (rr // CHUNK) <= (cc // CHUNK)

    m_ref[...] = jnp.full_like(m_ref, -jnp.inf)
    l_ref[...] = jnp.zeros_like(l_ref)
    acc_ref[...] = jnp.zeros_like(acc_ref)

    def tile(j, masked):
        krows = pl.ds(pl.multiple_of(j * t, t), t)

        def scores(hd):
            sl = slice(hd * HEAD_PAD, (hd + 1) * HEAD_PAD)
            return _nt_dot(k_ref[krows, sl], q_ref[:, sl])

        pending = [scores(hd) for hd in range(MLA_AHEAD)]
        for hd in range(MLA_HEADS):
            if hd + MLA_AHEAD < MLA_HEADS:
                pending.append(scores(hd + MLA_AHEAD))
            s = pending[hd]
            if masked:
                s = jnp.where(visible, s, -jnp.inf)
            m_old = m_ref[hd]
            m_new = jnp.maximum(m_old, jnp.max(s, axis=0, keepdims=True))
            alpha = jnp.exp2(m_old - m_new)
            p = jnp.exp2(s - m_new)
            l_ref[hd] = alpha * l_ref[hd] + jnp.sum(p, axis=0, keepdims=True)
            rows = slice(hd * MLA_V, (hd + 1) * MLA_V)
            acc_ref[rows, :] = alpha * acc_ref[rows, :] + _dot(v_ref[j, rows, :], p.astype(BF16))
            m_ref[hd] = m_new

    def body(j, c):
        tile(j, False)
        return c

    lax.fori_loop(0, i, body, 0)
    tile(i, True)
    for hd in range(MLA_HEADS):
        rows = slice(hd * MLA_V, (hd + 1) * MLA_V)
        o_ref[rows, :] = (acc_ref[rows, :] / l_ref[hd]).astype(o_ref.dtype)


def _mla_call(qm, km, vm, bsz, seq):
    t = ATTN_TILE
    nq = seq // t
    width = MLA_HEADS * HEAD_PAD
    return pl.pallas_call(
        _mla_kernel,
        grid=(bsz, nq),
        in_specs=[pl.BlockSpec((t, width), lambda b, i: (b * nq + i, 0)),
                  pl.BlockSpec((seq, width), lambda b, i: (b, 0)),
                  pl.BlockSpec((None, nq, MLA_HEADS * MLA_V, t), lambda b, i: (b, 0, 0, 0))],
        out_specs=pl.BlockSpec((None, MLA_HEADS * MLA_V, t), lambda b, i: (b, 0, i)),
        out_shape=jax.ShapeDtypeStruct((bsz, MLA_HEADS * MLA_V, seq), BF16),
        scratch_shapes=[pltpu.VMEM((MLA_HEADS, 1, t), F32), pltpu.VMEM((MLA_HEADS, 1, t), F32),
                        pltpu.VMEM((MLA_HEADS * MLA_V, t), F32)],
        compiler_params=pltpu.CompilerParams(dimension_semantics=("arbitrary",) * 2, vmem_limit_bytes=VMEM_LIMIT),
        name="mla_attn",
    )(qm, km, vm)


def _post_kernel(osb_ref, omla_ref, gate_ref, x_ref, wosb_ref, womla_ref, wout_ref, gffn_ref, wrh_ref, wrl_ref,
                 br_ref, x1_ref, h2_ref, topw_ref, tope_ref, pos_ref, cnt_ref, run_ref):
    tm = TOKEN_TILE
    d = x_ref.shape[1]

    @pl.when(pl.program_id(0) == 0)
    def _():
        run_ref[...] = jnp.zeros_like(run_ref)

    tn = (((0,), (0,)), ((), ()))
    a = lax.dot_general(osb_ref[...], wosb_ref[...], tn, preferred_element_type=F32)
    b = lax.dot_general(omla_ref[...], womla_ref[...], tn, preferred_element_type=F32)
    mixed = gate_ref[:, :d].astype(F32) * a + gate_ref[:, d:].astype(F32) * b
    x1 = x_ref[...] + _dot(mixed.astype(BF16), wout_ref[...])
    x1_ref[...] = x1
    h2 = _rms(x1, gffn_ref[...])
    h2_ref[...] = h2

    h_hi = h2.astype(BF16)
    h_lo = (h2 - h_hi.astype(F32)).astype(BF16)
    logits = (_nt_dot(wrh_ref[...], h_hi) + _nt_dot(wrh_ref[...], h_lo) + _nt_dot(wrl_ref[...], h_hi)
              + br_ref[...])

    eidx = lax.broadcasted_iota(jnp.int32, (N_EXPERTS, tm), 0)
    work = logits
    tops, sels = [], []
    for _ in range(TOP_K):
        mk = jnp.max(work, axis=0, keepdims=True)
        ik = jnp.min(jnp.where(work == mk, eidx, N_EXPERTS), axis=0, keepdims=True)
        sel = eidx == ik
        work = jnp.where(sel, -jnp.inf, work)
        tops.append(mk)
        sels.append(sel)
    exps = [jnp.exp(mk - tops[0]) for mk in tops]
    denom = exps[0] + exps[1] + exps[2] + exps[3]

    chosen = jnp.zeros((N_EXPERTS, tm), F32)
    for sel in sels:
        chosen = chosen + sel.astype(F32)
    rr = lax.broadcasted_iota(jnp.int32, (tm, tm), 0)
    cc = lax.broadcasted_iota(jnp.int32, (tm, tm), 1)
    before = (rr < cc).astype(BF16)
    rank = _dot(chosen.astype(BF16), before) + run_ref[...]
    for k in range(TOP_K):
        topw_ref[k:k + 1, :] = exps[k] / denom
        tope_ref[k:k + 1, :] = jnp.sum(jnp.where(sels[k], eidx, 0), axis=0, keepdims=True)
        pos_ref[k:k + 1, :] = jnp.sum(jnp.where(sels[k], rank, 0.0), axis=0, keepdims=True).astype(jnp.int32)
    run_ref[...] = run_ref[...] + jnp.sum(chosen, axis=1, keepdims=True)
    cnt_ref[...] = jnp.broadcast_to(run_ref[...], cnt_ref.shape)


def _post_call(osb, omla, gates, x2, wosb, womla, wout, gffn, wrh, wrl, br):
    n, d = x2.shape
    tm = TOKEN_TILE
    nt = osb.shape[2] // tm
    const = lambda i: (0, 0)
    full = lambda a: pl.BlockSpec(a.shape, const)
    row = lambda w: pl.BlockSpec((tm, w), lambda i: (i, 0))
    col = lambda: pl.BlockSpec((TOP_K, tm), lambda i: (0, i))
    feat = lambda a: pl.BlockSpec((None, a.shape[1], tm), lambda i: (i // nt, 0, i % nt))
    return pl.pallas_call(
        _post_kernel,
        grid=(n // tm,),
        in_specs=[feat(osb), feat(omla), row(2 * d), row(d), full(wosb), full(womla), full(wout),
                  full(gffn), full(wrh), full(wrl), full(br)],
        out_specs=[row(d), row(d), col(), col(), col(), pl.BlockSpec((N_EXPERTS, LANES), const)],
        out_shape=[jax.ShapeDtypeStruct((n, d), F32), jax.ShapeDtypeStruct((n, d), F32),
                   jax.ShapeDtypeStruct((TOP_K, n), F32), jax.ShapeDtypeStruct((TOP_K, n), jnp.int32),
                   jax.ShapeDtypeStruct((TOP_K, n), jnp.int32), jax.ShapeDtypeStruct((N_EXPERTS, LANES), F32)],
        scratch_shapes=[pltpu.VMEM((N_EXPERTS, 1), F32)],
        compiler_params=pltpu.CompilerParams(dimension_semantics=("arbitrary",), vmem_limit_bytes=VMEM_LIMIT),
        name="post",
    )(osb, omla, gates, x2, wosb, womla, wout, gffn, wrh, wrl, br)


DMA_UNROLL = 8


def _disp_kernel(idx_ref, h2_ref, xs_hbm, sem):
    tm = TOKEN_TILE

    def copy(r, k, row):
        return pltpu.make_async_copy(h2_ref.at[pl.ds(r, 1)], xs_hbm.at[pl.ds(row, 1)], sem)

    def start_body(r, c):
        for k in range(TOP_K):
            copy(r, k, idx_ref[0, 0, k * tm + r]).start()
        return c
    lax.fori_loop(0, tm, start_body, 0, unroll=DMA_UNROLL)

    def wait_body(r, c):
        for k in range(TOP_K):
            copy(r, k, 0).wait()
        return c
    lax.fori_loop(0, tm, wait_body, 0, unroll=DMA_UNROLL)


def _disp_call(dest3, h2):
    n, d = h2.shape
    tm = TOKEN_TILE
    return pl.pallas_call(
        _disp_kernel,
        grid=(n // tm,),
        in_specs=[pl.BlockSpec((1, 1, TOP_K * tm), lambda g: (g, 0, 0), memory_space=pltpu.SMEM),
                  pl.BlockSpec((tm, d), lambda g: (g, 0))],
        out_specs=pl.BlockSpec(memory_space=pl.ANY),
        out_shape=jax.ShapeDtypeStruct((n * TOP_K, d), F32),
        scratch_shapes=[pltpu.SemaphoreType.DMA],
        compiler_params=pltpu.CompilerParams(dimension_semantics=("arbitrary",), vmem_limit_bytes=VMEM_LIMIT),
        name="disp",
    )(dest3, h2)


def _moe_kernel(blk_ref, exp_ref, lo_ref, hi_ref, first_ref, xs_ref, wup_ref, bup_ref, wdn_ref, bdn_ref, y_ref):
    t = EXPERT_TILE
    w = pl.program_id(0)
    lo = lo_ref[w]
    hi = hi_ref[w]
    half = wdn_ref.shape[0]

    @pl.when(hi > lo)
    def _():
        x = xs_ref[...].astype(BF16)
        gu = _dot(x, wup_ref[...]) + bup_ref[...]
        lane = lax.broadcasted_iota(jnp.int32, (1, LANES), 1)
        even = (lane % 2) == 0
        acts = []
        for c in range(half // LANES):
            a = gu[:, c * LANES:(c + 1) * LANES]
            b = gu[:, half + c * LANES:half + (c + 1) * LANES]
            x_glu = jnp.minimum(jnp.where(even, a, pltpu.roll(b, 1, 1)), SWIGLU_LIMIT)
            x_lin = jnp.clip(jnp.where(even, pltpu.roll(a, LANES - 1, 1), b), -SWIGLU_LIMIT, SWIGLU_LIMIT)
            acts.append((x_glu * jax.nn.sigmoid(SWIGLU_ALPHA * x_glu) * (x_lin + 1.0)).astype(BF16))
        act = jnp.concatenate(acts, axis=1)
        y = _dot(act, wdn_ref[...]) + bdn_ref[...]
        row = lax.broadcasted_iota(jnp.int32, (t, 1), 0)
        mine = jnp.logical_and(row >= lo, row < hi)
        keep = jnp.where(first_ref[w] == 1, jnp.zeros_like(y), y_ref[...])
        y_ref[...] = jnp.where(mine, y, keep)


def _moe_call(items, xs, wup, bup, wdn, bdn):
    t = EXPERT_TILE
    n_rows, d = xs.shape
    ff = wdn.shape[1]
    n_items = items[0].shape[0]
    grid_spec = pltpu.PrefetchScalarGridSpec(
        num_scalar_prefetch=5,
        grid=(n_items,),
        in_specs=[pl.BlockSpec((t, d), lambda w, blk, ex, lo, hi, fi: (blk[w], 0)),
                  pl.BlockSpec((None, d, 2 * ff), lambda w, blk, ex, lo, hi, fi: (ex[w], 0, 0)),
                  pl.BlockSpec((None, 1, 2 * ff), lambda w, blk, ex, lo, hi, fi: (ex[w], 0, 0)),
                  pl.BlockSpec((None, ff, d), lambda w, blk, ex, lo, hi, fi: (ex[w], 0, 0)),
                  pl.BlockSpec((None, 1, d), lambda w, blk, ex, lo, hi, fi: (ex[w], 0, 0))],
        out_specs=pl.BlockSpec((t, d), lambda w, blk, ex, lo, hi, fi: (blk[w], 0)),
    )
    return pl.pallas_call(
        _moe_kernel,
        grid_spec=grid_spec,
        out_shape=jax.ShapeDtypeStruct((n_rows, d), F32),
        compiler_params=pltpu.CompilerParams(dimension_semantics=("arbitrary",), vmem_limit_bytes=VMEM_LIMIT),
        name="moe",
    )(*items, xs, wup, bup, wdn, bdn)


def _moe_items(counts, n_rows):
    t = EXPERT_TILE
    nb = n_rows // t
    n_items = nb + N_EXPERTS - 1
    ends = jnp.cumsum(counts)
    starts = ends - counts
    first_blk = starts // t
    tiles = jnp.where(counts > 0, (ends - 1) // t - first_blk + 1, 0)
    item_end = jnp.cumsum(tiles)
    item_start = item_end - tiles
    total = item_end[-1]
    w = jnp.arange(n_items, dtype=jnp.int32)
    wc = jnp.minimum(w, total - 1)
    ex = jnp.sum((item_end[None, :] <= wc[:, None]).astype(jnp.int32), axis=1)
    onehot = (ex[:, None] == jnp.arange(N_EXPERTS, dtype=jnp.int32)[None, :]).astype(jnp.int32)
    pick = lambda v: jnp.sum(onehot * v[None, :], axis=1)
    blk = pick(first_blk) + wc - pick(item_start)
    lo = jnp.clip(pick(starts) - blk * t, 0, t)
    hi = jnp.where(w < total, jnp.clip(pick(ends) - blk * t, 0, t), lo)
    first = jnp.concatenate([jnp.ones((1,), jnp.int32), (blk[1:] != blk[:-1]).astype(jnp.int32)])
    return blk.astype(jnp.int32), ex.astype(jnp.int32), lo.astype(jnp.int32), hi.astype(jnp.int32), first


def _comb_kernel(idx_ref, nxt_ref, y_hbm, x1_ref, w_ref, o_ref, ybuf, sem):
    tm = TOKEN_TILE
    g = pl.program_id(0)
    nb = pl.num_programs(0)
    slot = g % 2
    rows = TOP_K * tm

    def copy(row, dst_slot, r):
        return pltpu.make_async_copy(y_hbm.at[pl.ds(row, 1)], ybuf.at[dst_slot, pl.ds(r, 1)], sem.at[dst_slot])

    def start_gather(ref, dst_slot):
        def body(r, c):
            copy(ref[0, 0, r], dst_slot, r).start()
            return c
        lax.fori_loop(0, rows, body, 0, unroll=DMA_UNROLL)

    @pl.when(g == 0)
    def _():
        start_gather(idx_ref, 0)

    @pl.when(g + 1 < nb)
    def _():
        start_gather(nxt_ref, 1 - slot)

    def wait_body(r, c):
        copy(0, slot, r).wait()
        return c
    lax.fori_loop(0, rows, wait_body, 0, unroll=DMA_UNROLL)

    acc = x1_ref[...]
    for k in range(TOP_K):
        acc = acc + w_ref[:, k:k + 1] * ybuf[slot, k * tm:(k + 1) * tm, :]
    o_ref[...] = acc


def _comb_call(dest3, y, x1, w_tok):
    n, d = x1.shape
    tm = TOKEN_TILE
    nb = n // tm
    return pl.pallas_call(
        _comb_kernel,
        grid=(nb,),
        in_specs=[pl.BlockSpec((1, 1, TOP_K * tm), lambda g: (g, 0, 0), memory_space=pltpu.SMEM),
                  pl.BlockSpec((1, 1, TOP_K * tm), lambda g: (jnp.minimum(g + 1, nb - 1), 0, 0),
                               memory_space=pltpu.SMEM),
                  pl.BlockSpec(memory_space=pl.ANY),
                  pl.BlockSpec((tm, d), lambda g: (g, 0)),
                  pl.BlockSpec((tm, TOP_K), lambda g: (g, 0))],
        out_specs=pl.BlockSpec((tm, d), lambda g: (g, 0)),
        out_shape=jax.ShapeDtypeStruct((n, d), F32),
        scratch_shapes=[pltpu.VMEM((2, TOP_K * tm, d), F32), pltpu.SemaphoreType.DMA((2,))],
        compiler_params=pltpu.CompilerParams(dimension_semantics=("arbitrary",), vmem_limit_bytes=VMEM_LIMIT),
        name="comb",
    )(dest3, dest3, y, x1, w_tok)


def _layer(x2, pos2, bsz, seq, g_mix, w_in, g_cq, w_uq, g_ckv, w_ukv, g_qnorm, g_knorm, w_o_sb, w_o_mla, w_out,
           g_ffn, w_router, b_router, w_up, b_up, w_down, b_down):
    n, d = x2.shape
    c_sb = 3 * SB_WIDTH
    c_q = c_sb + MLA_Q_LORA
    c_kv = c_q + MLA_KV_LORA
    c_kr = c_kv + MLA_ROPE
    w1 = w_in[:, :2 * SB_WIDTH].astype(BF16)
    w1v = w_in[:, 2 * SB_WIDTH:c_sb].T.astype(BF16)
    w_kr = jnp.pad(w_in[:, c_kv:c_kr], ((0, 0), (MLA_NOPE, HEAD_PAD - MLA_QK)))
    w2 = jnp.concatenate([w_in[:, c_sb:c_kv], w_kr], axis=1).astype(BF16)
    w3 = w_in[:, c_kr:].astype(BF16)
    wuq = jnp.pad(w_uq.reshape(MLA_Q_LORA, MLA_HEADS, MLA_QK), ((0, 0), (0, 0), (0, HEAD_PAD - MLA_QK)))
    wuq = wuq.reshape(MLA_Q_LORA, MLA_HEADS * HEAD_PAD).astype(BF16)
    wukv = w_ukv.reshape(MLA_KV_LORA, MLA_HEADS, MLA_NOPE + MLA_V)
    wuk = jnp.pad(wukv[:, :, :MLA_NOPE], ((0, 0), (0, 0), (0, HEAD_PAD - MLA_NOPE)))
    wuk = wuk.reshape(MLA_KV_LORA, MLA_HEADS * HEAD_PAD).astype(BF16)
    wuv = wukv[:, :, MLA_NOPE:].reshape(MLA_KV_LORA, MLA_HEADS * MLA_V).T.astype(BF16)
    gq = jnp.tile(jnp.pad(g_qnorm, (0, HEAD_PAD - MLA_QK)), MLA_HEADS)[None, :]
    gk = jnp.tile(jnp.pad(g_knorm, (0, HEAD_PAD - MLA_QK)), MLA_HEADS)[None, :]
    half = MLA_ROPE // 2
    inv_freq = ROPE_THETA ** (-jnp.arange(half, dtype=F32) / half)
    invf = jnp.pad(jnp.concatenate([inv_freq, inv_freq]), (MLA_NOPE, HEAD_PAD - MLA_QK))[None, :]
    sgn = jnp.pad(jnp.concatenate([-jnp.ones((half,), F32), jnp.ones((half,), F32)]),
                  (MLA_NOPE, HEAD_PAD - MLA_QK))[None, :]

    sb, sbv, qm, km, vm, gates = _pre_call(x2, pos2, bsz, seq, g_mix[None, :], w1, w1v, w2, w3, g_cq[None, :],
                                           g_ckv[None, :], wuq, wuk, wuv, gq, gk, invf, sgn)
    o_sb = _sb_call(sb, sbv, bsz, seq)
    o_mla = _mla_call(qm, km, vm, bsz, seq)

    wr_t = w_router.T
    wr_hi = wr_t.astype(BF16)
    wr_lo = (wr_t - wr_hi.astype(F32)).astype(BF16)
    x1, h2, top_w, top_e, pos, cnt = _post_call(
        o_sb, o_mla, gates, x2, w_o_sb.astype(BF16), w_o_mla.astype(BF16), w_out.astype(BF16), g_ffn[None, :],
        wr_hi, wr_lo, b_router[:, None])

    counts = cnt[:, 0].astype(jnp.int32)
    starts = jnp.cumsum(counts) - counts
    e_iota = jnp.arange(N_EXPERTS, dtype=jnp.int32)[None, None, :]
    dest = jnp.sum(jnp.where(top_e[:, :, None] == e_iota, starts[None, None, :], 0), axis=2) + pos
    tm = TOKEN_TILE
    dest3 = dest.reshape(TOP_K, n // tm, tm).transpose(1, 0, 2).reshape(n // tm, 1, TOP_K * tm)

    xs = _disp_call(dest3, h2)
    ff = w_down.shape[1]
    wdn = w_down.reshape(N_EXPERTS, 2, ff // 2, d).transpose(0, 2, 1, 3).reshape(N_EXPERTS, ff, d).astype(BF16)
    y = _moe_call(_moe_items(counts, n * TOP_K), xs, w_up.astype(BF16), b_up[:, None, :], wdn, b_down[:, None, :])
    return _comb_call(dest3, y, x1, top_w.T)


def kernel(x, positions, g_mix, w_in, g_cq, w_uq, g_ckv, w_ukv, g_qnorm, g_knorm, w_o_sb, w_o_mla, w_out, g_ffn,
           w_router, b_router, w_up, b_up, w_down, b_down):
    bsz, seq, d = x.shape
    x2 = x.reshape(bsz * seq, d)
    pos2 = positions.reshape(bsz * seq, 1)
    for l in range(g_mix.shape[0]):
        x2 = _layer(x2, pos2, bsz, seq, g_mix[l], w_in[l], g_cq[l], w_uq[l], g_ckv[l], w_ukv[l], g_qnorm[l],
                    g_knorm[l], w_o_sb[l], w_o_mla[l], w_out[l], g_ffn[l], w_router[l], b_router[l], w_up[l],
                    b_up[l], w_down[l], b_down[l])
    return x2.reshape(bsz, seq, d)
```

```python
import functools
import math

import jax
import jax.numpy as jnp
from jax import lax
from jax.experimental import pallas as pl
from jax.experimental.pallas import tpu as pltpu

F32 = jnp.float32
BF16 = jnp.bfloat16

EPS = 1e-6
CHUNK = 64
SB_HEADS = 8
SB_DIM = 64
SB_WIDTH = SB_HEADS * SB_DIM
MLA_HEADS = 8
MLA_NOPE = 64
MLA_ROPE = 32
MLA_QK = MLA_NOPE + MLA_ROPE
MLA_V = 64
MLA_Q_LORA = 384
MLA_KV_LORA = 256
ROPE_THETA = 10000.0
N_EXPERTS = 32
TOP_K = 4
SWIGLU_LIMIT = 7.0
SWIGLU_ALPHA = 1.702

LANES = 128
HEAD_PAD = LANES
TOKEN_TILE = 256
ATTN_TILE = 256
EXPERT_TILE = 256
MLA_AHEAD = 4
SB_SKEW = 1
LOG2E = math.log2(math.e)
SB_DEAD = -160.0
VMEM_LIMIT = 48 * 1024 * 1024
MOE_VMEM_LIMIT = 56 * 1024 * 1024


def _nt_dot(a, b):
    return lax.dot_general(a, b, (((1,), (1,)), ((), ())), preferred_element_type=F32)


def _dot(a, b):
    return jnp.dot(a, b, preferred_element_type=F32)


def _rms(t, g):
    return t * lax.rsqrt(jnp.mean(t * t, axis=-1, keepdims=True) + EPS) * g


def _pre_kernel(x_ref, pos_ref, gmix_ref, w1_ref, w1v_ref, w2_ref, w3_ref, gcq_ref, gckv_ref, wuq_ref, wuk_ref,
                wuv_ref, gq_ref, gk_ref, invf_ref, sgn_ref,
                sb_ref, sbv_ref, qm_ref, km_ref, vm_ref, gate_ref):
    h = _rms(x_ref[...], gmix_ref[...]).astype(BF16)

    lat = _dot(h, w2_ref[...])
    cqn = _rms(lat[:, :MLA_Q_LORA], gcq_ref[...]).astype(BF16)
    ckvn = _rms(lat[:, MLA_Q_LORA:MLA_Q_LORA + MLA_KV_LORA], gckv_ref[...]).astype(BF16)
    kr = lat[:, MLA_Q_LORA + MLA_KV_LORA:]
    qf = _dot(cqn, wuq_ref[...])
    kf = _dot(ckvn, wuk_ref[...])
    vm_ref[...] = _nt_dot(wuv_ref[...], ckvn).astype(BF16)

    sb_ref[:, :SB_WIDTH] = (_dot(h, w1_ref[:, :SB_WIDTH]) * (1.0 / math.sqrt(SB_DIM))).astype(BF16)
    sb_ref[:, SB_WIDTH:] = _dot(h, w1_ref[:, SB_WIDTH:]).astype(BF16)
    sbv_ref[...] = _nt_dot(w1v_ref[...], h).astype(BF16)

    gate_ref[...] = jax.nn.sigmoid(_dot(h, w3_ref[...])).astype(BF16)

    ang = pos_ref[...].astype(F32) * invf_ref[...]
    cos = jnp.cos(ang)
    sin = jnp.sin(ang) * sgn_ref[...]
    lane = lax.broadcasted_iota(jnp.int32, (1, LANES), 1)
    first_half = lane < MLA_NOPE + MLA_ROPE // 2

    def rope(t):
        partner = jnp.where(first_half, pltpu.roll(t, LANES - MLA_ROPE // 2, 1), pltpu.roll(t, MLA_ROPE // 2, 1))
        return t * cos + partner * sin

    def head_norm(t, g):
        return t * lax.rsqrt(jnp.sum(t * t, axis=-1, keepdims=True) * (1.0 / MLA_QK) + EPS) * g

    mla_scale = LOG2E / math.sqrt(MLA_QK)
    for hd in range(MLA_HEADS):
        sl = slice(hd * HEAD_PAD, (hd + 1) * HEAD_PAD)
        qm_ref[:, sl] = (rope(head_norm(qf[:, sl], gq_ref[:, sl])) * mla_scale).astype(BF16)
        km_ref[:, sl] = rope(head_norm(kf[:, sl] + kr, gk_ref[:, sl])).astype(BF16)


def _pre_call(x2, pos2, bsz, seq, gmix, w1, w1v, w2, w3, gcq, gckv, wuq, wuk, wuv, gq, gk, invf, sgn):
    n, d = x2.shape
    tm = TOKEN_TILE
    assert tm == ATTN_TILE and seq % tm == 0
    nt = seq // tm
    const = lambda i: (0, 0)
    full = lambda a: pl.BlockSpec(a.shape, const)
    row = lambda w: pl.BlockSpec((tm, w), lambda i: (i, 0))
    vt_spec = lambda w: pl.BlockSpec((None, None, w, tm), lambda i: (i // nt, i % nt, 0, 0))
    return pl.pallas_call(
        _pre_kernel,
        grid=(n // tm,),
        in_specs=[row(d), row(1), full(gmix), full(w1), full(w1v), full(w2), full(w3), full(gcq), full(gckv),
                  full(wuq), full(wuk), full(wuv), full(gq), full(gk), full(invf), full(sgn)],
        out_specs=[row(2 * SB_WIDTH), vt_spec(SB_WIDTH), row(MLA_HEADS * HEAD_PAD), row(MLA_HEADS * HEAD_PAD),
                   vt_spec(MLA_HEADS * MLA_V), row(2 * d)],
        out_shape=[jax.ShapeDtypeStruct((n, 2 * SB_WIDTH), BF16),
                   jax.ShapeDtypeStruct((bsz, nt, SB_WIDTH, tm), BF16),
                   jax.ShapeDtypeStruct((n, MLA_HEADS * HEAD_PAD), BF16),
                   jax.ShapeDtypeStruct((n, MLA_HEADS * HEAD_PAD), BF16),
                   jax.ShapeDtypeStruct((bsz, nt, MLA_HEADS * MLA_V, tm), BF16),
                   jax.ShapeDtypeStruct((n, 2 * d), BF16)],
        compiler_params=pltpu.CompilerParams(dimension_semantics=("arbitrary",), vmem_limit_bytes=VMEM_LIMIT),
        name="pre",
    )(x2, pos2, gmix, w1, w1v, w2, w3, gcq, gckv, wuq, wuk, wuv, gq, gk, invf, sgn)


def _sb_kernel(q_ref, k_ref, v_ref, o_ref, acc_ref, car_ref):
    t = ATTN_TILE
    i = pl.program_id(1)
    lane = lax.broadcasted_iota(jnp.int32, (1, LANES), 1)
    lo = lane < SB_DIM
    rr = lax.broadcasted_iota(jnp.int32, (t, t), 0)
    cc = lax.broadcasted_iota(jnp.int32, (t, t), 1)
    later_sum = (cc > rr).astype(BF16)
    strict = rr < cc

    acc_ref[...] = jnp.zeros_like(acc_ref)
    car_ref[...] = jnp.zeros_like(car_ref)

    def tile(j, masked):
        krows = pl.ds(pl.multiple_of(j * t, t), t)

        def scores(hd):
            pair = slice((hd // 2) * LANES, (hd // 2 + 1) * LANES)
            q2 = q_ref[:, pair]
            zero = jnp.zeros_like(q2)
            qh = jnp.where(lo, q2, zero) if hd % 2 == 0 else jnp.where(lo, zero, q2)
            return _nt_dot(k_ref[krows, pair], qh)

        def stay(z):
            z = z * LOG2E
            sp = jnp.log2(1.0 + jnp.exp2(-jnp.abs(z)))
            log_beta = jnp.minimum(z, 0.0) - sp
            log_stay = log_beta - z
            if masked:
                log_stay = jnp.where(strict, log_stay, 0.0)
            hi = log_stay.astype(BF16)
            lo_part = (log_stay - hi.astype(F32)).astype(BF16)
            later = _dot(later_sum, hi) + _dot(later_sum, lo_part)
            return log_beta, later, jnp.sum(log_stay, axis=0, keepdims=True)

        def weigh(hd, log_beta, later, total):
            carry = car_ref[hd]
            a = jnp.exp2(log_beta + later + carry)
            if masked:
                a = jnp.where(strict, a, 0.0)
            rows = slice(hd * SB_DIM, (hd + 1) * SB_DIM)
            acc_ref[rows, :] += _dot(v_ref[j, rows, :], a.astype(BF16))
            car_ref[hd] = carry + total

        zs, mids = {}, {}
        for step in range(SB_HEADS + 2 * SB_SKEW):
            if step < SB_HEADS:
                zs[step] = scores(step)
            hd = step - SB_SKEW
            if 0 <= hd < SB_HEADS:
                mids[hd] = stay(zs.pop(hd))
            hd = step - 2 * SB_SKEW
            if 0 <= hd < SB_HEADS:
                weigh(hd, *mids.pop(hd))

    tile(i, True)

    def cond(c):
        j, alive = c
        return jnp.logical_and(j >= 0, alive > SB_DEAD)

    def body(c):
        j, _ = c
        tile(j, False)
        return j - 1, jnp.max(car_ref[...])

    lax.while_loop(cond, body, (i - 1, jnp.max(car_ref[...])))
    o_ref[...] = acc_ref[...].astype(o_ref.dtype)


def _sb_call(sb, sbv, bsz, seq):
    t = ATTN_TILE
    nq = seq // t
    return pl.pallas_call(
        _sb_kernel,
        grid=(bsz, nq),
        in_specs=[pl.BlockSpec((t, SB_WIDTH), lambda b, i: (b * nq + i, 0)),
                  pl.BlockSpec((seq, SB_WIDTH), lambda b, i: (b, 1)),
                  pl.BlockSpec((None, nq, SB_WIDTH, t), lambda b, i: (b, 0, 0, 0))],
        out_specs=pl.BlockSpec((None, SB_WIDTH, t), lambda b, i: (b, 0, i)),
        out_shape=jax.ShapeDtypeStruct((bsz, SB_WIDTH, seq), BF16),
        scratch_shapes=[pltpu.VMEM((SB_WIDTH, t), F32), pltpu.VMEM((SB_HEADS, 1, t), F32)],
        compiler_params=pltpu.CompilerParams(dimension_semantics=("arbitrary",) * 2, vmem_limit_bytes=VMEM_LIMIT),
        name="sb_attn",
    )(sb, sb, sbv)


def _mla_kernel(q_ref, k_ref, v_ref, o_ref, m_ref, l_ref, acc_ref):
    t = ATTN_TILE
    i = pl.program_id(1)
    rr = lax.broadcasted_iota(jnp.int32, (t, t), 0)
    cc = lax.broadcasted_iota(jnp.int32, (t, t), 1)
    visible = (rr // CHUNK) <= (cc // CHUNK)

    m_ref[...] = jnp.full_like(m_ref, -jnp.inf)
    l_ref[...] = jnp.zeros_like(l_ref)
    acc_ref[...] = jnp.zeros_like(acc_ref)

    def tile(j, masked):
        krows = pl.ds(pl.multiple_of(j * t, t), t)

        def scores(hd):
            sl = slice(hd * HEAD_PAD, (hd + 1) * HEAD_PAD)
            return _nt_dot(k_ref[krows, sl], q_ref[:, sl])

        pending = [scores(hd) for hd in range(MLA_AHEAD)]
        for hd in range(MLA_HEADS):
            if hd + MLA_AHEAD < MLA_HEADS:
                pending.append(scores(hd + MLA_AHEAD))
            s = pending[hd]
            if masked:
                s = jnp.where(visible, s, -jnp.inf)
            m_old = m_ref[hd]
            m_new = jnp.maximum(m_old, jnp.max(s, axis=0, keepdims=True))
            alpha = jnp.exp2(m_old - m_new)
            p = jnp.exp2(s - m_new)
            l_ref[hd] = alpha * l_ref[hd] + jnp.sum(p, axis=0, keepdims=True)
            rows = slice(hd * MLA_V, (hd + 1) * MLA_V)
            acc_ref[rows, :] = alpha * acc_ref[rows, :] + _dot(v_ref[j, rows, :], p.astype(BF16))
            m_ref[hd] = m_new

    def body(j, c):
        tile(j, False)
        return c

    lax.fori_loop(0, i, body, 0)
    tile(i, True)
    for hd in range(MLA_HEADS):
        rows = slice(hd * MLA_V, (hd + 1) * MLA_V)
        o_ref[rows, :] = (acc_ref[rows, :] / l_ref[hd]).astype(o_ref.dtype)


def _mla_call(qm, km, vm, bsz, seq):
    t = ATTN_TILE
    nq = seq // t
    width = MLA_HEADS * HEAD_PAD
    return pl.pallas_call(
        _mla_kernel,
        grid=(bsz, nq),
        in_specs=[pl.BlockSpec((t, width), lambda b, i: (b * nq + i, 0)),
                  pl.BlockSpec((seq, width), lambda b, i: (b, 0)),
                  pl.BlockSpec((None, nq, MLA_HEADS * MLA_V, t), lambda b, i: (b, 0, 0, 0))],
        out_specs=pl.BlockSpec((None, MLA_HEADS * MLA_V, t), lambda b, i: (b, 0, i)),
        out_shape=jax.ShapeDtypeStruct((bsz, MLA_HEADS * MLA_V, seq), BF16),
        scratch_shapes=[pltpu.VMEM((MLA_HEADS, 1, t), F32), pltpu.VMEM((MLA_HEADS, 1, t), F32),
                        pltpu.VMEM((MLA_HEADS * MLA_V, t), F32)],
        compiler_params=pltpu.CompilerParams(dimension_semantics=("arbitrary",) * 2, vmem_limit_bytes=VMEM_LIMIT),
        name="mla_attn",
    )(qm, km, vm)


def _post_kernel(osb_ref, omla_ref, gate_ref, x_ref, wosb_ref, womla_ref, wout_ref, gffn_ref, wrh_ref, wrl_ref,
                 br_ref, x1_ref, h2_ref, topw_ref, tope_ref, pos_ref, cnt_ref, run_ref):
    tm = TOKEN_TILE
    d = x_ref.shape[1]

    @pl.when(pl.program_id(0) == 0)
    def _():
        run_ref[...] = jnp.zeros_like(run_ref)

    tn = (((0,), (0,)), ((), ()))
    a = lax.dot_general(osb_ref[...], wosb_ref[...], tn, preferred_element_type=F32)
    b = lax.dot_general(omla_ref[...], womla_ref[...], tn, preferred_element_type=F32)
    mixed = gate_ref[:, :d].astype(F32) * a + gate_ref[:, d:].astype(F32) * b
    x1 = x_ref[...] + _dot(mixed.astype(BF16), wout_ref[...])
    x1_ref[...] = x1
    h2 = _rms(x1, gffn_ref[...])
    h2_ref[...] = h2

    h_hi = h2.astype(BF16)
    h_lo = (h2 - h_hi.astype(F32)).astype(BF16)
    logits = (_nt_dot(wrh_ref[...], h_hi) + _nt_dot(wrh_ref[...], h_lo) + _nt_dot(wrl_ref[...], h_hi)
              + br_ref[...])

    eidx = lax.broadcasted_iota(jnp.int32, (N_EXPERTS, tm), 0)
    work = logits
    tops, sels = [], []
    for _ in range(TOP_K):
        mk = jnp.max(work, axis=0, keepdims=True)
        ik = jnp.min(jnp.where(work == mk, eidx, N_EXPERTS), axis=0, keepdims=True)
        sel = eidx == ik
        work = jnp.where(sel, -jnp.inf, work)
        tops.append(mk)
        sels.append(sel)
    exps = [jnp.exp(mk - tops[0]) for mk in tops]
    denom = exps[0] + exps[1] + exps[2] + exps[3]

    chosen = jnp.zeros((N_EXPERTS, tm), F32)
    for sel in sels:
        chosen = chosen + sel.astype(F32)
    rr = lax.broadcasted_iota(jnp.int32, (tm, tm), 0)
    cc = lax.broadcasted_iota(jnp.int32, (tm, tm), 1)
    before = (rr < cc).astype(BF16)
    rank = _dot(chosen.astype(BF16), before) + run_ref[...]
    for k in range(TOP_K):
        topw_ref[k:k + 1, :] = exps[k] / denom
        tope_ref[k:k + 1, :] = jnp.sum(jnp.where(sels[k], eidx, 0), axis=0, keepdims=True)
        pos_ref[k:k + 1, :] = jnp.sum(jnp.where(sels[k], rank, 0.0), axis=0, keepdims=True).astype(jnp.int32)
    run_ref[...] = run_ref[...] + jnp.sum(chosen, axis=1, keepdims=True)
    cnt_ref[...] = jnp.broadcast_to(run_ref[...], cnt_ref.shape)


def _post_call(osb, omla, gates, x2, wosb, womla, wout, gffn, wrh, wrl, br):
    n, d = x2.shape
    tm = TOKEN_TILE
    nt = osb.shape[2] // tm
    const = lambda i: (0, 0)
    full = lambda a: pl.BlockSpec(a.shape, const)
    row = lambda w: pl.BlockSpec((tm, w), lambda i: (i, 0))
    col = lambda: pl.BlockSpec((TOP_K, tm), lambda i: (0, i))
    feat = lambda a: pl.BlockSpec((None, a.shape[1], tm), lambda i: (i // nt, 0, i % nt))
    return pl.pallas_call(
        _post_kernel,
        grid=(n // tm,),
        in_specs=[feat(osb), feat(omla), row(2 * d), row(d), full(wosb), full(womla), full(wout),
                  full(gffn), full(wrh), full(wrl), full(br)],
        out_specs=[row(d), row(d), col(), col(), col(), pl.BlockSpec((N_EXPERTS, LANES), const)],
        out_shape=[jax.ShapeDtypeStruct((n, d), F32), jax.ShapeDtypeStruct((n, d), F32),
                   jax.ShapeDtypeStruct((TOP_K, n), F32), jax.ShapeDtypeStruct((TOP_K, n), jnp.int32),
                   jax.ShapeDtypeStruct((TOP_K, n), jnp.int32), jax.ShapeDtypeStruct((N_EXPERTS, LANES), F32)],
        scratch_shapes=[pltpu.VMEM((N_EXPERTS, 1), F32)],
        compiler_params=pltpu.CompilerParams(dimension_semantics=("arbitrary",), vmem_limit_bytes=VMEM_LIMIT),
        name="post",
    )(osb, omla, gates, x2, wosb, womla, wout, gffn, wrh, wrl, br)


DMA_UNROLL = 8


def _disp_kernel(idx_ref, h2_ref, xs_hbm, sem):
    tm = TOKEN_TILE

    def copy(r, k, row):
        return pltpu.make_async_copy(h2_ref.at[pl.ds(r, 1)], xs_hbm.at[pl.ds(row, 1)], sem)

    def start_body(r, c):
        for k in range(TOP_K):
            copy(r, k, idx_ref[0, 0, k * tm + r]).start()
        return c
    lax.fori_loop(0, tm, start_body, 0, unroll=DMA_UNROLL)

    def wait_body(r, c):
        for k in range(TOP_K):
            copy(r, k, 0).wait()
        return c
    lax.fori_loop(0, tm, wait_body, 0, unroll=DMA_UNROLL)


def _disp_call(dest3, h2):
    n, d = h2.shape
    tm = TOKEN_TILE
    return pl.pallas_call(
        _disp_kernel,
        grid=(n // tm,),
        in_specs=[pl.BlockSpec((1, 1, TOP_K * tm), lambda g: (g, 0, 0), memory_space=pltpu.SMEM),
                  pl.BlockSpec((tm, d), lambda g: (g, 0))],
        out_specs=pl.BlockSpec(memory_space=pl.ANY),
        out_shape=jax.ShapeDtypeStruct((n * TOP_K, d), F32),
        scratch_shapes=[pltpu.SemaphoreType.DMA],
        compiler_params=pltpu.CompilerParams(dimension_semantics=("arbitrary",), vmem_limit_bytes=VMEM_LIMIT),
        name="disp",
    )(dest3, h2)


def _moe_kernel(blk_ref, exp_ref, lo_ref, hi_ref, first_ref, new_ref, nxt_ref, slot_ref,
                xs_ref, wup_hbm, bup_ref, wdn_hbm, bdn_ref, y_ref, wupf, wdnf, wupb, wdnp, wdnb, sem):
    t = EXPERT_TILE
    w = pl.program_id(0)
    lo = lo_ref[w]
    hi = hi_ref[w]
    half = wdnb.shape[0]
    d = wdnb.shape[1]

    def weight_copies(e, s):
        return (pltpu.make_async_copy(wup_hbm.at[e], wupf.at[s], sem.at[0, s]),
                pltpu.make_async_copy(wdn_hbm.at[e], wdnf.at[s], sem.at[1, s]))

    @pl.when(w == 0)
    def _():
        for c in weight_copies(exp_ref[0], 0):
            c.start()

    @pl.when(new_ref[w] == 1)
    def _():
        s = slot_ref[w]
        for c in weight_copies(exp_ref[w], s):
            c.wait()

        @pl.when(nxt_ref[w] >= 0)
        def _():
            for c in weight_copies(nxt_ref[w], 1 - s):
                c.start()

        wupb[...] = wupf[s].astype(BF16)
        for c in range(d // LANES):
            sl = slice(c * LANES, (c + 1) * LANES)
            wdnp[c, pl.ds(0, half // 2, stride=2), :] = wdnf[s, :half // 2, sl]
            wdnp[c, pl.ds(1, half // 2, stride=2), :] = wdnf[s, half // 2:, sl]
            wdnb[:, sl] = wdnp[c].astype(BF16)

    @pl.when(hi > lo)
    def _():
        x = xs_ref[...].astype(BF16)
        wup_ref = wupb
        wdn_ref = wdnb
        gu = _dot(x, wup_ref[...]) + bup_ref[...]
        lane = lax.broadcasted_iota(jnp.int32, (1, LANES), 1)
        even = (lane % 2) == 0
        acts = []
        for c in range(half // LANES):
            a = gu[:, c * LANES:(c + 1) * LANES]
            b = gu[:, half + c * LANES:half + (c + 1) * LANES]
            x_glu = jnp.minimum(jnp.where(even, a, pltpu.roll(b, 1, 1)), SWIGLU_LIMIT)
            x_lin = jnp.clip(jnp.where(even, pltpu.roll(a, LANES - 1, 1), b), -SWIGLU_LIMIT, SWIGLU_LIMIT)
            acts.append((x_glu * jax.nn.sigmoid(SWIGLU_ALPHA * x_glu) * (x_lin + 1.0)).astype(BF16))
        act = jnp.concatenate(acts, axis=1)
        y = _dot(act, wdn_ref[...]) + bdn_ref[...]
        row = lax.broadcasted_iota(jnp.int32, (t, 1), 0)
        mine = jnp.logical_and(row >= lo, row < hi)
        keep = jnp.where(first_ref[w] == 1, jnp.zeros_like(y), y_ref[...])
        y_ref[...] = jnp.where(mine, y, keep)


def _moe_call(items, xs, wup, bup, wdn, bdn):
    t = EXPERT_TILE
    n_rows, d = xs.shape
    ff = wdn.shape[1]
    n_items = items[0].shape[0]
    by_blk = lambda w, blk, ex, *_: (blk[w], 0)
    by_exp = lambda w, blk, ex, *_: (ex[w], 0, 0)
    grid_spec = pltpu.PrefetchScalarGridSpec(
        num_scalar_prefetch=len(items),
        grid=(n_items,),
        in_specs=[pl.BlockSpec((t, d), by_blk),
                  pl.BlockSpec(memory_space=pl.ANY),
                  pl.BlockSpec((None, 1, 2 * ff), by_exp),
                  pl.BlockSpec(memory_space=pl.ANY),
                  pl.BlockSpec((None, 1, d), by_exp)],
        out_specs=pl.BlockSpec((t, d), by_blk),
        scratch_shapes=[pltpu.VMEM((2, d, 2 * ff), F32),
                        pltpu.VMEM((2, ff, d), F32),
                        pltpu.VMEM((d, 2 * ff), BF16),
                        pltpu.VMEM((d // LANES, ff, LANES), F32),
                        pltpu.VMEM((ff, d), BF16),
                        pltpu.SemaphoreType.DMA((2, 2))],
    )
    return pl.pallas_call(
        _moe_kernel,
        grid_spec=grid_spec,
        out_shape=jax.ShapeDtypeStruct((n_rows, d), F32),
        compiler_params=pltpu.CompilerParams(dimension_semantics=("arbitrary",), vmem_limit_bytes=MOE_VMEM_LIMIT),
        name="moe",
    )(*items, xs, wup, bup, wdn, bdn)


def _moe_items(counts, n_rows):
    t = EXPERT_TILE
    nb = n_rows // t
    n_items = nb + N_EXPERTS - 1
    ends = jnp.cumsum(counts)
    starts = ends - counts
    first_blk = starts // t
    tiles = jnp.where(counts > 0, (ends - 1) // t - first_blk + 1, 0)
    item_end = jnp.cumsum(tiles)
    item_start = item_end - tiles
    total = item_end[-1]
    w = jnp.arange(n_items, dtype=jnp.int32)
    wc = jnp.minimum(w, total - 1)
    ex = jnp.sum((item_end[None, :] <= wc[:, None]).astype(jnp.int32), axis=1)
    onehot = (ex[:, None] == jnp.arange(N_EXPERTS, dtype=jnp.int32)[None, :]).astype(jnp.int32)
    pick = lambda v: jnp.sum(onehot * v[None, :], axis=1)
    blk = pick(first_blk) + wc - pick(item_start)
    lo = jnp.clip(pick(starts) - blk * t, 0, t)
    hi = jnp.where(w < total, jnp.clip(pick(ends) - blk * t, 0, t), lo)
    first = jnp.concatenate([jnp.ones((1,), jnp.int32), (blk[1:] != blk[:-1]).astype(jnp.int32)])
    new = jnp.logical_and(w < total, w == pick(item_start)).astype(jnp.int32)
    e_ids = jnp.arange(N_EXPERTS, dtype=jnp.int32)
    later = jnp.logical_and(e_ids[None, :] > e_ids[:, None], counts[None, :] > 0)
    next_e = jnp.min(jnp.where(later, e_ids[None, :], N_EXPERTS), axis=1)
    next_e = jnp.where(next_e == N_EXPERTS, -1, next_e)
    ordinal = jnp.cumsum((counts > 0).astype(jnp.int32)) - 1
    as_i32 = lambda v: v.astype(jnp.int32)
    return tuple(map(as_i32, (blk, ex, lo, hi, first, new, pick(next_e), pick(ordinal) % 2)))


def _comb_kernel(idx_ref, nxt_ref, y_hbm, x1_ref, w_ref, o_ref, ybuf, sem):
    tm = TOKEN_TILE
    g = pl.program_id(0)
    nb = pl.num_programs(0)
    slot = g % 2
    rows = TOP_K * tm

    def copy(row, dst_slot, r):
        return pltpu.make_async_copy(y_hbm.at[pl.ds(row, 1)], ybuf.at[dst_slot, pl.ds(r, 1)], sem.at[dst_slot])

    def start_gather(ref, dst_slot):
        def body(r, c):
            copy(ref[0, 0, r], dst_slot, r).start()
            return c
        lax.fori_loop(0, rows, body, 0, unroll=DMA_UNROLL)

    @pl.when(g == 0)
    def _():
        start_gather(idx_ref, 0)

    @pl.when(g + 1 < nb)
    def _():
        start_gather(nxt_ref, 1 - slot)

    def wait_body(r, c):
        copy(0, slot, r).wait()
        return c
    lax.fori_loop(0, rows, wait_body, 0, unroll=DMA_UNROLL)

    acc = x1_ref[...]
    for k in range(TOP_K):
        acc = acc + w_ref[:, k:k + 1] * ybuf[slot, k * tm:(k + 1) * tm, :]
    o_ref[...] = acc


def _comb_call(dest3, y, x1, w_tok):
    n, d = x1.shape
    tm = TOKEN_TILE
    nb = n // tm
    return pl.pallas_call(
        _comb_kernel,
        grid=(nb,),
        in_specs=[pl.BlockSpec((1, 1, TOP_K * tm), lambda g: (g, 0, 0), memory_space=pltpu.SMEM),
                  pl.BlockSpec((1, 1, TOP_K * tm), lambda g: (jnp.minimum(g + 1, nb - 1), 0, 0),
                               memory_space=pltpu.SMEM),
                  pl.BlockSpec(memory_space=pl.ANY),
                  pl.BlockSpec((tm, d), lambda g: (g, 0)),
                  pl.BlockSpec((tm, TOP_K), lambda g: (g, 0))],
        out_specs=pl.BlockSpec((tm, d), lambda g: (g, 0)),
        out_shape=jax.ShapeDtypeStruct((n, d), F32),
        scratch_shapes=[pltpu.VMEM((2, TOP_K * tm, d), F32), pltpu.SemaphoreType.DMA((2,))],
        compiler_params=pltpu.CompilerParams(dimension_semantics=("arbitrary",), vmem_limit_bytes=VMEM_LIMIT),
        name="comb",
    )(dest3, dest3, y, x1, w_tok)


def _layer(x2, pos2, bsz, seq, g_mix, w_in, g_cq, w_uq, g_ckv, w_ukv, g_qnorm, g_knorm, w_o_sb, w_o_mla, w_out,
           g_ffn, w_router, b_router, w_up, b_up, w_down, b_down):
    n, d = x2.shape
    c_sb = 3 * SB_WIDTH
    c_q = c_sb + MLA_Q_LORA
    c_kv = c_q + MLA_KV_LORA
    c_kr = c_kv + MLA_ROPE
    w1 = w_in[:, :2 * SB_WIDTH].astype(BF16)
    w1v = w_in[:, 2 * SB_WIDTH:c_sb].T.astype(BF16)
    w_kr = jnp.pad(w_in[:, c_kv:c_kr], ((0, 0), (MLA_NOPE, HEAD_PAD - MLA_QK)))
    w2 = jnp.concatenate([w_in[:, c_sb:c_kv], w_kr], axis=1).astype(BF16)
    w3 = w_in[:, c_kr:].astype(BF16)
    wuq = jnp.pad(w_uq.reshape(MLA_Q_LORA, MLA_HEADS, MLA_QK), ((0, 0), (0, 0), (0, HEAD_PAD - MLA_QK)))
    wuq = wuq.reshape(MLA_Q_LORA, MLA_HEADS * HEAD_PAD).astype(BF16)
    wukv = w_ukv.reshape(MLA_KV_LORA, MLA_HEADS, MLA_NOPE + MLA_V)
    wuk = jnp.pad(wukv[:, :, :MLA_NOPE], ((0, 0), (0, 0), (0, HEAD_PAD - MLA_NOPE)))
    wuk = wuk.reshape(MLA_KV_LORA, MLA_HEADS * HEAD_PAD).astype(BF16)
    wuv = wukv[:, :, MLA_NOPE:].reshape(MLA_KV_LORA, MLA_HEADS * MLA_V).T.astype(BF16)
    gq = jnp.tile(jnp.pad(g_qnorm, (0, HEAD_PAD - MLA_QK)), MLA_HEADS)[None, :]
    gk = jnp.tile(jnp.pad(g_knorm, (0, HEAD_PAD - MLA_QK)), MLA_HEADS)[None, :]
    half = MLA_ROPE // 2
    inv_freq = ROPE_THETA ** (-jnp.arange(half, dtype=F32) / half)
    invf = jnp.pad(jnp.concatenate([inv_freq, inv_freq]), (MLA_NOPE, HEAD_PAD - MLA_QK))[None, :]
    sgn = jnp.pad(jnp.concatenate([-jnp.ones((half,), F32), jnp.ones((half,), F32)]),
                  (MLA_NOPE, HEAD_PAD - MLA_QK))[None, :]

    sb, sbv, qm, km, vm, gates = _pre_call(x2, pos2, bsz, seq, g_mix[None, :], w1, w1v, w2, w3, g_cq[None, :],
                                           g_ckv[None, :], wuq, wuk, wuv, gq, gk, invf, sgn)
    o_sb = _sb_call(sb, sbv, bsz, seq)
    o_mla = _mla_call(qm, km, vm, bsz, seq)

    wr_t = w_router.T
    wr_hi = wr_t.astype(BF16)
    wr_lo = (wr_t - wr_hi.astype(F32)).astype(BF16)
    x1, h2, top_w, top_e, pos, cnt = _post_call(
        o_sb, o_mla, gates, x2, w_o_sb.astype(BF16), w_o_mla.astype(BF16), w_out.astype(BF16), g_ffn[None, :],
        wr_hi, wr_lo, b_router[:, None])

    counts = cnt[:, 0].astype(jnp.int32)
    starts = jnp.cumsum(counts) - counts
    e_iota = jnp.arange(N_EXPERTS, dtype=jnp.int32)[None, None, :]
    dest = jnp.sum(jnp.where(top_e[:, :, None] == e_iota, starts[None, None, :], 0), axis=2) + pos
    tm = TOKEN_TILE
    dest3 = dest.reshape(TOP_K, n // tm, tm).transpose(1, 0, 2).reshape(n // tm, 1, TOP_K * tm)

    xs = _disp_call(dest3, h2)
    y = _moe_call(_moe_items(counts, n * TOP_K), xs, w_up, b_up[:, None, :], w_down, b_down[:, None, :])
    return _comb_call(dest3, y, x1, top_w.T)


def kernel(x, positions, g_mix, w_in, g_cq, w_uq, g_ckv, w_ukv, g_qnorm, g_knorm, w_o_sb, w_o_mla, w_out, g_ffn,
           w_router, b_router, w_up, b_up, w_down, b_down):
    bsz, seq, d = x.shape
    x2 = x.reshape(bsz * seq, d)
    pos2 = positions.reshape(bsz * seq, 1)
    for l in range(g_mix.shape[0]):
        x2 = _layer(x2, pos2, bsz, seq, g_mix[l], w_in[l], g_cq[l], w_uq[l], g_ckv[l], w_ukv[l], g_qnorm[l],
                    g_knorm[l], w_o_sb[l], w_o_mla[l], w_out[l], g_ffn[l], w_router[l], b_router[l], w_up[l],
                    b_up[l], w_down[l], b_down[l])
    return x2.reshape(bsz, seq, d)
```

```python
import functools
import math

import jax
import jax.numpy as jnp
from jax import lax
from jax.experimental import pallas as pl
from jax.experimental.pallas import tpu as pltpu

F32 = jnp.float32
BF16 = jnp.bfloat16

EPS = 1e-6
CHUNK = 64
SB_HEADS = 8
SB_DIM = 64
SB_WIDTH = SB_HEADS * SB_DIM
MLA_HEADS = 8
MLA_NOPE = 64
MLA_ROPE = 32
MLA_QK = MLA_NOPE + MLA_ROPE
MLA_V = 64
MLA_Q_LORA = 384
MLA_KV_LORA = 256
ROPE_THETA = 10000.0
N_EXPERTS = 32
TOP_K = 4
SWIGLU_LIMIT = 7.0
SWIGLU_ALPHA = 1.702

LANES = 128
HEAD_PAD = LANES
TOKEN_TILE = 256
ATTN_TILE = 256
EXPERT_TILE = 256
MLA_AHEAD = 4
SB_SKEW = 1
LOG2E = math.log2(math.e)
SB_DEAD = -160.0
VMEM_LIMIT = 48 * 1024 * 1024
MOE_VMEM_LIMIT = 56 * 1024 * 1024


def _nt_dot(a, b):
    return lax.dot_general(a, b, (((1,), (1,)), ((), ())), preferred_element_type=F32)


def _dot(a, b):
    return jnp.dot(a, b, preferred_element_type=F32)


def _rms(t, g):
    return t * lax.rsqrt(jnp.mean(t * t, axis=-1, keepdims=True) + EPS) * g


def _pre_kernel(x_ref, pos_ref, gmix_ref, w1_ref, w1v_ref, w2_ref, w3_ref, gcq_ref, gckv_ref, wuq_ref, wuk_ref,
                wuv_ref, gq_ref, gk_ref, invf_ref, sgn_ref,
                sb_ref, sbv_ref, qm_ref, km_ref, vm_ref, gate_ref):
    h = _rms(x_ref[...], gmix_ref[...]).astype(BF16)

    lat = _dot(h, w2_ref[...])
    cqn = _rms(lat[:, :MLA_Q_LORA], gcq_ref[...]).astype(BF16)
    ckvn = _rms(lat[:, MLA_Q_LORA:MLA_Q_LORA + MLA_KV_LORA], gckv_ref[...]).astype(BF16)
    kr = lat[:, MLA_Q_LORA + MLA_KV_LORA:]
    qf = _dot(cqn, wuq_ref[...])
    kf = _dot(ckvn, wuk_ref[...])
    vm_ref[...] = _nt_dot(wuv_ref[...], ckvn).astype(BF16)

    sb_ref[:, :SB_WIDTH] = (_dot(h, w1_ref[:, :SB_WIDTH]) * (1.0 / math.sqrt(SB_DIM))).astype(BF16)
    sb_ref[:, SB_WIDTH:] = _dot(h, w1_ref[:, SB_WIDTH:]).astype(BF16)
    sbv_ref[...] = _nt_dot(w1v_ref[...], h).astype(BF16)

    gate_ref[...] = jax.nn.sigmoid(_dot(h, w3_ref[...])).astype(BF16)

    ang = pos_ref[...].astype(F32) * invf_ref[...]
    cos = jnp.cos(ang)
    sin = jnp.sin(ang) * sgn_ref[...]
    lane = lax.broadcasted_iota(jnp.int32, (1, LANES), 1)
    first_half = lane < MLA_NOPE + MLA_ROPE // 2

    def rope(t):
        partner = jnp.where(first_half, pltpu.roll(t, LANES - MLA_ROPE // 2, 1), pltpu.roll(t, MLA_ROPE // 2, 1))
        return t * cos + partner * sin

    def head_norm(t, g):
        return t * lax.rsqrt(jnp.sum(t * t, axis=-1, keepdims=True) * (1.0 / MLA_QK) + EPS) * g

    mla_scale = LOG2E / math.sqrt(MLA_QK)
    for hd in range(MLA_HEADS):
        sl = slice(hd * HEAD_PAD, (hd + 1) * HEAD_PAD)
        qm_ref[:, sl] = (rope(head_norm(qf[:, sl], gq_ref[:, sl])) * mla_scale).astype(BF16)
        km_ref[:, sl] = rope(head_norm(kf[:, sl] + kr, gk_ref[:, sl])).astype(BF16)


def _pre_call(x2, pos2, bsz, seq, gmix, w1, w1v, w2, w3, gcq, gckv, wuq, wuk, wuv, gq, gk, invf, sgn):
    n, d = x2.shape
    tm = TOKEN_TILE
    assert tm == ATTN_TILE and seq % tm == 0
    nt = seq // tm
    const = lambda i: (0, 0)
    full = lambda a: pl.BlockSpec(a.shape, const)
    row = lambda w: pl.BlockSpec((tm, w), lambda i: (i, 0))
    vt_spec = lambda w: pl.BlockSpec((None, None, w, tm), lambda i: (i // nt, i % nt, 0, 0))
    return pl.pallas_call(
        _pre_kernel,
        grid=(n // tm,),
        in_specs=[row(d), row(1), full(gmix), full(w1), full(w1v), full(w2), full(w3), full(gcq), full(gckv),
                  full(wuq), full(wuk), full(wuv), full(gq), full(gk), full(invf), full(sgn)],
        out_specs=[row(2 * SB_WIDTH), vt_spec(SB_WIDTH), row(MLA_HEADS * HEAD_PAD), row(MLA_HEADS * HEAD_PAD),
                   vt_spec(MLA_HEADS * MLA_V), row(2 * d)],
        out_shape=[jax.ShapeDtypeStruct((n, 2 * SB_WIDTH), BF16),
                   jax.ShapeDtypeStruct((bsz, nt, SB_WIDTH, tm), BF16),
                   jax.ShapeDtypeStruct((n, MLA_HEADS * HEAD_PAD), BF16),
                   jax.ShapeDtypeStruct((n, MLA_HEADS * HEAD_PAD), BF16),
                   jax.ShapeDtypeStruct((bsz, nt, MLA_HEADS * MLA_V, tm), BF16),
                   jax.ShapeDtypeStruct((n, 2 * d), BF16)],
        compiler_params=pltpu.CompilerParams(dimension_semantics=("arbitrary",), vmem_limit_bytes=VMEM_LIMIT),
        name="pre",
    )(x2, pos2, gmix, w1, w1v, w2, w3, gcq, gckv, wuq, wuk, wuv, gq, gk, invf, sgn)


def _sb_kernel(q_ref, k_ref, v_ref, o_ref, acc_ref, car_ref):
    t = ATTN_TILE
    i = pl.program_id(1)
    lane = lax.broadcasted_iota(jnp.int32, (1, LANES), 1)
    lo = lane < SB_DIM
    rr = lax.broadcasted_iota(jnp.int32, (t, t), 0)
    cc = lax.broadcasted_iota(jnp.int32, (t, t), 1)
    later_sum = (cc > rr).astype(BF16)
    strict = rr < cc

    acc_ref[...] = jnp.zeros_like(acc_ref)
    car_ref[...] = jnp.zeros_like(car_ref)

    def tile(j, masked):
        krows = pl.ds(pl.multiple_of(j * t, t), t)

        def scores(hd):
            pair = slice((hd // 2) * LANES, (hd // 2 + 1) * LANES)
            q2 = q_ref[:, pair]
            zero = jnp.zeros_like(q2)
            qh = jnp.where(lo, q2, zero) if hd % 2 == 0 else jnp.where(lo, zero, q2)
            return _nt_dot(k_ref[krows, pair], qh)

        def stay(z):
            z = z * LOG2E
            sp = jnp.log2(1.0 + jnp.exp2(-jnp.abs(z)))
            log_beta = jnp.minimum(z, 0.0) - sp
            log_stay = log_beta - z
            if masked:
                log_stay = jnp.where(strict, log_stay, 0.0)
            hi = log_stay.astype(BF16)
            lo_part = (log_stay - hi.astype(F32)).astype(BF16)
            later = _dot(later_sum, hi) + _dot(later_sum, lo_part)
            return log_beta, later, jnp.sum(log_stay, axis=0, keepdims=True)

        def weigh(hd, log_beta, later, total):
            carry = car_ref[hd]
            a = jnp.exp2(log_beta + later + carry)
            if masked:
                a = jnp.where(strict, a, 0.0)
            rows = slice(hd * SB_DIM, (hd + 1) * SB_DIM)
            acc_ref[rows, :] += _dot(v_ref[j, rows, :], a.astype(BF16))
            car_ref[hd] = carry + total

        zs, mids = {}, {}
        for step in range(SB_HEADS + 2 * SB_SKEW):
            if step < SB_HEADS:
                zs[step] = scores(step)
            hd = step - SB_SKEW
            if 0 <= hd < SB_HEADS:
                mids[hd] = stay(zs.pop(hd))
            hd = step - 2 * SB_SKEW
            if 0 <= hd < SB_HEADS:
                weigh(hd, *mids.pop(hd))

    tile(i, True)

    def cond(c):
        j, alive = c
        return jnp.logical_and(j >= 0, alive > SB_DEAD)

    def body(c):
        j, _ = c
        tile(j, False)
        return j - 1, jnp.max(car_ref[...])

    lax.while_loop(cond, body, (i - 1, jnp.max(car_ref[...])))
    o_ref[...] = acc_ref[...].astype(o_ref.dtype)


def _sb_call(sb, sbv, bsz, seq):
    t = ATTN_TILE
    nq = seq // t
    return pl.pallas_call(
        _sb_kernel,
        grid=(bsz, nq),
        in_specs=[pl.BlockSpec((t, SB_WIDTH), lambda b, i: (b * nq + i, 0)),
                  pl.BlockSpec((seq, SB_WIDTH), lambda b, i: (b, 1)),
                  pl.BlockSpec((None, nq, SB_WIDTH, t), lambda b, i: (b, 0, 0, 0))],
        out_specs=pl.BlockSpec((None, SB_WIDTH, t), lambda b, i: (b, 0, i)),
        out_shape=jax.ShapeDtypeStruct((bsz, SB_WIDTH, seq), BF16),
        scratch_shapes=[pltpu.VMEM((SB_WIDTH, t), F32), pltpu.VMEM((SB_HEADS, 1, t), F32)],
        compiler_params=pltpu.CompilerParams(dimension_semantics=("arbitrary",) * 2, vmem_limit_bytes=VMEM_LIMIT),
        name="sb_attn",
    )(sb, sb, sbv)


def _mla_kernel(q_ref, k_ref, v_ref, o_ref, m_ref, l_ref, acc_ref):
    t = ATTN_TILE
    i = pl.program_id(1)
    rr = lax.broadcasted_iota(jnp.int32, (t, t), 0)
    cc = lax.broadcasted_iota(jnp.int32, (t, t), 1)
    visible = (rr // CHUNK) <= (cc // CHUNK)

    m_ref[...] = jnp.full_like(m_ref, -jnp.inf)
    l_ref[...] = jnp.zeros_like(l_ref)
    acc_ref[...] = jnp.zeros_like(acc_ref)

    def tile(j, masked):
        krows = pl.ds(pl.multiple_of(j * t, t), t)

        def scores(hd):
            sl = slice(hd * HEAD_PAD, (hd + 1) * HEAD_PAD)
            return _nt_dot(k_ref[krows, sl], q_ref[:, sl])

        pending = [scores(hd) for hd in range(MLA_AHEAD)]
        for hd in range(MLA_HEADS):
            if hd + MLA_AHEAD < MLA_HEADS:
                pending.append(scores(hd + MLA_AHEAD))
            s = pending[hd]
            if masked:
                s = jnp.where(visible, s, -jnp.inf)
            m_old = m_ref[hd]
            m_new = jnp.maximum(m_old, jnp.max(s, axis=0, keepdims=True))
            alpha = jnp.exp2(m_old - m_new)
            p = jnp.exp2(s - m_new)
            l_ref[hd] = alpha * l_ref[hd] + jnp.sum(p, axis=0, keepdims=True)
            rows = slice(hd * MLA_V, (hd + 1) * MLA_V)
            acc_ref[rows, :] = alpha * acc_ref[rows, :] + _dot(v_ref[j, rows, :], p.astype(BF16))
            m_ref[hd] = m_new

    def body(j, c):
        tile(j, False)
        return c

    lax.fori_loop(0, i, body, 0)
    tile(i, True)
    for hd in range(MLA_HEADS):
        rows = slice(hd * MLA_V, (hd + 1) * MLA_V)
        o_ref[rows, :] = (acc_ref[rows, :] / l_ref[hd]).astype(o_ref.dtype)


def _mla_call(qm, km, vm, bsz, seq):
    t = ATTN_TILE
    nq = seq // t
    width = MLA_HEADS * HEAD_PAD
    return pl.pallas_call(
        _mla_kernel,
        grid=(bsz, nq),
        in_specs=[pl.BlockSpec((t, width), lambda b, i: (b * nq + i, 0)),
                  pl.BlockSpec((seq, width), lambda b, i: (b, 0)),
                  pl.BlockSpec((None, nq, MLA_HEADS * MLA_V, t), lambda b, i: (b, 0, 0, 0))],
        out_specs=pl.BlockSpec((None, MLA_HEADS * MLA_V, t), lambda b, i: (b, 0, i)),
        out_shape=jax.ShapeDtypeStruct((bsz, MLA_HEADS * MLA_V, seq), BF16),
        scratch_shapes=[pltpu.VMEM((MLA_HEADS, 1, t), F32), pltpu.VMEM((MLA_HEADS, 1, t), F32),
                        pltpu.VMEM((MLA_HEADS * MLA_V, t), F32)],
        compiler_params=pltpu.CompilerParams(dimension_semantics=("arbitrary",) * 2, vmem_limit_bytes=VMEM_LIMIT),
        name="mla_attn",
    )(qm, km, vm)


def _post_kernel(osb_ref, omla_ref, gate_ref, x_ref, wosb_ref, womla_ref, wout_ref, gffn_ref, wrh_ref, wrl_ref,
                 br_ref, x1_ref, h2_ref, topw_ref, tope_ref, pos_ref, cnt_ref, run_ref):
    tm = TOKEN_TILE
    d = x_ref.shape[1]

    @pl.when(pl.program_id(0) == 0)
    def _():
        run_ref[...] = jnp.zeros_like(run_ref)

    tn = (((0,), (0,)), ((), ()))
    a = lax.dot_general(osb_ref[...], wosb_ref[...], tn, preferred_element_type=F32)
    b = lax.dot_general(omla_ref[...], womla_ref[...], tn, preferred_element_type=F32)
    mixed = gate_ref[:, :d].astype(F32) * a + gate_ref[:, d:].astype(F32) * b
    x1 = x_ref[...] + _dot(mixed.astype(BF16), wout_ref[...])
    x1_ref[...] = x1
    h2 = _rms(x1, gffn_ref[...])
    h2_ref[...] = h2

    h_hi = h2.astype(BF16)
    h_lo = (h2 - h_hi.astype(F32)).astype(BF16)
    logits = (_nt_dot(wrh_ref[...], h_hi) + _nt_dot(wrh_ref[...], h_lo) + _nt_dot(wrl_ref[...], h_hi)
              + br_ref[...])

    eidx = lax.broadcasted_iota(jnp.int32, (N_EXPERTS, tm), 0)
    work = logits
    tops, sels = [], []
    for _ in range(TOP_K):
        mk = jnp.max(work, axis=0, keepdims=True)
        ik = jnp.min(jnp.where(work == mk, eidx, N_EXPERTS), axis=0, keepdims=True)
        sel = eidx == ik
        work = jnp.where(sel, -jnp.inf, work)
        tops.append(mk)
        sels.append(sel)
    exps = [jnp.exp(mk - tops[0]) for mk in tops]
    denom = exps[0] + exps[1] + exps[2] + exps[3]

    chosen = jnp.zeros((N_EXPERTS, tm), F32)
    for sel in sels:
        chosen = chosen + sel.astype(F32)
    rr = lax.broadcasted_iota(jnp.int32, (tm, tm), 0)
    cc = lax.broadcasted_iota(jnp.int32, (tm, tm), 1)
    before = (rr < cc).astype(BF16)
    rank = _dot(chosen.astype(BF16), before) + run_ref[...]
    for k in range(TOP_K):
        topw_ref[k:k + 1, :] = exps[k] / denom
        tope_ref[k:k + 1, :] = jnp.sum(jnp.where(sels[k], eidx, 0), axis=0, keepdims=True)
        pos_ref[k:k + 1, :] = jnp.sum(jnp.where(sels[k], rank, 0.0), axis=0, keepdims=True).astype(jnp.int32)
    run_ref[...] = run_ref[...] + jnp.sum(chosen, axis=1, keepdims=True)
    cnt_ref[...] = jnp.broadcast_to(run_ref[...], cnt_ref.shape)


def _post_call(osb, omla, gates, x2, wosb, womla, wout, gffn, wrh, wrl, br):
    n, d = x2.shape
    tm = TOKEN_TILE
    nt = osb.shape[2] // tm
    const = lambda i: (0, 0)
    full = lambda a: pl.BlockSpec(a.shape, const)
    row = lambda w: pl.BlockSpec((tm, w), lambda i: (i, 0))
    col = lambda: pl.BlockSpec((TOP_K, tm), lambda i: (0, i))
    feat = lambda a: pl.BlockSpec((None, a.shape[1], tm), lambda i: (i // nt, 0, i % nt))
    return pl.pallas_call(
        _post_kernel,
        grid=(n // tm,),
        in_specs=[feat(osb), feat(omla), row(2 * d), row(d), full(wosb), full(womla), full(wout),
                  full(gffn), full(wrh), full(wrl), full(br)],
        out_specs=[row(d), row(d), col(), col(), col(), pl.BlockSpec((N_EXPERTS, LANES), const)],
        out_shape=[jax.ShapeDtypeStruct((n, d), F32), jax.ShapeDtypeStruct((n, d), F32),
                   jax.ShapeDtypeStruct((TOP_K, n), F32), jax.ShapeDtypeStruct((TOP_K, n), jnp.int32),
                   jax.ShapeDtypeStruct((TOP_K, n), jnp.int32), jax.ShapeDtypeStruct((N_EXPERTS, LANES), F32)],
        scratch_shapes=[pltpu.VMEM((N_EXPERTS, 1), F32)],
        compiler_params=pltpu.CompilerParams(dimension_semantics=("arbitrary",), vmem_limit_bytes=VMEM_LIMIT),
        name="post",
    )(osb, omla, gates, x2, wosb, womla, wout, gffn, wrh, wrl, br)


SEG = 8
MAX_SLABS = TOKEN_TILE * TOP_K // SEG + N_EXPERTS
SORT_ROWS = MAX_SLABS * SEG


def _sorted_rows(n):
    nt = n // TOKEN_TILE
    return -(-(n * TOP_K + nt * N_EXPERTS * (SEG - 1)) // EXPERT_TILE) * EXPERT_TILE


def _route(top_e, pos, n):
    tm = TOKEN_TILE
    nt = n // tm
    e_ids = jnp.arange(N_EXPERTS, dtype=jnp.int32)
    onehot = top_e[:, :, None] == e_ids[None, None, :]
    c_tile = jnp.sum(onehot.reshape(TOP_K, nt, tm, N_EXPERTS).astype(jnp.int32), axis=(0, 2))
    slabs = (c_tile + SEG - 1) // SEG
    rank_base = jnp.cumsum(c_tile, axis=0) - c_tile
    row_base = (jnp.cumsum(slabs, axis=0) - slabs) * SEG
    padded = jnp.sum(slabs, axis=0) * SEG
    starts = jnp.cumsum(padded) - padded
    seg_global = starts[None, :] + row_base
    slab_end = jnp.cumsum(slabs, axis=1)
    seg_local = (slab_end - slabs) * SEG

    per_token = lambda tab: jnp.broadcast_to(tab[:, None, :], (nt, tm, N_EXPERTS)).reshape(n, N_EXPERTS)
    local = jnp.sum(jnp.where(onehot, per_token(seg_local - rank_base)[None], 0), axis=2) + pos

    q = jnp.arange(MAX_SLABS, dtype=jnp.int32)
    e_q = jnp.minimum(jnp.sum((slab_end[:, None, :] <= q[None, :, None]).astype(jnp.int32), axis=2), N_EXPERTS - 1)
    hit = e_q[:, :, None] == e_ids[None, None, :]
    pick = lambda tab: jnp.sum(jnp.where(hit, tab[:, None, :], 0), axis=2)
    within = (q[None, :] - pick(slab_end - slabs)) * SEG
    n_slabs = slab_end[:, -1]
    live = q[None, :] < n_slabs[:, None]
    slab_local = jnp.where(live, pick(seg_local) + within, 0).astype(jnp.int32)
    slab_global = jnp.where(live, pick(seg_global) + within, 0).astype(jnp.int32)
    return (padded.astype(jnp.int32), local.astype(jnp.int32), n_slabs.astype(jnp.int32),
            slab_local.reshape(nt, 1, MAX_SLABS), slab_global.reshape(nt, 1, MAX_SLABS))


def _disp_kernel(ns_ref, used_ref, sl_ref, sg_ref, local_ref, h2_ref, xs_hbm, xsort, zeros, sem):
    tm = TOKEN_TILE
    g = pl.program_id(0)
    nt = pl.num_programs(0)
    slot = g % 2

    m_iota = lax.broadcasted_iota(jnp.int32, (SORT_ROWS, tm), 0)
    place = jnp.zeros((SORT_ROWS, tm), F32)
    for k in range(TOP_K):
        place = jnp.where(m_iota == local_ref[k:k + 1, :], 1.0, place)
    xsort[slot] = _dot(place.astype(BF16), h2_ref[...].astype(BF16))

    def slab(q, s):
        return pltpu.make_async_copy(xsort.at[s, pl.ds(pl.multiple_of(sl_ref[0, 0, q], SEG), SEG)],
                                     xs_hbm.at[pl.ds(pl.multiple_of(sg_ref[0, 0, q], SEG), SEG)], sem.at[s])

    def start_body(q, c):
        slab(q, slot).start()
        return c
    lax.fori_loop(0, ns_ref[g], start_body, 0)

    def drain(count, s):
        def wait_body(q, c):
            pltpu.make_async_copy(xsort.at[s, pl.ds(0, SEG)], xs_hbm.at[pl.ds(0, SEG)], sem.at[s]).wait()
            return c
        lax.fori_loop(0, count, wait_body, 0)

    @pl.when(g > 0)
    def _():
        drain(ns_ref[g - 1], 1 - slot)

    @pl.when(g == nt - 1)
    def _():
        drain(ns_ref[g], slot)
        bt = EXPERT_TILE
        zeros[...] = jnp.zeros_like(zeros)
        used = used_ref[0]
        boundary = lax.shift_left(lax.shift_right_logical(used + (bt - 1), bt.bit_length() - 1), bt.bit_length() - 1)
        n_small = lax.shift_right_logical(boundary - used, SEG.bit_length() - 1)
        n_big = lax.shift_right_logical(xs_hbm.shape[0] - boundary, bt.bit_length() - 1)

        def small(q):
            row = pl.multiple_of(used + q * SEG, SEG)
            return pltpu.make_async_copy(zeros.at[pl.ds(0, SEG)], xs_hbm.at[pl.ds(row, SEG)], sem.at[2])

        def big(q):
            row = pl.multiple_of(boundary + q * bt, bt)
            return pltpu.make_async_copy(zeros, xs_hbm.at[pl.ds(row, bt)], sem.at[2])

        for piece, count in ((small, n_small), (big, n_big)):
            lax.fori_loop(0, count, lambda q, c, piece=piece: (piece(q).start(), c)[1], 0)
        for piece, count in ((small, n_small), (big, n_big)):
            lax.fori_loop(0, count, lambda q, c, piece=piece: (piece(q).wait(), c)[1], 0)


def _disp_call(n_slabs, used_rows, slab_local, slab_global, local, h2):
    n, d = h2.shape
    tm = TOKEN_TILE
    assert EXPERT_TILE & (EXPERT_TILE - 1) == 0 and SEG & (SEG - 1) == 0
    lists = pl.BlockSpec((1, 1, MAX_SLABS), lambda g, ns, used: (g, 0, 0), memory_space=pltpu.SMEM)
    grid_spec = pltpu.PrefetchScalarGridSpec(
        num_scalar_prefetch=2,
        grid=(n // tm,),
        in_specs=[lists, lists,
                  pl.BlockSpec((TOP_K, tm), lambda g, ns, used: (0, g)),
                  pl.BlockSpec((tm, d), lambda g, ns, used: (g, 0))],
        out_specs=pl.BlockSpec(memory_space=pl.ANY),
        scratch_shapes=[pltpu.VMEM((2, SORT_ROWS, d), F32), pltpu.VMEM((EXPERT_TILE, d), F32),
                        pltpu.SemaphoreType.DMA((3,))],
    )
    return pl.pallas_call(
        _disp_kernel,
        grid_spec=grid_spec,
        out_shape=jax.ShapeDtypeStruct((_sorted_rows(n), d), F32),
        compiler_params=pltpu.CompilerParams(dimension_semantics=("arbitrary",), vmem_limit_bytes=VMEM_LIMIT),
        name="disp",
    )(n_slabs, used_rows, slab_local, slab_global, local, h2)


def _moe_kernel(blk_ref, xblk_ref, exp_ref, lo_ref, hi_ref, first_ref, new_ref, nxt_ref, slot_ref,
                xs_ref, wup_hbm, bup_ref, wdn_hbm, bdn_ref, y_ref, wupf, wdnf, wupb, wdnp, wdnb, sem):
    t = EXPERT_TILE
    w = pl.program_id(0)
    lo = lo_ref[w]
    hi = hi_ref[w]
    half = wdnb.shape[0]
    d = wdnb.shape[1]

    def weight_copies(e, s):
        return (pltpu.make_async_copy(wup_hbm.at[e], wupf.at[s], sem.at[0, s]),
                pltpu.make_async_copy(wdn_hbm.at[e], wdnf.at[s], sem.at[1, s]))

    @pl.when(w == 0)
    def _():
        for c in weight_copies(exp_ref[0], 0):
            c.start()

    @pl.when(new_ref[w] == 1)
    def _():
        s = slot_ref[w]
        for c in weight_copies(exp_ref[w], s):
            c.wait()

        @pl.when(nxt_ref[w] >= 0)
        def _():
            for c in weight_copies(nxt_ref[w], 1 - s):
                c.start()

        wupb[...] = wupf[s].astype(BF16)
        for c in range(d // LANES):
            sl = slice(c * LANES, (c + 1) * LANES)
            wdnp[c, pl.ds(0, half // 2, stride=2), :] = wdnf[s, :half // 2, sl]
            wdnp[c, pl.ds(1, half // 2, stride=2), :] = wdnf[s, half // 2:, sl]
            wdnb[:, sl] = wdnp[c].astype(BF16)

    @pl.when(hi > lo)
    def _():
        x = xs_ref[...].astype(BF16)
        wup_ref = wupb
        wdn_ref = wdnb
        gu = _dot(x, wup_ref[...]) + bup_ref[...]
        lane = lax.broadcasted_iota(jnp.int32, (1, LANES), 1)
        even = (lane % 2) == 0
        acts = []
        for c in range(half // LANES):
            a = gu[:, c * LANES:(c + 1) * LANES]
            b = gu[:, half + c * LANES:half + (c + 1) * LANES]
            x_glu = jnp.minimum(jnp.where(even, a, pltpu.roll(b, 1, 1)), SWIGLU_LIMIT)
            x_lin = jnp.clip(jnp.where(even, pltpu.roll(a, LANES - 1, 1), b), -SWIGLU_LIMIT, SWIGLU_LIMIT)
            acts.append((x_glu * jax.nn.sigmoid(SWIGLU_ALPHA * x_glu) * (x_lin + 1.0)).astype(BF16))
        act = jnp.concatenate(acts, axis=1)
        y = _dot(act, wdn_ref[...]) + bdn_ref[...]
        row = lax.broadcasted_iota(jnp.int32, (t, 1), 0)
        mine = jnp.logical_and(row >= lo, row < hi)
        keep = jnp.where(first_ref[w] == 1, jnp.zeros_like(y), y_ref[...])
        y_ref[...] = jnp.where(mine, y, keep)

    @pl.when(jnp.logical_and(hi <= lo, first_ref[w] == 1))
    def _():
        y_ref[...] = jnp.zeros_like(y_ref)


def _moe_call(items, xs, wup, bup, wdn, bdn):
    t = EXPERT_TILE
    n_rows, d = xs.shape
    ff = wdn.shape[1]
    n_items = items[0].shape[0]
    by_blk = lambda w, blk, xblk, ex, *_: (blk[w], 0)
    by_xblk = lambda w, blk, xblk, ex, *_: (xblk[w], 0)
    by_exp = lambda w, blk, xblk, ex, *_: (ex[w], 0, 0)
    grid_spec = pltpu.PrefetchScalarGridSpec(
        num_scalar_prefetch=len(items),
        grid=(n_items,),
        in_specs=[pl.BlockSpec((t, d), by_xblk),
                  pl.BlockSpec(memory_space=pl.ANY),
                  pl.BlockSpec((None, 1, 2 * ff), by_exp),
                  pl.BlockSpec(memory_space=pl.ANY),
                  pl.BlockSpec((None, 1, d), by_exp)],
        out_specs=pl.BlockSpec((t, d), by_blk),
        scratch_shapes=[pltpu.VMEM((2, d, 2 * ff), F32),
                        pltpu.VMEM((2, ff, d), F32),
                        pltpu.VMEM((d, 2 * ff), BF16),
                        pltpu.VMEM((d // LANES, ff, LANES), F32),
                        pltpu.VMEM((ff, d), BF16),
                        pltpu.SemaphoreType.DMA((2, 2))],
    )
    return pl.pallas_call(
        _moe_kernel,
        grid_spec=grid_spec,
        out_shape=jax.ShapeDtypeStruct((n_rows, d), F32),
        compiler_params=pltpu.CompilerParams(dimension_semantics=("arbitrary",), vmem_limit_bytes=MOE_VMEM_LIMIT),
        name="moe",
    )(*items, xs, wup, bup, wdn, bdn)


def _moe_items(counts, n_rows):
    t = EXPERT_TILE
    nb = n_rows // t
    n_items = nb + N_EXPERTS - 1
    ends = jnp.cumsum(counts)
    starts = ends - counts
    first_blk = starts // t
    tiles = jnp.where(counts > 0, (ends - 1) // t - first_blk + 1, 0)
    item_end = jnp.cumsum(tiles)
    item_start = item_end - tiles
    total = item_end[-1]
    w = jnp.arange(n_items, dtype=jnp.int32)
    wc = jnp.minimum(w, total - 1)
    ex = jnp.sum((item_end[None, :] <= wc[:, None]).astype(jnp.int32), axis=1)
    onehot = (ex[:, None] == jnp.arange(N_EXPERTS, dtype=jnp.int32)[None, :]).astype(jnp.int32)
    pick = lambda v: jnp.sum(onehot * v[None, :], axis=1)
    xblk = pick(first_blk) + wc - pick(item_start)
    lo = jnp.clip(pick(starts) - xblk * t, 0, t)
    hi = jnp.where(w < total, jnp.clip(pick(ends) - xblk * t, 0, t), lo)
    used_blocks = (ends[-1] + t - 1) // t
    blk = jnp.where(w < total, xblk, jnp.minimum(used_blocks + w - total, nb - 1))
    first = jnp.concatenate([jnp.ones((1,), jnp.int32), (blk[1:] != blk[:-1]).astype(jnp.int32)])
    new = jnp.logical_and(w < total, w == pick(item_start)).astype(jnp.int32)
    e_ids = jnp.arange(N_EXPERTS, dtype=jnp.int32)
    later = jnp.logical_and(e_ids[None, :] > e_ids[:, None], counts[None, :] > 0)
    next_e = jnp.min(jnp.where(later, e_ids[None, :], N_EXPERTS), axis=1)
    next_e = jnp.where(next_e == N_EXPERTS, -1, next_e)
    ordinal = jnp.cumsum((counts > 0).astype(jnp.int32)) - 1
    as_i32 = lambda v: v.astype(jnp.int32)
    return tuple(map(as_i32, (blk, xblk, ex, lo, hi, first, new, pick(next_e), pick(ordinal) % 2)))


def _comb_kernel(ns_ref, sl_ref, sg_ref, nsl_ref, nsg_ref, local_ref, w_ref, y_hbm, x1_ref, o_ref, ysort, sem):
    tm = TOKEN_TILE
    g = pl.program_id(0)
    nt = pl.num_programs(0)
    slot = g % 2

    def slab(lref, gref, q, s):
        return pltpu.make_async_copy(y_hbm.at[pl.ds(pl.multiple_of(gref[0, 0, q], SEG), SEG)],
                                     ysort.at[s, pl.ds(pl.multiple_of(lref[0, 0, q], SEG), SEG)], sem.at[s])

    def fetch(lref, gref, count, s):
        def body(q, c):
            slab(lref, gref, q, s).start()
            return c
        lax.fori_loop(0, count, body, 0)

    @pl.when(g == 0)
    def _():
        ysort[...] = jnp.zeros_like(ysort)
        fetch(sl_ref, sg_ref, ns_ref[0], 0)

    @pl.when(g + 1 < nt)
    def _():
        fetch(nsl_ref, nsg_ref, ns_ref[jnp.minimum(g + 1, nt - 1)], 1 - slot)

    def wait_body(q, c):
        pltpu.make_async_copy(y_hbm.at[pl.ds(0, SEG)], ysort.at[slot, pl.ds(0, SEG)], sem.at[slot]).wait()
        return c
    lax.fori_loop(0, ns_ref[g], wait_body, 0)

    m_iota = lax.broadcasted_iota(jnp.int32, (tm, SORT_ROWS), 1)
    mix = jnp.zeros((tm, SORT_ROWS), F32)
    for k in range(TOP_K):
        mix = jnp.where(m_iota == local_ref[:, k:k + 1], w_ref[:, k:k + 1], mix)
    mix = mix.astype(BF16)
    y = ysort[slot]
    y_hi = y.astype(BF16)
    y_lo = (y - y_hi.astype(F32)).astype(BF16)
    o_ref[...] = x1_ref[...] + _dot(mix, y_hi) + _dot(mix, y_lo)


def _comb_call(n_slabs, slab_local, slab_global, local_t, w_t, y, x1):
    n, d = x1.shape
    tm = TOKEN_TILE
    nt = n // tm
    cur = lambda g, ns: (g, 0, 0)
    nxt = lambda g, ns: (jnp.minimum(g + 1, nt - 1), 0, 0)
    lists = lambda index_map: pl.BlockSpec((1, 1, MAX_SLABS), index_map, memory_space=pltpu.SMEM)
    grid_spec = pltpu.PrefetchScalarGridSpec(
        num_scalar_prefetch=1,
        grid=(nt,),
        in_specs=[lists(cur), lists(cur), lists(nxt), lists(nxt),
                  pl.BlockSpec((tm, TOP_K), lambda g, ns: (g, 0)),
                  pl.BlockSpec((tm, TOP_K), lambda g, ns: (g, 0)),
                  pl.BlockSpec(memory_space=pl.ANY),
                  pl.BlockSpec((tm, d), lambda g, ns: (g, 0))],
        out_specs=pl.BlockSpec((tm, d), lambda g, ns: (g, 0)),
        scratch_shapes=[pltpu.VMEM((2, SORT_ROWS, d), F32), pltpu.SemaphoreType.DMA((2,))],
    )
    return pl.pallas_call(
        _comb_kernel,
        grid_spec=grid_spec,
        out_shape=jax.ShapeDtypeStruct((n, d), F32),
        compiler_params=pltpu.CompilerParams(dimension_semantics=("arbitrary",), vmem_limit_bytes=VMEM_LIMIT),
        name="comb",
    )(n_slabs, slab_local, slab_global, slab_local, slab_global, local_t, w_t, y, x1)


def _layer(x2, pos2, bsz, seq, g_mix, w_in, g_cq, w_uq, g_ckv, w_ukv, g_qnorm, g_knorm, w_o_sb, w_o_mla, w_out,
           g_ffn, w_router, b_router, w_up, b_up, w_down, b_down):
    n, d = x2.shape
    c_sb = 3 * SB_WIDTH
    c_q = c_sb + MLA_Q_LORA
    c_kv = c_q + MLA_KV_LORA
    c_kr = c_kv + MLA_ROPE
    w1 = w_in[:, :2 * SB_WIDTH].astype(BF16)
    w1v = w_in[:, 2 * SB_WIDTH:c_sb].T.astype(BF16)
    w_kr = jnp.pad(w_in[:, c_kv:c_kr], ((0, 0), (MLA_NOPE, HEAD_PAD - MLA_QK)))
    w2 = jnp.concatenate([w_in[:, c_sb:c_kv], w_kr], axis=1).astype(BF16)
    w3 = w_in[:, c_kr:].astype(BF16)
    wuq = jnp.pad(w_uq.reshape(MLA_Q_LORA, MLA_HEADS, MLA_QK), ((0, 0), (0, 0), (0, HEAD_PAD - MLA_QK)))
    wuq = wuq.reshape(MLA_Q_LORA, MLA_HEADS * HEAD_PAD).astype(BF16)
    wukv = w_ukv.reshape(MLA_KV_LORA, MLA_HEADS, MLA_NOPE + MLA_V)
    wuk = jnp.pad(wukv[:, :, :MLA_NOPE], ((0, 0), (0, 0), (0, HEAD_PAD - MLA_NOPE)))
    wuk = wuk.reshape(MLA_KV_LORA, MLA_HEADS * HEAD_PAD).astype(BF16)
    wuv = wukv[:, :, MLA_NOPE:].reshape(MLA_KV_LORA, MLA_HEADS * MLA_V).T.astype(BF16)
    gq = jnp.tile(jnp.pad(g_qnorm, (0, HEAD_PAD - MLA_QK)), MLA_HEADS)[None, :]
    gk = jnp.tile(jnp.pad(g_knorm, (0, HEAD_PAD - MLA_QK)), MLA_HEADS)[None, :]
    half = MLA_ROPE // 2
    inv_freq = ROPE_THETA ** (-jnp.arange(half, dtype=F32) / half)
    invf = jnp.pad(jnp.concatenate([inv_freq, inv_freq]), (MLA_NOPE, HEAD_PAD - MLA_QK))[None, :]
    sgn = jnp.pad(jnp.concatenate([-jnp.ones((half,), F32), jnp.ones((half,), F32)]),
                  (MLA_NOPE, HEAD_PAD - MLA_QK))[None, :]

    sb, sbv, qm, km, vm, gates = _pre_call(x2, pos2, bsz, seq, g_mix[None, :], w1, w1v, w2, w3, g_cq[None, :],
                                           g_ckv[None, :], wuq, wuk, wuv, gq, gk, invf, sgn)
    o_sb = _sb_call(sb, sbv, bsz, seq)
    o_mla = _mla_call(qm, km, vm, bsz, seq)

    wr_t = w_router.T
    wr_hi = wr_t.astype(BF16)
    wr_lo = (wr_t - wr_hi.astype(F32)).astype(BF16)
    x1, h2, top_w, top_e, pos, cnt = _post_call(
        o_sb, o_mla, gates, x2, w_o_sb.astype(BF16), w_o_mla.astype(BF16), w_out.astype(BF16), g_ffn[None, :],
        wr_hi, wr_lo, b_router[:, None])

    del cnt
    padded, local, n_slabs, slab_local, slab_global = _route(top_e, pos, n)
    xs = _disp_call(n_slabs, jnp.sum(padded)[None], slab_local, slab_global, local, h2)
    y = _moe_call(_moe_items(padded, xs.shape[0]), xs, w_up, b_up[:, None, :], w_down, b_down[:, None, :])
    return _comb_call(n_slabs, slab_local, slab_global, local.T, top_w.T, y, x1)


def kernel(x, positions, g_mix, w_in, g_cq, w_uq, g_ckv, w_ukv, g_qnorm, g_knorm, w_o_sb, w_o_mla, w_out, g_ffn,
           w_router, b_router, w_up, b_up, w_down, b_down):
    bsz, seq, d = x.shape
    x2 = x.reshape(bsz * seq, d)
    pos2 = positions.reshape(bsz * seq, 1)
    for l in range(g_mix.shape[0]):
        x2 = _layer(x2, pos2, bsz, seq, g_mix[l], w_in[l], g_cq[l], w_uq[l], g_ckv[l], w_ukv[l], g_qnorm[l],
                    g_knorm[l], w_o_sb[l], w_o_mla[l], w_out[l], g_ffn[l], w_router[l], b_router[l], w_up[l],
                    b_up[l], w_down[l], b_down[l])
    return x2.reshape(bsz, seq, d)
```

```python
import functools
import math

import jax
import jax.numpy as jnp
from jax import lax
from jax.experimental import pallas as pl
from jax.experimental.pallas import tpu as pltpu

F32 = jnp.float32
BF16 = jnp.bfloat16

EPS = 1e-6
CHUNK = 64
SB_HEADS = 8
SB_DIM = 64
SB_WIDTH = SB_HEADS * SB_DIM
MLA_HEADS = 8
MLA_NOPE = 64
MLA_ROPE = 32
MLA_QK = MLA_NOPE + MLA_ROPE
MLA_V = 64
MLA_Q_LORA = 384
MLA_KV_LORA = 256
ROPE_THETA = 10000.0
N_EXPERTS = 32
TOP_K = 4
SWIGLU_LIMIT = 7.0
SWIGLU_ALPHA = 1.702

LANES = 128
HEAD_PAD = LANES
TOKEN_TILE = 256
PRE_TILE = 512
ATTN_TILE = 256
EXPERT_TILE = 256
MLA_AHEAD = 4
SB_SKEW = 1
LOG2E = math.log2(math.e)
SB_DEAD = -160.0
VMEM_LIMIT = 48 * 1024 * 1024
MOE_VMEM_LIMIT = 56 * 1024 * 1024


def _nt_dot(a, b):
    return lax.dot_general(a, b, (((1,), (1,)), ((), ())), preferred_element_type=F32)


def _dot(a, b):
    return jnp.dot(a, b, preferred_element_type=F32)


def _rms(t, g):
    return t * lax.rsqrt(jnp.mean(t * t, axis=-1, keepdims=True) + EPS) * g


def _pre_kernel(x_ref, pos_ref, gmix_ref, w1_ref, w1v_ref, w2_ref, w3_ref, gcq_ref, gckv_ref, wuq_ref, wuk_ref,
                wuv_ref, gq_ref, gk_ref, invf_ref, sgn_ref,
                sb_ref, sbv_ref, qm_ref, km_ref, vm_ref, gate_ref):
    h = _rms(x_ref[...], gmix_ref[...]).astype(BF16)

    lat = _dot(h, w2_ref[...])
    cqn = _rms(lat[:, :MLA_Q_LORA], gcq_ref[...]).astype(BF16)
    ckvn = _rms(lat[:, MLA_Q_LORA:MLA_Q_LORA + MLA_KV_LORA], gckv_ref[...]).astype(BF16)
    kr = lat[:, MLA_Q_LORA + MLA_KV_LORA:]
    qf = _dot(cqn, wuq_ref[...])
    kf = _dot(ckvn, wuk_ref[...])
    key_tiles = [slice(j * ATTN_TILE, (j + 1) * ATTN_TILE) for j in range(PRE_TILE // ATTN_TILE)]
    for j, rows in enumerate(key_tiles):
        vm_ref[j] = _nt_dot(wuv_ref[...], ckvn[rows, :]).astype(BF16)

    sb_ref[:, :SB_WIDTH] = (_dot(h, w1_ref[:, :SB_WIDTH]) * (LOG2E / math.sqrt(SB_DIM))).astype(BF16)
    sb_ref[:, SB_WIDTH:] = _dot(h, w1_ref[:, SB_WIDTH:]).astype(BF16)
    for j, rows in enumerate(key_tiles):
        sbv_ref[j] = _nt_dot(w1v_ref[...], h[rows, :]).astype(BF16)

    gate_ref[...] = jax.nn.sigmoid(_dot(h, w3_ref[...])).astype(BF16)

    ang = pos_ref[...].astype(F32) * invf_ref[...]
    cos = jnp.cos(ang)
    sin = jnp.sin(ang) * sgn_ref[...]
    lane = lax.broadcasted_iota(jnp.int32, (1, LANES), 1)
    first_half = lane < MLA_NOPE + MLA_ROPE // 2

    def rope(t):
        partner = jnp.where(first_half, pltpu.roll(t, LANES - MLA_ROPE // 2, 1), pltpu.roll(t, MLA_ROPE // 2, 1))
        return t * cos + partner * sin

    def head_norm(t, g):
        return t * lax.rsqrt(jnp.sum(t * t, axis=-1, keepdims=True) * (1.0 / MLA_QK) + EPS) * g

    mla_scale = LOG2E / math.sqrt(MLA_QK)
    for hd in range(MLA_HEADS):
        sl = slice(hd * HEAD_PAD, (hd + 1) * HEAD_PAD)
        qm_ref[:, sl] = (rope(head_norm(qf[:, sl], gq_ref[:, sl])) * mla_scale).astype(BF16)
        km_ref[:, sl] = rope(head_norm(kf[:, sl] + kr, gk_ref[:, sl])).astype(BF16)


def _pre_call(x2, pos2, bsz, seq, gmix, w1, w1v, w2, w3, gcq, gckv, wuq, wuk, wuv, gq, gk, invf, sgn):
    n, d = x2.shape
    tm = PRE_TILE
    ta = ATTN_TILE
    assert tm % ta == 0 and seq % tm == 0
    nt = seq // ta
    steps = seq // tm
    const = lambda i: (0, 0)
    full = lambda a: pl.BlockSpec(a.shape, const)
    row = lambda w: pl.BlockSpec((tm, w), lambda i: (i, 0))
    vt_spec = lambda w: pl.BlockSpec((None, tm // ta, w, ta), lambda i: (i // steps, i % steps, 0, 0))
    return pl.pallas_call(
        _pre_kernel,
        grid=(n // tm,),
        in_specs=[row(d), row(1), full(gmix), full(w1), full(w1v), full(w2), full(w3), full(gcq), full(gckv),
                  full(wuq), full(wuk), full(wuv), full(gq), full(gk), full(invf), full(sgn)],
        out_specs=[row(2 * SB_WIDTH), vt_spec(SB_WIDTH), row(MLA_HEADS * HEAD_PAD), row(MLA_HEADS * HEAD_PAD),
                   vt_spec(MLA_HEADS * MLA_V), row(2 * d)],
        out_shape=[jax.ShapeDtypeStruct((n, 2 * SB_WIDTH), BF16),
                   jax.ShapeDtypeStruct((bsz, nt, SB_WIDTH, ta), BF16),
                   jax.ShapeDtypeStruct((n, MLA_HEADS * HEAD_PAD), BF16),
                   jax.ShapeDtypeStruct((n, MLA_HEADS * HEAD_PAD), BF16),
                   jax.ShapeDtypeStruct((bsz, nt, MLA_HEADS * MLA_V, ta), BF16),
                   jax.ShapeDtypeStruct((n, 2 * d), BF16)],
        compiler_params=pltpu.CompilerParams(dimension_semantics=("arbitrary",), vmem_limit_bytes=VMEM_LIMIT),
        name="pre",
    )(x2, pos2, gmix, w1, w1v, w2, w3, gcq, gckv, wuq, wuk, wuv, gq, gk, invf, sgn)


def _sb_kernel(q_ref, k_ref, v_ref, o_ref, acc_ref, car_ref):
    t = ATTN_TILE
    i = pl.program_id(1)
    lane = lax.broadcasted_iota(jnp.int32, (1, LANES), 1)
    lo = lane < SB_DIM
    rr = lax.broadcasted_iota(jnp.int32, (t, t), 0)
    cc = lax.broadcasted_iota(jnp.int32, (t, t), 1)
    later_sum = (cc > rr).astype(BF16)
    strict = rr < cc

    acc_ref[...] = jnp.zeros_like(acc_ref)
    car_ref[...] = jnp.zeros_like(car_ref)

    def tile(j, masked):
        krows = pl.ds(pl.multiple_of(j * t, t), t)

        def scores(hd):
            pair = slice((hd // 2) * LANES, (hd // 2 + 1) * LANES)
            q2 = q_ref[:, pair]
            zero = jnp.zeros_like(q2)
            qh = jnp.where(lo, q2, zero) if hd % 2 == 0 else jnp.where(lo, zero, q2)
            return _nt_dot(k_ref[krows, pair], qh)

        def stay(z):
            neg_abs = pltpu.bitcast(pltpu.bitcast(z, jnp.uint32) | jnp.uint32(0x80000000), F32)
            sp = jnp.log2(1.0 + jnp.exp2(neg_abs))
            log_beta = jnp.minimum(z, 0.0) - sp
            log_stay = log_beta - z
            if masked:
                log_stay = jnp.where(strict, log_stay, 0.0)
            later = _dot(later_sum, log_stay.astype(BF16))
            return log_beta, later, jnp.sum(log_stay, axis=0, keepdims=True)

        def weigh(hd, log_beta, later, total):
            carry = car_ref[hd]
            a = jnp.exp2(log_beta + later + carry)
            if masked:
                a = jnp.where(strict, a, 0.0)
            rows = slice(hd * SB_DIM, (hd + 1) * SB_DIM)
            acc_ref[rows, :] += _dot(v_ref[j, rows, :], a.astype(BF16))
            car_ref[hd] = carry + total

        zs, mids = {}, {}
        for step in range(SB_HEADS + 2 * SB_SKEW):
            if step < SB_HEADS:
                zs[step] = scores(step)
            hd = step - SB_SKEW
            if 0 <= hd < SB_HEADS:
                mids[hd] = stay(zs.pop(hd))
            hd = step - 2 * SB_SKEW
            if 0 <= hd < SB_HEADS:
                weigh(hd, *mids.pop(hd))

    tile(i, True)

    def cond(c):
        j, alive = c
        return jnp.logical_and(j >= 0, alive > SB_DEAD)

    def body(c):
        j, _ = c
        tile(j, False)
        return j - 1, jnp.max(car_ref[...])

    lax.while_loop(cond, body, (i - 1, jnp.max(car_ref[...])))
    o_ref[...] = acc_ref[...].astype(o_ref.dtype)


def _sb_call(sb, sbv, bsz, seq):
    t = ATTN_TILE
    nq = seq // t
    return pl.pallas_call(
        _sb_kernel,
        grid=(bsz, nq),
        in_specs=[pl.BlockSpec((t, SB_WIDTH), lambda b, i: (b * nq + i, 0)),
                  pl.BlockSpec((seq, SB_WIDTH), lambda b, i: (b, 1)),
                  pl.BlockSpec((None, nq, SB_WIDTH, t), lambda b, i: (b, 0, 0, 0))],
        out_specs=pl.BlockSpec((None, SB_WIDTH, t), lambda b, i: (b, 0, i)),
        out_shape=jax.ShapeDtypeStruct((bsz, SB_WIDTH, seq), BF16),
        scratch_shapes=[pltpu.VMEM((SB_WIDTH, t), F32), pltpu.VMEM((SB_HEADS, 1, t), F32)],
        compiler_params=pltpu.CompilerParams(dimension_semantics=("arbitrary",) * 2, vmem_limit_bytes=VMEM_LIMIT),
        name="sb_attn",
    )(sb, sb, sbv)


def _mla_kernel(q_ref, k_ref, v_ref, o_ref, m_ref, l_ref, acc_ref):
    t = ATTN_TILE
    i = pl.program_id(1)
    rr = lax.broadcasted_iota(jnp.int32, (t, t), 0)
    cc = lax.broadcasted_iota(jnp.int32, (t, t), 1)
    visible = (rr // CHUNK) <= (cc // CHUNK)

    m_ref[...] = jnp.full_like(m_ref, -jnp.inf)
    l_ref[...] = jnp.zeros_like(l_ref)
    acc_ref[...] = jnp.zeros_like(acc_ref)

    def tile(j, masked):
        krows = pl.ds(pl.multiple_of(j * t, t), t)

        def scores(hd):
            sl = slice(hd * HEAD_PAD, (hd + 1) * HEAD_PAD)
            return _nt_dot(k_ref[krows, sl], q_ref[:, sl])

        pending = [scores(hd) for hd in range(MLA_AHEAD)]
        for hd in range(MLA_HEADS):
            if hd + MLA_AHEAD < MLA_HEADS:
                pending.append(scores(hd + MLA_AHEAD))
            s = pending[hd]
            if masked:
                s = jnp.where(visible, s, -jnp.inf)
            m_old = m_ref[hd]
            m_new = jnp.maximum(m_old, jnp.max(s, axis=0, keepdims=True))
            alpha = jnp.exp2(m_old - m_new)
            p = jnp.exp2(s - m_new)
            l_ref[hd] = alpha * l_ref[hd] + jnp.sum(p, axis=0, keepdims=True)
            rows = slice(hd * MLA_V, (hd + 1) * MLA_V)
            acc_ref[rows, :] = alpha * acc_ref[rows, :] + _dot(v_ref[j, rows, :], p.astype(BF16))
            m_ref[hd] = m_new

    def body(j, c):
        tile(j, False)
        return c

    lax.fori_loop(0, i, body, 0)
    tile(i, True)
    for hd in range(MLA_HEADS):
        rows = slice(hd * MLA_V, (hd + 1) * MLA_V)
        o_ref[rows, :] = (acc_ref[rows, :] / l_ref[hd]).astype(o_ref.dtype)


def _mla_call(qm, km, vm, bsz, seq):
    t = ATTN_TILE
    nq = seq // t
    width = MLA_HEADS * HEAD_PAD
    return pl.pallas_call(
        _mla_kernel,
        grid=(bsz, nq),
        in_specs=[pl.BlockSpec((t, width), lambda b, i: (b * nq + i, 0)),
                  pl.BlockSpec((seq, width), lambda b, i: (b, 0)),
                  pl.BlockSpec((None, nq, MLA_HEADS * MLA_V, t), lambda b, i: (b, 0, 0, 0))],
        out_specs=pl.BlockSpec((None, MLA_HEADS * MLA_V, t), lambda b, i: (b, 0, i)),
        out_shape=jax.ShapeDtypeStruct((bsz, MLA_HEADS * MLA_V, seq), BF16),
        scratch_shapes=[pltpu.VMEM((MLA_HEADS, 1, t), F32), pltpu.VMEM((MLA_HEADS, 1, t), F32),
                        pltpu.VMEM((MLA_HEADS * MLA_V, t), F32)],
        compiler_params=pltpu.CompilerParams(dimension_semantics=("arbitrary",) * 2, vmem_limit_bytes=VMEM_LIMIT),
        name="mla_attn",
    )(qm, km, vm)


def _post_kernel(osb_ref, omla_ref, gate_ref, x_ref, wosb_ref, womla_ref, wout_ref, gffn_ref, wrh_ref, wrl_ref,
                 br_ref, x1_ref, h2_ref, topw_ref, tope_ref, pos_ref, cnt_ref, run_ref):
    tm = TOKEN_TILE
    d = x_ref.shape[1]

    @pl.when(pl.program_id(0) == 0)
    def _():
        run_ref[...] = jnp.zeros_like(run_ref)

    tn = (((0,), (0,)), ((), ()))
    a = lax.dot_general(osb_ref[...], wosb_ref[...], tn, preferred_element_type=F32)
    b = lax.dot_general(omla_ref[...], womla_ref[...], tn, preferred_element_type=F32)
    mixed = gate_ref[:, :d].astype(F32) * a + gate_ref[:, d:].astype(F32) * b
    x1 = x_ref[...] + _dot(mixed.astype(BF16), wout_ref[...])
    x1_ref[...] = x1
    h2 = _rms(x1, gffn_ref[...])
    h2_ref[...] = h2

    h_hi = h2.astype(BF16)
    h_lo = (h2 - h_hi.astype(F32)).astype(BF16)
    logits = (_nt_dot(wrh_ref[...], h_hi) + _nt_dot(wrh_ref[...], h_lo) + _nt_dot(wrl_ref[...], h_hi)
              + br_ref[...])

    eidx = lax.broadcasted_iota(jnp.int32, (N_EXPERTS, tm), 0)
    work = logits
    tops, sels = [], []
    for _ in range(TOP_K):
        mk = jnp.max(work, axis=0, keepdims=True)
        ik = jnp.min(jnp.where(work == mk, eidx, N_EXPERTS), axis=0, keepdims=True)
        sel = eidx == ik
        work = jnp.where(sel, -jnp.inf, work)
        tops.append(mk)
        sels.append(sel)
    exps = [jnp.exp(mk - tops[0]) for mk in tops]
    denom = exps[0] + exps[1] + exps[2] + exps[3]

    chosen = jnp.zeros((N_EXPERTS, tm), F32)
    for sel in sels:
        chosen = chosen + sel.astype(F32)
    rr = lax.broadcasted_iota(jnp.int32, (tm, tm), 0)
    cc = lax.broadcasted_iota(jnp.int32, (tm, tm), 1)
    before = (rr < cc).astype(BF16)
    rank = _dot(chosen.astype(BF16), before) + run_ref[...]
    for k in range(TOP_K):
        topw_ref[k:k + 1, :] = exps[k] / denom
        tope_ref[k:k + 1, :] = jnp.sum(jnp.where(sels[k], eidx, 0), axis=0, keepdims=True)
        pos_ref[k:k + 1, :] = jnp.sum(jnp.where(sels[k], rank, 0.0), axis=0, keepdims=True).astype(jnp.int32)
    run_ref[...] = run_ref[...] + jnp.sum(chosen, axis=1, keepdims=True)
    cnt_ref[...] = jnp.broadcast_to(run_ref[...], cnt_ref.shape)


def _post_call(osb, omla, gates, x2, wosb, womla, wout, gffn, wrh, wrl, br):
    n, d = x2.shape
    tm = TOKEN_TILE
    nt = osb.shape[2] // tm
    const = lambda i: (0, 0)
    full = lambda a: pl.BlockSpec(a.shape, const)
    row = lambda w: pl.BlockSpec((tm, w), lambda i: (i, 0))
    col = lambda: pl.BlockSpec((TOP_K, tm), lambda i: (0, i))
    feat = lambda a: pl.BlockSpec((None, a.shape[1], tm), lambda i: (i // nt, 0, i % nt))
    return pl.pallas_call(
        _post_kernel,
        grid=(n // tm,),
        in_specs=[feat(osb), feat(omla), row(2 * d), row(d), full(wosb), full(womla), full(wout),
                  full(gffn), full(wrh), full(wrl), full(br)],
        out_specs=[row(d), row(d), col(), col(), col(), pl.BlockSpec((N_EXPERTS, LANES), const)],
        out_shape=[jax.ShapeDtypeStruct((n, d), F32), jax.ShapeDtypeStruct((n, d), F32),
                   jax.ShapeDtypeStruct((TOP_K, n), F32), jax.ShapeDtypeStruct((TOP_K, n), jnp.int32),
                   jax.ShapeDtypeStruct((TOP_K, n), jnp.int32), jax.ShapeDtypeStruct((N_EXPERTS, LANES), F32)],
        scratch_shapes=[pltpu.VMEM((N_EXPERTS, 1), F32)],
        compiler_params=pltpu.CompilerParams(dimension_semantics=("arbitrary",), vmem_limit_bytes=VMEM_LIMIT),
        name="post",
    )(osb, omla, gates, x2, wosb, womla, wout, gffn, wrh, wrl, br)


SEG = 8
MAX_SLABS = TOKEN_TILE * TOP_K // SEG + N_EXPERTS
SORT_ROWS = MAX_SLABS * SEG


def _sorted_rows(n):
    nt = n // TOKEN_TILE
    return -(-(n * TOP_K + nt * N_EXPERTS * (SEG - 1)) // EXPERT_TILE) * EXPERT_TILE


def _route(top_e, pos, n):
    tm = TOKEN_TILE
    nt = n // tm
    e_ids = jnp.arange(N_EXPERTS, dtype=jnp.int32)
    onehot = top_e[:, :, None] == e_ids[None, None, :]
    c_tile = jnp.sum(onehot.reshape(TOP_K, nt, tm, N_EXPERTS).astype(jnp.int32), axis=(0, 2))
    slabs = (c_tile + SEG - 1) // SEG
    rank_base = jnp.cumsum(c_tile, axis=0) - c_tile
    row_base = (jnp.cumsum(slabs, axis=0) - slabs) * SEG
    padded = jnp.sum(slabs, axis=0) * SEG
    starts = jnp.cumsum(padded) - padded
    seg_global = starts[None, :] + row_base
    slab_end = jnp.cumsum(slabs, axis=1)
    seg_local = (slab_end - slabs) * SEG

    per_token = lambda tab: jnp.broadcast_to(tab[:, None, :], (nt, tm, N_EXPERTS)).reshape(n, N_EXPERTS)
    local = jnp.sum(jnp.where(onehot, per_token(seg_local - rank_base)[None], 0), axis=2) + pos

    q = jnp.arange(MAX_SLABS, dtype=jnp.int32)
    e_q = jnp.minimum(jnp.sum((slab_end[:, None, :] <= q[None, :, None]).astype(jnp.int32), axis=2), N_EXPERTS - 1)
    hit = e_q[:, :, None] == e_ids[None, None, :]
    pick = lambda tab: jnp.sum(jnp.where(hit, tab[:, None, :], 0), axis=2)
    within = (q[None, :] - pick(slab_end - slabs)) * SEG
    n_slabs = slab_end[:, -1]
    live = q[None, :] < n_slabs[:, None]
    slab_local = jnp.where(live, pick(seg_local) + within, 0).astype(jnp.int32)
    slab_global = jnp.where(live, pick(seg_global) + within, 0).astype(jnp.int32)
    return (padded.astype(jnp.int32), local.astype(jnp.int32), n_slabs.astype(jnp.int32),
            slab_local.reshape(nt, 1, MAX_SLABS), slab_global.reshape(nt, 1, MAX_SLABS))


def _disp_kernel(ns_ref, used_ref, sl_ref, sg_ref, local_ref, h2_ref, xs_hbm, xsort, zeros, sem):
    tm = TOKEN_TILE
    g = pl.program_id(0)
    nt = pl.num_programs(0)
    slot = g % 2

    m_iota = lax.broadcasted_iota(jnp.int32, (SORT_ROWS, tm), 0)
    place = jnp.zeros((SORT_ROWS, tm), F32)
    for k in range(TOP_K):
        place = jnp.where(m_iota == local_ref[k:k + 1, :], 1.0, place)
    xsort[slot] = _dot(place.astype(BF16), h2_ref[...].astype(BF16))

    def slab(q, s):
        return pltpu.make_async_copy(xsort.at[s, pl.ds(pl.multiple_of(sl_ref[0, 0, q], SEG), SEG)],
                                     xs_hbm.at[pl.ds(pl.multiple_of(sg_ref[0, 0, q], SEG), SEG)], sem.at[s])

    def start_body(q, c):
        slab(q, slot).start()
        return c
    lax.fori_loop(0, ns_ref[g], start_body, 0)

    def drain(count, s):
        def wait_body(q, c):
            pltpu.make_async_copy(xsort.at[s, pl.ds(0, SEG)], xs_hbm.at[pl.ds(0, SEG)], sem.at[s]).wait()
            return c
        lax.fori_loop(0, count, wait_body, 0)

    @pl.when(g > 0)
    def _():
        drain(ns_ref[g - 1], 1 - slot)

    @pl.when(g == nt - 1)
    def _():
        drain(ns_ref[g], slot)
        bt = EXPERT_TILE
        zeros[...] = jnp.zeros_like(zeros)
        used = used_ref[0]
        boundary = lax.shift_left(lax.shift_right_logical(used + (bt - 1), bt.bit_length() - 1), bt.bit_length() - 1)
        n_small = lax.shift_right_logical(boundary - used, SEG.bit_length() - 1)
        n_big = lax.shift_right_logical(xs_hbm.shape[0] - boundary, bt.bit_length() - 1)

        def small(q):
            row = pl.multiple_of(used + q * SEG, SEG)
            return pltpu.make_async_copy(zeros.at[pl.ds(0, SEG)], xs_hbm.at[pl.ds(row, SEG)], sem.at[2])

        def big(q):
            row = pl.multiple_of(boundary + q * bt, bt)
            return pltpu.make_async_copy(zeros, xs_hbm.at[pl.ds(row, bt)], sem.at[2])

        for piece, count in ((small, n_small), (big, n_big)):
            lax.fori_loop(0, count, lambda q, c, piece=piece: (piece(q).start(), c)[1], 0)
        for piece, count in ((small, n_small), (big, n_big)):
            lax.fori_loop(0, count, lambda q, c, piece=piece: (piece(q).wait(), c)[1], 0)


def _disp_call(n_slabs, used_rows, slab_local, slab_global, local, h2):
    n, d = h2.shape
    tm = TOKEN_TILE
    assert EXPERT_TILE & (EXPERT_TILE - 1) == 0 and SEG & (SEG - 1) == 0
    lists = pl.BlockSpec((1, 1, MAX_SLABS), lambda g, ns, used: (g, 0, 0), memory_space=pltpu.SMEM)
    grid_spec = pltpu.PrefetchScalarGridSpec(
        num_scalar_prefetch=2,
        grid=(n // tm,),
        in_specs=[lists, lists,
                  pl.BlockSpec((TOP_K, tm), lambda g, ns, used: (0, g)),
                  pl.BlockSpec((tm, d), lambda g, ns, used: (g, 0))],
        out_specs=pl.BlockSpec(memory_space=pl.ANY),
        scratch_shapes=[pltpu.VMEM((2, SORT_ROWS, d), F32), pltpu.VMEM((EXPERT_TILE, d), F32),
                        pltpu.SemaphoreType.DMA((3,))],
    )
    return pl.pallas_call(
        _disp_kernel,
        grid_spec=grid_spec,
        out_shape=jax.ShapeDtypeStruct((_sorted_rows(n), d), F32),
        compiler_params=pltpu.CompilerParams(dimension_semantics=("arbitrary",), vmem_limit_bytes=VMEM_LIMIT),
        name="disp",
    )(n_slabs, used_rows, slab_local, slab_global, local, h2)


def _moe_kernel(blk_ref, xblk_ref, exp_ref, lo_ref, hi_ref, first_ref, new_ref, nxt_ref, slot_ref,
                xs_ref, wup_hbm, bup_ref, wdn_hbm, bdn_ref, y_ref, wupf, wdnf, wupb, wdnp, wdnb, sem):
    t = EXPERT_TILE
    w = pl.program_id(0)
    lo = lo_ref[w]
    hi = hi_ref[w]
    half = wdnb.shape[0]
    d = wdnb.shape[1]

    def weight_copies(e, s):
        return (pltpu.make_async_copy(wup_hbm.at[e], wupf.at[s], sem.at[0, s]),
                pltpu.make_async_copy(wdn_hbm.at[e], wdnf.at[s], sem.at[1, s]))

    @pl.when(w == 0)
    def _():
        for c in weight_copies(exp_ref[0], 0):
            c.start()

    @pl.when(new_ref[w] == 1)
    def _():
        s = slot_ref[w]
        for c in weight_copies(exp_ref[w], s):
            c.wait()

        @pl.when(nxt_ref[w] >= 0)
        def _():
            for c in weight_copies(nxt_ref[w], 1 - s):
                c.start()

        wupb[...] = wupf[s].astype(BF16)
        for c in range(d // LANES):
            sl = slice(c * LANES, (c + 1) * LANES)
            wdnp[c, pl.ds(0, half // 2, stride=2), :] = wdnf[s, :half // 2, sl]
            wdnp[c, pl.ds(1, half // 2, stride=2), :] = wdnf[s, half // 2:, sl]
            wdnb[:, sl] = wdnp[c].astype(BF16)

    @pl.when(hi > lo)
    def _():
        x = xs_ref[...].astype(BF16)
        wup_ref = wupb
        wdn_ref = wdnb
        gu = _dot(x, wup_ref[...]) + bup_ref[...]
        lane = lax.broadcasted_iota(jnp.int32, (1, LANES), 1)
        even = (lane % 2) == 0
        acts = []
        for c in range(half // LANES):
            a = gu[:, c * LANES:(c + 1) * LANES]
            b = gu[:, half + c * LANES:half + (c + 1) * LANES]
            x_glu = jnp.minimum(jnp.where(even, a, pltpu.roll(b, 1, 1)), SWIGLU_LIMIT)
            x_lin = jnp.clip(jnp.where(even, pltpu.roll(a, LANES - 1, 1), b), -SWIGLU_LIMIT, SWIGLU_LIMIT)
            acts.append((x_glu * jax.nn.sigmoid(SWIGLU_ALPHA * x_glu) * (x_lin + 1.0)).astype(BF16))
        act = jnp.concatenate(acts, axis=1)
        y = _dot(act, wdn_ref[...]) + bdn_ref[...]
        row = lax.broadcasted_iota(jnp.int32, (t, 1), 0)
        mine = jnp.logical_and(row >= lo, row < hi)
        keep = jnp.where(first_ref[w] == 1, jnp.zeros_like(y), y_ref[...])
        y_ref[...] = jnp.where(mine, y, keep)

    @pl.when(jnp.logical_and(hi <= lo, first_ref[w] == 1))
    def _():
        y_ref[...] = jnp.zeros_like(y_ref)


def _moe_call(items, xs, wup, bup, wdn, bdn):
    t = EXPERT_TILE
    n_rows, d = xs.shape
    ff = wdn.shape[1]
    n_items = items[0].shape[0]
    by_blk = lambda w, blk, xblk, ex, *_: (blk[w], 0)
    by_xblk = lambda w, blk, xblk, ex, *_: (xblk[w], 0)
    by_exp = lambda w, blk, xblk, ex, *_: (ex[w], 0, 0)
    grid_spec = pltpu.PrefetchScalarGridSpec(
        num_scalar_prefetch=len(items),
        grid=(n_items,),
        in_specs=[pl.BlockSpec((t, d), by_xblk),
                  pl.BlockSpec(memory_space=pl.ANY),
                  pl.BlockSpec((None, 1, 2 * ff), by_exp),
                  pl.BlockSpec(memory_space=pl.ANY),
                  pl.BlockSpec((None, 1, d), by_exp)],
        out_specs=pl.BlockSpec((t, d), by_blk),
        scratch_shapes=[pltpu.VMEM((2, d, 2 * ff), F32),
                        pltpu.VMEM((2, ff, d), F32),
                        pltpu.VMEM((d, 2 * ff), BF16),
                        pltpu.VMEM((d // LANES, ff, LANES), F32),
                        pltpu.VMEM((ff, d), BF16),
                        pltpu.SemaphoreType.DMA((2, 2))],
    )
    return pl.pallas_call(
        _moe_kernel,
        grid_spec=grid_spec,
        out_shape=jax.ShapeDtypeStruct((n_rows, d), F32),
        compiler_params=pltpu.CompilerParams(dimension_semantics=("arbitrary",), vmem_limit_bytes=MOE_VMEM_LIMIT),
        name="moe",
    )(*items, xs, wup, bup, wdn, bdn)


def _moe_items(counts, n_rows):
    t = EXPERT_TILE
    nb = n_rows // t
    n_items = nb + N_EXPERTS - 1
    ends = jnp.cumsum(counts)
    starts = ends - counts
    first_blk = starts // t
    tiles = jnp.where(counts > 0, (ends - 1) // t - first_blk + 1, 0)
    item_end = jnp.cumsum(tiles)
    item_start = item_end - tiles
    total = item_end[-1]
    w = jnp.arange(n_items, dtype=jnp.int32)
    wc = jnp.minimum(w, total - 1)
    ex = jnp.sum((item_end[None, :] <= wc[:, None]).astype(jnp.int32), axis=1)
    onehot = (ex[:, None] == jnp.arange(N_EXPERTS, dtype=jnp.int32)[None, :]).astype(jnp.int32)
    pick = lambda v: jnp.sum(onehot * v[None, :], axis=1)
    xblk = pick(first_blk) + wc - pick(item_start)
    lo = jnp.clip(pick(starts) - xblk * t, 0, t)
    hi = jnp.where(w < total, jnp.clip(pick(ends) - xblk * t, 0, t), lo)
    used_blocks = (ends[-1] + t - 1) // t
    blk = jnp.where(w < total, xblk, jnp.minimum(used_blocks + w - total, nb - 1))
    first = jnp.concatenate([jnp.ones((1,), jnp.int32), (blk[1:] != blk[:-1]).astype(jnp.int32)])
    new = jnp.logical_and(w < total, w == pick(item_start)).astype(jnp.int32)
    e_ids = jnp.arange(N_EXPERTS, dtype=jnp.int32)
    later = jnp.logical_and(e_ids[None, :] > e_ids[:, None], counts[None, :] > 0)
    next_e = jnp.min(jnp.where(later, e_ids[None, :], N_EXPERTS), axis=1)
    next_e = jnp.where(next_e == N_EXPERTS, -1, next_e)
    ordinal = jnp.cumsum((counts > 0).astype(jnp.int32)) - 1
    as_i32 = lambda v: v.astype(jnp.int32)
    return tuple(map(as_i32, (blk, xblk, ex, lo, hi, first, new, pick(next_e), pick(ordinal) % 2)))


def _comb_kernel(ns_ref, sl_ref, sg_ref, nsl_ref, nsg_ref, local_ref, w_ref, y_hbm, x1_ref, o_ref, ysort, sem):
    tm = TOKEN_TILE
    g = pl.program_id(0)
    nt = pl.num_programs(0)
    slot = g % 2

    def slab(lref, gref, q, s):
        return pltpu.make_async_copy(y_hbm.at[pl.ds(pl.multiple_of(gref[0, 0, q], SEG), SEG)],
                                     ysort.at[s, pl.ds(pl.multiple_of(lref[0, 0, q], SEG), SEG)], sem.at[s])

    def fetch(lref, gref, count, s):
        def body(q, c):
            slab(lref, gref, q, s).start()
            return c
        lax.fori_loop(0, count, body, 0)

    @pl.when(g == 0)
    def _():
        ysort[...] = jnp.zeros_like(ysort)
        fetch(sl_ref, sg_ref, ns_ref[0], 0)

    @pl.when(g + 1 < nt)
    def _():
        fetch(nsl_ref, nsg_ref, ns_ref[jnp.minimum(g + 1, nt - 1)], 1 - slot)

    def wait_body(q, c):
        pltpu.make_async_copy(y_hbm.at[pl.ds(0, SEG)], ysort.at[slot, pl.ds(0, SEG)], sem.at[slot]).wait()
        return c
    lax.fori_loop(0, ns_ref[g], wait_body, 0)

    m_iota = lax.broadcasted_iota(jnp.int32, (tm, SORT_ROWS), 1)
    mix = jnp.zeros((tm, SORT_ROWS), F32)
    for k in range(TOP_K):
        mix = jnp.where(m_iota == local_ref[:, k:k + 1], w_ref[:, k:k + 1], mix)
    mix = mix.astype(BF16)
    y = ysort[slot]
    y_hi = y.astype(BF16)
    y_lo = (y - y_hi.astype(F32)).astype(BF16)
    o_ref[...] = x1_ref[...] + _dot(mix, y_hi) + _dot(mix, y_lo)


def _comb_call(n_slabs, slab_local, slab_global, local_t, w_t, y, x1):
    n, d = x1.shape
    tm = TOKEN_TILE
    nt = n // tm
    cur = lambda g, ns: (g, 0, 0)
    nxt = lambda g, ns: (jnp.minimum(g + 1, nt - 1), 0, 0)
    lists = lambda index_map: pl.BlockSpec((1, 1, MAX_SLABS), index_map, memory_space=pltpu.SMEM)
    grid_spec = pltpu.PrefetchScalarGridSpec(
        num_scalar_prefetch=1,
        grid=(nt,),
        in_specs=[lists(cur), lists(cur), lists(nxt), lists(nxt),
                  pl.BlockSpec((tm, TOP_K), lambda g, ns: (g, 0)),
                  pl.BlockSpec((tm, TOP_K), lambda g, ns: (g, 0)),
                  pl.BlockSpec(memory_space=pl.ANY),
                  pl.BlockSpec((tm, d), lambda g, ns: (g, 0))],
        out_specs=pl.BlockSpec((tm, d), lambda g, ns: (g, 0)),
        scratch_shapes=[pltpu.VMEM((2, SORT_ROWS, d), F32), pltpu.SemaphoreType.DMA((2,))],
    )
    return pl.pallas_call(
        _comb_kernel,
        grid_spec=grid_spec,
        out_shape=jax.ShapeDtypeStruct((n, d), F32),
        compiler_params=pltpu.CompilerParams(dimension_semantics=("arbitrary",), vmem_limit_bytes=VMEM_LIMIT),
        name="comb",
    )(n_slabs, slab_local, slab_global, slab_local, slab_global, local_t, w_t, y, x1)


def _layer(x2, pos2, bsz, seq, g_mix, w_in, g_cq, w_uq, g_ckv, w_ukv, g_qnorm, g_knorm, w_o_sb, w_o_mla, w_out,
           g_ffn, w_router, b_router, w_up, b_up, w_down, b_down):
    n, d = x2.shape
    c_sb = 3 * SB_WIDTH
    c_q = c_sb + MLA_Q_LORA
    c_kv = c_q + MLA_KV_LORA
    c_kr = c_kv + MLA_ROPE
    w1 = w_in[:, :2 * SB_WIDTH].astype(BF16)
    w1v = w_in[:, 2 * SB_WIDTH:c_sb].T.astype(BF16)
    w_kr = jnp.pad(w_in[:, c_kv:c_kr], ((0, 0), (MLA_NOPE, HEAD_PAD - MLA_QK)))
    w2 = jnp.concatenate([w_in[:, c_sb:c_kv], w_kr], axis=1).astype(BF16)
    w3 = w_in[:, c_kr:].astype(BF16)
    wuq = jnp.pad(w_uq.reshape(MLA_Q_LORA, MLA_HEADS, MLA_QK), ((0, 0), (0, 0), (0, HEAD_PAD - MLA_QK)))
    wuq = wuq.reshape(MLA_Q_LORA, MLA_HEADS * HEAD_PAD).astype(BF16)
    wukv = w_ukv.reshape(MLA_KV_LORA, MLA_HEADS, MLA_NOPE + MLA_V)
    wuk = jnp.pad(wukv[:, :, :MLA_NOPE], ((0, 0), (0, 0), (0, HEAD_PAD - MLA_NOPE)))
    wuk = wuk.reshape(MLA_KV_LORA, MLA_HEADS * HEAD_PAD).astype(BF16)
    wuv = wukv[:, :, MLA_NOPE:].reshape(MLA_KV_LORA, MLA_HEADS * MLA_V).T.astype(BF16)
    gq = jnp.tile(jnp.pad(g_qnorm, (0, HEAD_PAD - MLA_QK)), MLA_HEADS)[None, :]
    gk = jnp.tile(jnp.pad(g_knorm, (0, HEAD_PAD - MLA_QK)), MLA_HEADS)[None, :]
    half = MLA_ROPE // 2
    inv_freq = ROPE_THETA ** (-jnp.arange(half, dtype=F32) / half)
    invf = jnp.pad(jnp.concatenate([inv_freq, inv_freq]), (MLA_NOPE, HEAD_PAD - MLA_QK))[None, :]
    sgn = jnp.pad(jnp.concatenate([-jnp.ones((half,), F32), jnp.ones((half,), F32)]),
                  (MLA_NOPE, HEAD_PAD - MLA_QK))[None, :]

    sb, sbv, qm, km, vm, gates = _pre_call(x2, pos2, bsz, seq, g_mix[None, :], w1, w1v, w2, w3, g_cq[None, :],
                                           g_ckv[None, :], wuq, wuk, wuv, gq, gk, invf, sgn)
    o_sb = _sb_call(sb, sbv, bsz, seq)
    o_mla = _mla_call(qm, km, vm, bsz, seq)

    wr_t = w_router.T
    wr_hi = wr_t.astype(BF16)
    wr_lo = (wr_t - wr_hi.astype(F32)).astype(BF16)
    x1, h2, top_w, top_e, pos, cnt = _post_call(
        o_sb, o_mla, gates, x2, w_o_sb.astype(BF16), w_o_mla.astype(BF16), w_out.astype(BF16), g_ffn[None, :],
        wr_hi, wr_lo, b_router[:, None])

    del cnt
    padded, local, n_slabs, slab_local, slab_global = _route(top_e, pos, n)
    xs = _disp_call(n_slabs, jnp.sum(padded)[None], slab_local, slab_global, local, h2)
    y = _moe_call(_moe_items(padded, xs.shape[0]), xs, w_up, b_up[:, None, :], w_down, b_down[:, None, :])
    return _comb_call(n_slabs, slab_local, slab_global, local.T, top_w.T, y, x1)


def kernel(x, positions, g_mix, w_in, g_cq, w_uq, g_ckv, w_ukv, g_qnorm, g_knorm, w_o_sb, w_o_mla, w_out, g_ffn,
           w_router, b_router, w_up, b_up, w_down, b_down):
    bsz, seq, d = x.shape
    x2 = x.reshape(bsz * seq, d)
    pos2 = positions.reshape(bsz * seq, 1)
    for l in range(g_mix.shape[0]):
        x2 = _layer(x2, pos2, bsz, seq, g_mix[l], w_in[l], g_cq[l], w_uq[l], g_ckv[l], w_ukv[l], g_qnorm[l],
                    g_knorm[l], w_o_sb[l], w_o_mla[l], w_out[l], g_ffn[l], w_router[l], b_router[l], w_up[l],
                    b_up[l], w_down[l], b_down[l])
    return x2.reshape(bsz, seq, d)
```

```python
import functools
import math

import jax
import jax.numpy as jnp
from jax import lax
from jax.experimental import pallas as pl
from jax.experimental.pallas import tpu as pltpu

F32 = jnp.float32
BF16 = jnp.bfloat16

EPS = 1e-6
CHUNK = 64
SB_HEADS = 8
SB_DIM = 64
SB_WIDTH = SB_HEADS * SB_DIM
MLA_HEADS = 8
MLA_NOPE = 64
MLA_ROPE = 32
MLA_QK = MLA_NOPE + MLA_ROPE
MLA_V = 64
MLA_Q_LORA = 384
MLA_KV_LORA = 256
ROPE_THETA = 10000.0
N_EXPERTS = 32
TOP_K = 4
SWIGLU_LIMIT = 7.0
SWIGLU_ALPHA = 1.702

LANES = 128
HEAD_PAD = LANES
TOKEN_TILE = 256
PRE_TILE = 512
ATTN_TILE = 256
EXPERT_TILE = 256
MLA_AHEAD = 4
SB_SKEW = 1
LOG2E = math.log2(math.e)
SB_DEAD = -160.0
VMEM_LIMIT = 48 * 1024 * 1024
MOE_VMEM_LIMIT = 56 * 1024 * 1024


def _nt_dot(a, b):
    return lax.dot_general(a, b, (((1,), (1,)), ((), ())), preferred_element_type=F32)


def _dot(a, b):
    return jnp.dot(a, b, preferred_element_type=F32)


def _rms(t, g):
    return t * lax.rsqrt(jnp.mean(t * t, axis=-1, keepdims=True) + EPS) * g


def _pre_kernel(x_ref, pos_ref, gmix_ref, w1_ref, w1v_ref, w2_ref, w3_ref, gcq_ref, gckv_ref, wuq_ref, wuk_ref,
                wuv_ref, gq_ref, gk_ref, invf_ref, sgn_ref,
                sb_ref, sbv_ref, qm_ref, km_ref, vm_ref, gate_ref):
    h = _rms(x_ref[...], gmix_ref[...]).astype(BF16)

    lat = _dot(h, w2_ref[...])
    cqn = _rms(lat[:, :MLA_Q_LORA], gcq_ref[...]).astype(BF16)
    ckvn = _rms(lat[:, MLA_Q_LORA:MLA_Q_LORA + MLA_KV_LORA], gckv_ref[...]).astype(BF16)
    kr = lat[:, MLA_Q_LORA + MLA_KV_LORA:]
    qf = _dot(cqn, wuq_ref[...])
    kf = _dot(ckvn, wuk_ref[...])
    key_tiles = [slice(j * ATTN_TILE, (j + 1) * ATTN_TILE) for j in range(PRE_TILE // ATTN_TILE)]
    for j, rows in enumerate(key_tiles):
        vm_ref[j] = _nt_dot(wuv_ref[...], ckvn[rows, :]).astype(BF16)

    sb_ref[:, :SB_WIDTH] = (_dot(h, w1_ref[:, :SB_WIDTH]) * (LOG2E / math.sqrt(SB_DIM))).astype(BF16)
    sb_ref[:, SB_WIDTH:] = _dot(h, w1_ref[:, SB_WIDTH:]).astype(BF16)
    for j, rows in enumerate(key_tiles):
        sbv_ref[j] = _nt_dot(w1v_ref[...], h[rows, :]).astype(BF16)

    gate_ref[...] = jax.nn.sigmoid(_dot(h, w3_ref[...])).astype(BF16)

    ang = pos_ref[...].astype(F32) * invf_ref[...]
    cos = jnp.cos(ang)
    sin = jnp.sin(ang) * sgn_ref[...]
    lane = lax.broadcasted_iota(jnp.int32, (1, LANES), 1)
    first_half = lane < MLA_NOPE + MLA_ROPE // 2

    def rope(t):
        partner = jnp.where(first_half, pltpu.roll(t, LANES - MLA_ROPE // 2, 1), pltpu.roll(t, MLA_ROPE // 2, 1))
        return t * cos + partner * sin

    def head_norm(t, g):
        return t * lax.rsqrt(jnp.sum(t * t, axis=-1, keepdims=True) * (1.0 / MLA_QK) + EPS) * g

    mla_scale = LOG2E / math.sqrt(MLA_QK)
    for hd in range(MLA_HEADS):
        sl = slice(hd * HEAD_PAD, (hd + 1) * HEAD_PAD)
        qm_ref[:, sl] = (rope(head_norm(qf[:, sl], gq_ref[:, sl])) * mla_scale).astype(BF16)
        km_ref[:, sl] = rope(head_norm(kf[:, sl] + kr, gk_ref[:, sl])).astype(BF16)


def _pre_call(x2, pos2, bsz, seq, gmix, w1, w1v, w2, w3, gcq, gckv, wuq, wuk, wuv, gq, gk, invf, sgn):
    n, d = x2.shape
    tm = PRE_TILE
    ta = ATTN_TILE
    assert tm % ta == 0 and seq % tm == 0
    nt = seq // ta
    steps = seq // tm
    const = lambda i: (0, 0)
    full = lambda a: pl.BlockSpec(a.shape, const)
    row = lambda w: pl.BlockSpec((tm, w), lambda i: (i, 0))
    vt_spec = lambda w: pl.BlockSpec((None, tm // ta, w, ta), lambda i: (i // steps, i % steps, 0, 0))
    return pl.pallas_call(
        _pre_kernel,
        grid=(n // tm,),
        in_specs=[row(d), row(1), full(gmix), full(w1), full(w1v), full(w2), full(w3), full(gcq), full(gckv),
                  full(wuq), full(wuk), full(wuv), full(gq), full(gk), full(invf), full(sgn)],
        out_specs=[row(2 * SB_WIDTH), vt_spec(SB_WIDTH), row(MLA_HEADS * HEAD_PAD), row(MLA_HEADS * HEAD_PAD),
                   vt_spec(MLA_HEADS * MLA_V), row(2 * d)],
        out_shape=[jax.ShapeDtypeStruct((n, 2 * SB_WIDTH), BF16),
                   jax.ShapeDtypeStruct((bsz, nt, SB_WIDTH, ta), BF16),
                   jax.ShapeDtypeStruct((n, MLA_HEADS * HEAD_PAD), BF16),
                   jax.ShapeDtypeStruct((n, MLA_HEADS * HEAD_PAD), BF16),
                   jax.ShapeDtypeStruct((bsz, nt, MLA_HEADS * MLA_V, ta), BF16),
                   jax.ShapeDtypeStruct((n, 2 * d), BF16)],
        compiler_params=pltpu.CompilerParams(dimension_semantics=("arbitrary",), vmem_limit_bytes=VMEM_LIMIT),
        name="pre",
    )(x2, pos2, gmix, w1, w1v, w2, w3, gcq, gckv, wuq, wuk, wuv, gq, gk, invf, sgn)


def _sb_kernel(q_ref, k_ref, v_ref, o_ref, acc_ref, car_ref):
    t = ATTN_TILE
    i = pl.program_id(1)
    lane = lax.broadcasted_iota(jnp.int32, (1, LANES), 1)
    lo = lane < SB_DIM
    rr = lax.broadcasted_iota(jnp.int32, (t, t), 0)
    cc = lax.broadcasted_iota(jnp.int32, (t, t), 1)
    later_sum = (cc > rr).astype(BF16)
    strict = rr < cc

    acc_ref[...] = jnp.zeros_like(acc_ref)
    car_ref[...] = jnp.zeros_like(car_ref)

    def tile(j, masked):
        krows = pl.ds(pl.multiple_of(j * t, t), t)

        def scores(hd):
            pair = slice((hd // 2) * LANES, (hd // 2 + 1) * LANES)
            q2 = q_ref[:, pair]
            zero = jnp.zeros_like(q2)
            qh = jnp.where(lo, q2, zero) if hd % 2 == 0 else jnp.where(lo, zero, q2)
            return _nt_dot(k_ref[krows, pair], qh)

        def stay(z):
            neg_abs = pltpu.bitcast(pltpu.bitcast(z, jnp.uint32) | jnp.uint32(0x80000000), F32)
            sp = jnp.log2(1.0 + jnp.exp2(neg_abs))
            log_beta = jnp.minimum(z, 0.0) - sp
            log_stay = log_beta - z
            if masked:
                log_stay = jnp.where(strict, log_stay, 0.0)
            later = _dot(later_sum, log_stay.astype(BF16))
            return log_beta, later, jnp.sum(log_stay, axis=0, keepdims=True)

        def weigh(hd, log_beta, later, total):
            carry = car_ref[hd]
            a = jnp.exp2(log_beta + later + carry)
            if masked:
                a = jnp.where(strict, a, 0.0)
            rows = slice(hd * SB_DIM, (hd + 1) * SB_DIM)
            acc_ref[rows, :] += _dot(v_ref[j, rows, :], a.astype(BF16))
            car_ref[hd] = carry + total

        zs, mids = {}, {}
        for step in range(SB_HEADS + 2 * SB_SKEW):
            if step < SB_HEADS:
                zs[step] = scores(step)
            hd = step - SB_SKEW
            if 0 <= hd < SB_HEADS:
                mids[hd] = stay(zs.pop(hd))
            hd = step - 2 * SB_SKEW
            if 0 <= hd < SB_HEADS:
                weigh(hd, *mids.pop(hd))

    tile(i, True)

    def cond(c):
        j, alive = c
        return jnp.logical_and(j >= 0, alive > SB_DEAD)

    def body(c):
        j, _ = c
        tile(j, False)
        return j - 1, jnp.max(car_ref[...])

    lax.while_loop(cond, body, (i - 1, jnp.max(car_ref[...])))
    o_ref[...] = acc_ref[...].astype(o_ref.dtype)


def _sb_call(sb, sbv, bsz, seq):
    t = ATTN_TILE
    nq = seq // t
    return pl.pallas_call(
        _sb_kernel,
        grid=(bsz, nq),
        in_specs=[pl.BlockSpec((t, SB_WIDTH), lambda b, i: (b * nq + i, 0)),
                  pl.BlockSpec((seq, SB_WIDTH), lambda b, i: (b, 1)),
                  pl.BlockSpec((None, nq, SB_WIDTH, t), lambda b, i: (b, 0, 0, 0))],
        out_specs=pl.BlockSpec((None, SB_WIDTH, t), lambda b, i: (b, 0, i)),
        out_shape=jax.ShapeDtypeStruct((bsz, SB_WIDTH, seq), BF16),
        scratch_shapes=[pltpu.VMEM((SB_WIDTH, t), F32), pltpu.VMEM((SB_HEADS, 1, t), F32)],
        compiler_params=pltpu.CompilerParams(dimension_semantics=("arbitrary",) * 2, vmem_limit_bytes=VMEM_LIMIT),
        name="sb_attn",
    )(sb, sb, sbv)


def _mla_kernel(q_ref, k_ref, v_ref, o_ref, m_ref, l_ref, acc_ref, sa_ref, sb_ref):
    t = ATTN_TILE
    i = pl.program_id(1)
    rr = lax.broadcasted_iota(jnp.int32, (t, t), 0)
    cc = lax.broadcasted_iota(jnp.int32, (t, t), 1)
    visible = (rr // CHUNK) <= (cc // CHUNK)

    m_ref[...] = jnp.full_like(m_ref, -jnp.inf)
    l_ref[...] = jnp.zeros_like(l_ref)
    acc_ref[...] = jnp.zeros_like(acc_ref)

    def produce(dst, j):
        krows = pl.ds(pl.multiple_of(j * t, t), t)
        for hd in range(MLA_HEADS):
            sl = slice(hd * HEAD_PAD, (hd + 1) * HEAD_PAD)
            dst[hd] = _nt_dot(k_ref[krows, sl], q_ref[:, sl])

    def consume(src, j, masked):
        for hd in range(MLA_HEADS):
            s = src[hd]
            if masked:
                s = jnp.where(visible, s, -jnp.inf)
            m_old = m_ref[hd]
            m_new = jnp.maximum(m_old, jnp.max(s, axis=0, keepdims=True))
            alpha = jnp.exp2(m_old - m_new)
            p = jnp.exp2(s - m_new)
            l_ref[hd] = alpha * l_ref[hd] + jnp.sum(p, axis=0, keepdims=True)
            rows = slice(hd * MLA_V, (hd + 1) * MLA_V)
            acc_ref[rows, :] = alpha * acc_ref[rows, :] + _dot(v_ref[j, rows, :], p.astype(BF16))
            m_ref[hd] = m_new

    def step(src, dst, j):
        produce(dst, j + 1)
        consume(src, j, False)

    produce(sa_ref, 0)

    def pair(jj, c):
        step(sa_ref, sb_ref, 2 * jj)
        step(sb_ref, sa_ref, 2 * jj + 1)
        return c

    lax.fori_loop(0, i // 2, pair, 0)
    odd = i % 2

    @pl.when(odd == 1)
    def _():
        step(sa_ref, sb_ref, i - 1)
        consume(sb_ref, i, True)

    @pl.when(odd == 0)
    def _():
        consume(sa_ref, i, True)

    for hd in range(MLA_HEADS):
        rows = slice(hd * MLA_V, (hd + 1) * MLA_V)
        o_ref[rows, :] = (acc_ref[rows, :] / l_ref[hd]).astype(o_ref.dtype)


def _mla_call(qm, km, vm, bsz, seq):
    t = ATTN_TILE
    nq = seq // t
    width = MLA_HEADS * HEAD_PAD
    return pl.pallas_call(
        _mla_kernel,
        grid=(bsz, nq),
        in_specs=[pl.BlockSpec((t, width), lambda b, i: (b * nq + i, 0)),
                  pl.BlockSpec((seq, width), lambda b, i: (b, 0)),
                  pl.BlockSpec((None, nq, MLA_HEADS * MLA_V, t), lambda b, i: (b, 0, 0, 0))],
        out_specs=pl.BlockSpec((None, MLA_HEADS * MLA_V, t), lambda b, i: (b, 0, i)),
        out_shape=jax.ShapeDtypeStruct((bsz, MLA_HEADS * MLA_V, seq), BF16),
        scratch_shapes=[pltpu.VMEM((MLA_HEADS, 1, t), F32), pltpu.VMEM((MLA_HEADS, 1, t), F32),
                        pltpu.VMEM((MLA_HEADS * MLA_V, t), F32),
                        pltpu.VMEM((MLA_HEADS, t, t), F32), pltpu.VMEM((MLA_HEADS, t, t), F32)],
        compiler_params=pltpu.CompilerParams(dimension_semantics=("arbitrary",) * 2, vmem_limit_bytes=VMEM_LIMIT),
        name="mla_attn",
    )(qm, km, vm)


def _post_kernel(osb_ref, omla_ref, gate_ref, x_ref, wosb_ref, womla_ref, wout_ref, gffn_ref, wrh_ref, wrl_ref,
                 br_ref, x1_ref, h2_ref, topw_ref, tope_ref, pos_ref, cnt_ref, run_ref):
    tm = TOKEN_TILE
    d = x_ref.shape[1]

    @pl.when(pl.program_id(0) == 0)
    def _():
        run_ref[...] = jnp.zeros_like(run_ref)

    tn = (((0,), (0,)), ((), ()))
    a = lax.dot_general(osb_ref[...], wosb_ref[...], tn, preferred_element_type=F32)
    b = lax.dot_general(omla_ref[...], womla_ref[...], tn, preferred_element_type=F32)
    mixed = gate_ref[:, :d].astype(F32) * a + gate_ref[:, d:].astype(F32) * b
    x1 = x_ref[...] + _dot(mixed.astype(BF16), wout_ref[...])
    x1_ref[...] = x1
    h2 = _rms(x1, gffn_ref[...])
    h2_ref[...] = h2

    h_hi = h2.astype(BF16)
    h_lo = (h2 - h_hi.astype(F32)).astype(BF16)
    logits = (_nt_dot(wrh_ref[...], h_hi) + _nt_dot(wrh_ref[...], h_lo) + _nt_dot(wrl_ref[...], h_hi)
              + br_ref[...])

    eidx = lax.broadcasted_iota(jnp.int32, (N_EXPERTS, tm), 0)
    work = logits
    tops, sels = [], []
    for _ in range(TOP_K):
        mk = jnp.max(work, axis=0, keepdims=True)
        ik = jnp.min(jnp.where(work == mk, eidx, N_EXPERTS), axis=0, keepdims=True)
        sel = eidx == ik
        work = jnp.where(sel, -jnp.inf, work)
        tops.append(mk)
        sels.append(sel)
    exps = [jnp.exp(mk - tops[0]) for mk in tops]
    denom = exps[0] + exps[1] + exps[2] + exps[3]

    chosen = jnp.zeros((N_EXPERTS, tm), F32)
    for sel in sels:
        chosen = chosen + sel.astype(F32)
    rr = lax.broadcasted_iota(jnp.int32, (tm, tm), 0)
    cc = lax.broadcasted_iota(jnp.int32, (tm, tm), 1)
    before = (rr < cc).astype(BF16)
    rank = _dot(chosen.astype(BF16), before) + run_ref[...]
    for k in range(TOP_K):
        topw_ref[k:k + 1, :] = exps[k] / denom
        tope_ref[k:k + 1, :] = jnp.sum(jnp.where(sels[k], eidx, 0), axis=0, keepdims=True)
        pos_ref[k:k + 1, :] = jnp.sum(jnp.where(sels[k], rank, 0.0), axis=0, keepdims=True).astype(jnp.int32)
    run_ref[...] = run_ref[...] + jnp.sum(chosen, axis=1, keepdims=True)
    cnt_ref[...] = jnp.broadcast_to(run_ref[...], cnt_ref.shape)


def _post_call(osb, omla, gates, x2, wosb, womla, wout, gffn, wrh, wrl, br):
    n, d = x2.shape
    tm = TOKEN_TILE
    nt = osb.shape[2] // tm
    const = lambda i: (0, 0)
    full = lambda a: pl.BlockSpec(a.shape, const)
    row = lambda w: pl.BlockSpec((tm, w), lambda i: (i, 0))
    col = lambda: pl.BlockSpec((TOP_K, tm), lambda i: (0, i))
    feat = lambda a: pl.BlockSpec((None, a.shape[1], tm), lambda i: (i // nt, 0, i % nt))
    return pl.pallas_call(
        _post_kernel,
        grid=(n // tm,),
        in_specs=[feat(osb), feat(omla), row(2 * d), row(d), full(wosb), full(womla), full(wout),
                  full(gffn), full(wrh), full(wrl), full(br)],
        out_specs=[row(d), row(d), col(), col(), col(), pl.BlockSpec((N_EXPERTS, LANES), const)],
        out_shape=[jax.ShapeDtypeStruct((n, d), F32), jax.ShapeDtypeStruct((n, d), F32),
                   jax.ShapeDtypeStruct((TOP_K, n), F32), jax.ShapeDtypeStruct((TOP_K, n), jnp.int32),
                   jax.ShapeDtypeStruct((TOP_K, n), jnp.int32), jax.ShapeDtypeStruct((N_EXPERTS, LANES), F32)],
        scratch_shapes=[pltpu.VMEM((N_EXPERTS, 1), F32)],
        compiler_params=pltpu.CompilerParams(dimension_semantics=("arbitrary",), vmem_limit_bytes=VMEM_LIMIT),
        name="post",
    )(osb, omla, gates, x2, wosb, womla, wout, gffn, wrh, wrl, br)


SEG = 8
MAX_SLABS = TOKEN_TILE * TOP_K // SEG + N_EXPERTS
SORT_ROWS = MAX_SLABS * SEG


def _sorted_rows(n):
    nt = n // TOKEN_TILE
    return -(-(n * TOP_K + nt * N_EXPERTS * (SEG - 1)) // EXPERT_TILE) * EXPERT_TILE


def _route(top_e, pos, n):
    tm = TOKEN_TILE
    nt = n // tm
    e_ids = jnp.arange(N_EXPERTS, dtype=jnp.int32)
    onehot = top_e[:, :, None] == e_ids[None, None, :]
    c_tile = jnp.sum(onehot.reshape(TOP_K, nt, tm, N_EXPERTS).astype(jnp.int32), axis=(0, 2))
    slabs = (c_tile + SEG - 1) // SEG
    rank_base = jnp.cumsum(c_tile, axis=0) - c_tile
    row_base = (jnp.cumsum(slabs, axis=0) - slabs) * SEG
    padded = jnp.sum(slabs, axis=0) * SEG
    starts = jnp.cumsum(padded) - padded
    seg_global = starts[None, :] + row_base
    slab_end = jnp.cumsum(slabs, axis=1)
    seg_local = (slab_end - slabs) * SEG

    per_token = lambda tab: jnp.broadcast_to(tab[:, None, :], (nt, tm, N_EXPERTS)).reshape(n, N_EXPERTS)
    local = jnp.sum(jnp.where(onehot, per_token(seg_local - rank_base)[None], 0), axis=2) + pos

    q = jnp.arange(MAX_SLABS, dtype=jnp.int32)
    e_q = jnp.minimum(jnp.sum((slab_end[:, None, :] <= q[None, :, None]).astype(jnp.int32), axis=2), N_EXPERTS - 1)
    hit = e_q[:, :, None] == e_ids[None, None, :]
    pick = lambda tab: jnp.sum(jnp.where(hit, tab[:, None, :], 0), axis=2)
    within = (q[None, :] - pick(slab_end - slabs)) * SEG
    n_slabs = slab_end[:, -1]
    live = q[None, :] < n_slabs[:, None]
    slab_local = jnp.where(live, pick(seg_local) + within, 0).astype(jnp.int32)
    slab_global = jnp.where(live, pick(seg_global) + within, 0).astype(jnp.int32)
    return (padded.astype(jnp.int32), local.astype(jnp.int32), n_slabs.astype(jnp.int32),
            slab_local.reshape(nt, 1, MAX_SLABS), slab_global.reshape(nt, 1, MAX_SLABS))


def _disp_kernel(ns_ref, used_ref, sl_ref, sg_ref, local_ref, h2_ref, xs_hbm, xsort, zeros, sem):
    tm = TOKEN_TILE
    g = pl.program_id(0)
    nt = pl.num_programs(0)
    slot = g % 2

    m_iota = lax.broadcasted_iota(jnp.int32, (SORT_ROWS, tm), 0)
    place = jnp.zeros((SORT_ROWS, tm), F32)
    for k in range(TOP_K):
        place = jnp.where(m_iota == local_ref[k:k + 1, :], 1.0, place)
    xsort[slot] = _dot(place.astype(BF16), h2_ref[...].astype(BF16))

    def slab(q, s):
        return pltpu.make_async_copy(xsort.at[s, pl.ds(pl.multiple_of(sl_ref[0, 0, q], SEG), SEG)],
                                     xs_hbm.at[pl.ds(pl.multiple_of(sg_ref[0, 0, q], SEG), SEG)], sem.at[s])

    def start_body(q, c):
        slab(q, slot).start()
        return c
    lax.fori_loop(0, ns_ref[g], start_body, 0)

    def drain(count, s):
        def wait_body(q, c):
            pltpu.make_async_copy(xsort.at[s, pl.ds(0, SEG)], xs_hbm.at[pl.ds(0, SEG)], sem.at[s]).wait()
            return c
        lax.fori_loop(0, count, wait_body, 0)

    @pl.when(g > 0)
    def _():
        drain(ns_ref[g - 1], 1 - slot)

    @pl.when(g == nt - 1)
    def _():
        drain(ns_ref[g], slot)
        bt = EXPERT_TILE
        zeros[...] = jnp.zeros_like(zeros)
        used = used_ref[0]
        boundary = lax.shift_left(lax.shift_right_logical(used + (bt - 1), bt.bit_length() - 1), bt.bit_length() - 1)
        n_small = lax.shift_right_logical(boundary - used, SEG.bit_length() - 1)
        n_big = lax.shift_right_logical(xs_hbm.shape[0] - boundary, bt.bit_length() - 1)

        def small(q):
            row = pl.multiple_of(used + q * SEG, SEG)
            return pltpu.make_async_copy(zeros.at[pl.ds(0, SEG)], xs_hbm.at[pl.ds(row, SEG)], sem.at[2])

        def big(q):
            row = pl.multiple_of(boundary + q * bt, bt)
            return pltpu.make_async_copy(zeros, xs_hbm.at[pl.ds(row, bt)], sem.at[2])

        for piece, count in ((small, n_small), (big, n_big)):
            lax.fori_loop(0, count, lambda q, c, piece=piece: (piece(q).start(), c)[1], 0)
        for piece, count in ((small, n_small), (big, n_big)):
            lax.fori_loop(0, count, lambda q, c, piece=piece: (piece(q).wait(), c)[1], 0)


def _disp_call(n_slabs, used_rows, slab_local, slab_global, local, h2):
    n, d = h2.shape
    tm = TOKEN_TILE
    assert EXPERT_TILE & (EXPERT_TILE - 1) == 0 and SEG & (SEG - 1) == 0
    lists = pl.BlockSpec((1, 1, MAX_SLABS), lambda g, ns, used: (g, 0, 0), memory_space=pltpu.SMEM)
    grid_spec = pltpu.PrefetchScalarGridSpec(
        num_scalar_prefetch=2,
        grid=(n // tm,),
        in_specs=[lists, lists,
                  pl.BlockSpec((TOP_K, tm), lambda g, ns, used: (0, g)),
                  pl.BlockSpec((tm, d), lambda g, ns, used: (g, 0))],
        out_specs=pl.BlockSpec(memory_space=pl.ANY),
        scratch_shapes=[pltpu.VMEM((2, SORT_ROWS, d), F32), pltpu.VMEM((EXPERT_TILE, d), F32),
                        pltpu.SemaphoreType.DMA((3,))],
    )
    return pl.pallas_call(
        _disp_kernel,
        grid_spec=grid_spec,
        out_shape=jax.ShapeDtypeStruct((_sorted_rows(n), d), F32),
        compiler_params=pltpu.CompilerParams(dimension_semantics=("arbitrary",), vmem_limit_bytes=VMEM_LIMIT),
        name="disp",
    )(n_slabs, used_rows, slab_local, slab_global, local, h2)


def _moe_kernel(blk_ref, xblk_ref, exp_ref, lo_ref, hi_ref, first_ref, new_ref, nxt_ref, slot_ref,
                xs_ref, wup_hbm, bup_ref, wdn_hbm, bdn_ref, y_ref, wupf, wdnf, wupb, wdnp, wdnb, sem):
    t = EXPERT_TILE
    w = pl.program_id(0)
    lo = lo_ref[w]
    hi = hi_ref[w]
    half = wdnb.shape[0]
    d = wdnb.shape[1]

    def weight_copies(e, s):
        return (pltpu.make_async_copy(wup_hbm.at[e], wupf.at[s], sem.at[0, s]),
                pltpu.make_async_copy(wdn_hbm.at[e], wdnf.at[s], sem.at[1, s]))

    @pl.when(w == 0)
    def _():
        for c in weight_copies(exp_ref[0], 0):
            c.start()

    @pl.when(new_ref[w] == 1)
    def _():
        s = slot_ref[w]
        for c in weight_copies(exp_ref[w], s):
            c.wait()

        @pl.when(nxt_ref[w] >= 0)
        def _():
            for c in weight_copies(nxt_ref[w], 1 - s):
                c.start()

        wupb[...] = wupf[s].astype(BF16)
        for c in range(d // LANES):
            sl = slice(c * LANES, (c + 1) * LANES)
            wdnp[c, pl.ds(0, half // 2, stride=2), :] = wdnf[s, :half // 2, sl]
            wdnp[c, pl.ds(1, half // 2, stride=2), :] = wdnf[s, half // 2:, sl]
            wdnb[:, sl] = wdnp[c].astype(BF16)

    @pl.when(hi > lo)
    def _():
        x = xs_ref[...].astype(BF16)
        wup_ref = wupb
        wdn_ref = wdnb
        gu = _dot(x, wup_ref[...]) + bup_ref[...]
        lane = lax.broadcasted_iota(jnp.int32, (1, LANES), 1)
        even = (lane % 2) == 0
        acts = []
        for c in range(half // LANES):
            a = gu[:, c * LANES:(c + 1) * LANES]
            b = gu[:, half + c * LANES:half + (c + 1) * LANES]
            x_glu = jnp.minimum(jnp.where(even, a, pltpu.roll(b, 1, 1)), SWIGLU_LIMIT)
            x_lin = jnp.clip(jnp.where(even, pltpu.roll(a, LANES - 1, 1), b), -SWIGLU_LIMIT, SWIGLU_LIMIT)
            acts.append((x_glu * jax.nn.sigmoid(SWIGLU_ALPHA * x_glu) * (x_lin + 1.0)).astype(BF16))
        act = jnp.concatenate(acts, axis=1)
        y = _dot(act, wdn_ref[...]) + bdn_ref[...]
        row = lax.broadcasted_iota(jnp.int32, (t, 1), 0)
        mine = jnp.logical_and(row >= lo, row < hi)
        keep = jnp.where(first_ref[w] == 1, jnp.zeros_like(y), y_ref[...])
        y_ref[...] = jnp.where(mine, y, keep)

    @pl.when(jnp.logical_and(hi <= lo, first_ref[w] == 1))
    def _():
        y_ref[...] = jnp.zeros_like(y_ref)


def _moe_call(items, xs, wup, bup, wdn, bdn):
    t = EXPERT_TILE
    n_rows, d = xs.shape
    ff = wdn.shape[1]
    n_items = items[0].shape[0]
    by_blk = lambda w, blk, xblk, ex, *_: (blk[w], 0)
    by_xblk = lambda w, blk, xblk, ex, *_: (xblk[w], 0)
    by_exp = lambda w, blk, xblk, ex, *_: (ex[w], 0, 0)
    grid_spec = pltpu.PrefetchScalarGridSpec(
        num_scalar_prefetch=len(items),
        grid=(n_items,),
        in_specs=[pl.BlockSpec((t, d), by_xblk),
                  pl.BlockSpec(memory_space=pl.ANY),
                  pl.BlockSpec((None, 1, 2 * ff), by_exp),
                  pl.BlockSpec(memory_space=pl.ANY),
                  pl.BlockSpec((None, 1, d), by_exp)],
        out_specs=pl.BlockSpec((t, d), by_blk),
        scratch_shapes=[pltpu.VMEM((2, d, 2 * ff), F32),
                        pltpu.VMEM((2, ff, d), F32),
                        pltpu.VMEM((d, 2 * ff), BF16),
                        pltpu.VMEM((d // LANES, ff, LANES), F32),
                        pltpu.VMEM((ff, d), BF16),
                        pltpu.SemaphoreType.DMA((2, 2))],
    )
    return pl.pallas_call(
        _moe_kernel,
        grid_spec=grid_spec,
        out_shape=jax.ShapeDtypeStruct((n_rows, d), F32),
        compiler_params=pltpu.CompilerParams(dimension_semantics=("arbitrary",), vmem_limit_bytes=MOE_VMEM_LIMIT),
        name="moe",
    )(*items, xs, wup, bup, wdn, bdn)


def _moe_items(counts, n_rows):
    t = EXPERT_TILE
    nb = n_rows // t
    n_items = nb + N_EXPERTS - 1
    ends = jnp.cumsum(counts)
    starts = ends - counts
    first_blk = starts // t
    tiles = jnp.where(counts > 0, (ends - 1) // t - first_blk + 1, 0)
    item_end = jnp.cumsum(tiles)
    item_start = item_end - tiles
    total = item_end[-1]
    w = jnp.arange(n_items, dtype=jnp.int32)
    wc = jnp.minimum(w, total - 1)
    ex = jnp.sum((item_end[None, :] <= wc[:, None]).astype(jnp.int32), axis=1)
    onehot = (ex[:, None] == jnp.arange(N_EXPERTS, dtype=jnp.int32)[None, :]).astype(jnp.int32)
    pick = lambda v: jnp.sum(onehot * v[None, :], axis=1)
    xblk = pick(first_blk) + wc - pick(item_start)
    lo = jnp.clip(pick(starts) - xblk * t, 0, t)
    hi = jnp.where(w < total, jnp.clip(pick(ends) - xblk * t, 0, t), lo)
    used_blocks = (ends[-1] + t - 1) // t
    blk = jnp.where(w < total, xblk, jnp.minimum(used_blocks + w - total, nb - 1))
    first = jnp.concatenate([jnp.ones((1,), jnp.int32), (blk[1:] != blk[:-1]).astype(jnp.int32)])
    new = jnp.logical_and(w < total, w == pick(item_start)).astype(jnp.int32)
    e_ids = jnp.arange(N_EXPERTS, dtype=jnp.int32)
    later = jnp.logical_and(e_ids[None, :] > e_ids[:, None], counts[None, :] > 0)
    next_e = jnp.min(jnp.where(later, e_ids[None, :], N_EXPERTS), axis=1)
    next_e = jnp.where(next_e == N_EXPERTS, -1, next_e)
    ordinal = jnp.cumsum((counts > 0).astype(jnp.int32)) - 1
    as_i32 = lambda v: v.astype(jnp.int32)
    return tuple(map(as_i32, (blk, xblk, ex, lo, hi, first, new, pick(next_e), pick(ordinal) % 2)))


def _comb_kernel(ns_ref, sl_ref, sg_ref, nsl_ref, nsg_ref, local_ref, w_ref, y_hbm, x1_ref, o_ref, ysort, sem):
    tm = TOKEN_TILE
    g = pl.program_id(0)
    nt = pl.num_programs(0)
    slot = g % 2

    def slab(lref, gref, q, s):
        return pltpu.make_async_copy(y_hbm.at[pl.ds(pl.multiple_of(gref[0, 0, q], SEG), SEG)],
                                     ysort.at[s, pl.ds(pl.multiple_of(lref[0, 0, q], SEG), SEG)], sem.at[s])

    def fetch(lref, gref, count, s):
        def body(q, c):
            slab(lref, gref, q, s).start()
            return c
        lax.fori_loop(0, count, body, 0)

    @pl.when(g == 0)
    def _():
        ysort[...] = jnp.zeros_like(ysort)
        fetch(sl_ref, sg_ref, ns_ref[0], 0)

    @pl.when(g + 1 < nt)
    def _():
        fetch(nsl_ref, nsg_ref, ns_ref[jnp.minimum(g + 1, nt - 1)], 1 - slot)

    def wait_body(q, c):
        pltpu.make_async_copy(y_hbm.at[pl.ds(0, SEG)], ysort.at[slot, pl.ds(0, SEG)], sem.at[slot]).wait()
        return c
    lax.fori_loop(0, ns_ref[g], wait_body, 0)

    m_iota = lax.broadcasted_iota(jnp.int32, (tm, SORT_ROWS), 1)
    mix = jnp.zeros((tm, SORT_ROWS), F32)
    for k in range(TOP_K):
        mix = jnp.where(m_iota == local_ref[:, k:k + 1], w_ref[:, k:k + 1], mix)
    mix = mix.astype(BF16)
    y = ysort[slot]
    y_hi = y.astype(BF16)
    y_lo = (y - y_hi.astype(F32)).astype(BF16)
    o_ref[...] = x1_ref[...] + _dot(mix, y_hi) + _dot(mix, y_lo)


def _comb_call(n_slabs, slab_local, slab_global, local_t, w_t, y, x1):
    n, d = x1.shape
    tm = TOKEN_TILE
    nt = n // tm
    cur = lambda g, ns: (g, 0, 0)
    nxt = lambda g, ns: (jnp.minimum(g + 1, nt - 1), 0, 0)
    lists = lambda index_map: pl.BlockSpec((1, 1, MAX_SLABS), index_map, memory_space=pltpu.SMEM)
    grid_spec = pltpu.PrefetchScalarGridSpec(
        num_scalar_prefetch=1,
        grid=(nt,),
        in_specs=[lists(cur), lists(cur), lists(nxt), lists(nxt),
                  pl.BlockSpec((tm, TOP_K), lambda g, ns: (g, 0)),
                  pl.BlockSpec((tm, TOP_K), lambda g, ns: (g, 0)),
                  pl.BlockSpec(memory_space=pl.ANY),
                  pl.BlockSpec((tm, d), lambda g, ns: (g, 0))],
        out_specs=pl.BlockSpec((tm, d), lambda g, ns: (g, 0)),
        scratch_shapes=[pltpu.VMEM((2, SORT_ROWS, d), F32), pltpu.SemaphoreType.DMA((2,))],
    )
    return pl.pallas_call(
        _comb_kernel,
        grid_spec=grid_spec,
        out_shape=jax.ShapeDtypeStruct((n, d), F32),
        compiler_params=pltpu.CompilerParams(dimension_semantics=("arbitrary",), vmem_limit_bytes=VMEM_LIMIT),
        name="comb",
    )(n_slabs, slab_local, slab_global, slab_local, slab_global, local_t, w_t, y, x1)


def _layer(x2, pos2, bsz, seq, g_mix, w_in, g_cq, w_uq, g_ckv, w_ukv, g_qnorm, g_knorm, w_o_sb, w_o_mla, w_out,
           g_ffn, w_router, b_router, w_up, b_up, w_down, b_down):
    n, d = x2.shape
    c_sb = 3 * SB_WIDTH
    c_q = c_sb + MLA_Q_LORA
    c_kv = c_q + MLA_KV_LORA
    c_kr = c_kv + MLA_ROPE
    w1 = w_in[:, :2 * SB_WIDTH].astype(BF16)
    w1v = w_in[:, 2 * SB_WIDTH:c_sb].T.astype(BF16)
    w_kr = jnp.pad(w_in[:, c_kv:c_kr], ((0, 0), (MLA_NOPE, HEAD_PAD - MLA_QK)))
    w2 = jnp.concatenate([w_in[:, c_sb:c_kv], w_kr], axis=1).astype(BF16)
    w3 = w_in[:, c_kr:].astype(BF16)
    wuq = jnp.pad(w_uq.reshape(MLA_Q_LORA, MLA_HEADS, MLA_QK), ((0, 0), (0, 0), (0, HEAD_PAD - MLA_QK)))
    wuq = wuq.reshape(MLA_Q_LORA, MLA_HEADS * HEAD_PAD).astype(BF16)
    wukv = w_ukv.reshape(MLA_KV_LORA, MLA_HEADS, MLA_NOPE + MLA_V)
    wuk = jnp.pad(wukv[:, :, :MLA_NOPE], ((0, 0), (0, 0), (0, HEAD_PAD - MLA_NOPE)))
    wuk = wuk.reshape(MLA_KV_LORA, MLA_HEADS * HEAD_PAD).astype(BF16)
    wuv = wukv[:, :, MLA_NOPE:].reshape(MLA_KV_LORA, MLA_HEADS * MLA_V).T.astype(BF16)
    gq = jnp.tile(jnp.pad(g_qnorm, (0, HEAD_PAD - MLA_QK)), MLA_HEADS)[None, :]
    gk = jnp.tile(jnp.pad(g_knorm, (0, HEAD_PAD - MLA_QK)), MLA_HEADS)[None, :]
    half = MLA_ROPE // 2
    inv_freq = ROPE_THETA ** (-jnp.arange(half, dtype=F32) / half)
    invf = jnp.pad(jnp.concatenate([inv_freq, inv_freq]), (MLA_NOPE, HEAD_PAD - MLA_QK))[None, :]
    sgn = jnp.pad(jnp.concatenate([-jnp.ones((half,), F32), jnp.ones((half,), F32)]),
                  (MLA_NOPE, HEAD_PAD - MLA_QK))[None, :]

    sb, sbv, qm, km, vm, gates = _pre_call(x2, pos2, bsz, seq, g_mix[None, :], w1, w1v, w2, w3, g_cq[None, :],
                                           g_ckv[None, :], wuq, wuk, wuv, gq, gk, invf, sgn)
    o_sb = _sb_call(sb, sbv, bsz, seq)
    o_mla = _mla_call(qm, km, vm, bsz, seq)

    wr_t = w_router.T
    wr_hi = wr_t.astype(BF16)
    wr_lo = (wr_t - wr_hi.astype(F32)).astype(BF16)
    x1, h2, top_w, top_e, pos, cnt = _post_call(
        o_sb, o_mla, gates, x2, w_o_sb.astype(BF16), w_o_mla.astype(BF16), w_out.astype(BF16), g_ffn[None, :],
        wr_hi, wr_lo, b_router[:, None])

    del cnt
    padded, local, n_slabs, slab_local, slab_global = _route(top_e, pos, n)
    xs = _disp_call(n_slabs, jnp.sum(padded)[None], slab_local, slab_global, local, h2)
    y = _moe_call(_moe_items(padded, xs.shape[0]), xs, w_up, b_up[:, None, :], w_down, b_down[:, None, :])
    return _comb_call(n_slabs, slab_local, slab_global, local.T, top_w.T, y, x1)


def kernel(x, positions, g_mix, w_in, g_cq, w_uq, g_ckv, w_ukv, g_qnorm, g_knorm, w_o_sb, w_o_mla, w_out, g_ffn,
           w_router, b_router, w_up, b_up, w_down, b_down):
    bsz, seq, d = x.shape
    x2 = x.reshape(bsz * seq, d)
    pos2 = positions.reshape(bsz * seq, 1)
    for l in range(g_mix.shape[0]):
        x2 = _layer(x2, pos2, bsz, seq, g_mix[l], w_in[l], g_cq[l], w_uq[l], g_ckv[l], w_ukv[l], g_qnorm[l],
                    g_knorm[l], w_o_sb[l], w_o_mla[l], w_out[l], g_ffn[l], w_router[l], b_router[l], w_up[l],
                    b_up[l], w_down[l], b_down[l])
    return x2.reshape(bsz, seq, d)
```

```python
import functools
import math

import jax
import jax.numpy as jnp
from jax import lax
from jax.experimental import pallas as pl
from jax.experimental.pallas import tpu as pltpu

F32 = jnp.float32
BF16 = jnp.bfloat16

EPS = 1e-6
CHUNK = 64
SB_HEADS = 8
SB_DIM = 64
SB_WIDTH = SB_HEADS * SB_DIM
MLA_HEADS = 8
MLA_NOPE = 64
MLA_ROPE = 32
MLA_QK = MLA_NOPE + MLA_ROPE
MLA_V = 64
MLA_Q_LORA = 384
MLA_KV_LORA = 256
ROPE_THETA = 10000.0
N_EXPERTS = 32
TOP_K = 4
SWIGLU_LIMIT = 7.0
SWIGLU_ALPHA = 1.702

LANES = 128
HEAD_PAD = LANES
TOKEN_TILE = 256
PRE_TILE = 512
POST_TILE = 512
ATTN_TILE = 256
EXPERT_TILE = 256
MLA_AHEAD = 4
SB_SKEW = 1
LOG2E = math.log2(math.e)
SB_DEAD = -160.0
VMEM_LIMIT = 48 * 1024 * 1024
MOE_VMEM_LIMIT = 56 * 1024 * 1024


def _nt_dot(a, b):
    return lax.dot_general(a, b, (((1,), (1,)), ((), ())), preferred_element_type=F32)


def _dot(a, b):
    return jnp.dot(a, b, preferred_element_type=F32)


def _rms(t, g):
    return t * lax.rsqrt(jnp.mean(t * t, axis=-1, keepdims=True) + EPS) * g


def _pre_kernel(x_ref, pos_ref, gmix_ref, w1_ref, w1v_ref, w2_ref, w3_ref, gcq_ref, gckv_ref, wuq_ref, wuk_ref,
                wuv_ref, gq_ref, gk_ref, invf_ref, sgn_ref,
                sb_ref, sbv_ref, qm_ref, km_ref, vm_ref, gate_ref):
    h = _rms(x_ref[...], gmix_ref[...]).astype(BF16)

    lat = _dot(h, w2_ref[...])
    cqn = _rms(lat[:, :MLA_Q_LORA], gcq_ref[...]).astype(BF16)
    ckvn = _rms(lat[:, MLA_Q_LORA:MLA_Q_LORA + MLA_KV_LORA], gckv_ref[...]).astype(BF16)
    kr = lat[:, MLA_Q_LORA + MLA_KV_LORA:]
    qf = _dot(cqn, wuq_ref[...])
    kf = _dot(ckvn, wuk_ref[...])
    key_tiles = [slice(j * ATTN_TILE, (j + 1) * ATTN_TILE) for j in range(PRE_TILE // ATTN_TILE)]
    for j, rows in enumerate(key_tiles):
        vm_ref[j] = _nt_dot(wuv_ref[...], ckvn[rows, :]).astype(BF16)

    sb_ref[:, :SB_WIDTH] = (_dot(h, w1_ref[:, :SB_WIDTH]) * (LOG2E / math.sqrt(SB_DIM))).astype(BF16)
    sb_ref[:, SB_WIDTH:] = _dot(h, w1_ref[:, SB_WIDTH:]).astype(BF16)
    for j, rows in enumerate(key_tiles):
        sbv_ref[j] = _nt_dot(w1v_ref[...], h[rows, :]).astype(BF16)

    gate_ref[...] = jax.nn.sigmoid(_dot(h, w3_ref[...])).astype(BF16)

    ang = pos_ref[...].astype(F32) * invf_ref[...]
    cos = jnp.cos(ang)
    sin = jnp.sin(ang) * sgn_ref[...]
    lane = lax.broadcasted_iota(jnp.int32, (1, LANES), 1)
    first_half = lane < MLA_NOPE + MLA_ROPE // 2

    def rope(t):
        partner = jnp.where(first_half, pltpu.roll(t, LANES - MLA_ROPE // 2, 1), pltpu.roll(t, MLA_ROPE // 2, 1))
        return t * cos + partner * sin

    def head_norm(t, g):
        return t * lax.rsqrt(jnp.sum(t * t, axis=-1, keepdims=True) * (1.0 / MLA_QK) + EPS) * g

    mla_scale = LOG2E / math.sqrt(MLA_QK)
    for hd in range(MLA_HEADS):
        sl = slice(hd * HEAD_PAD, (hd + 1) * HEAD_PAD)
        qm_ref[:, sl] = (rope(head_norm(qf[:, sl], gq_ref[:, sl])) * mla_scale).astype(BF16)
        km_ref[:, sl] = rope(head_norm(kf[:, sl] + kr, gk_ref[:, sl])).astype(BF16)


def _pre_call(x2, pos2, bsz, seq, gmix, w1, w1v, w2, w3, gcq, gckv, wuq, wuk, wuv, gq, gk, invf, sgn):
    n, d = x2.shape
    tm = PRE_TILE
    ta = ATTN_TILE
    assert tm % ta == 0 and seq % tm == 0
    nt = seq // ta
    steps = seq // tm
    const = lambda i: (0, 0)
    full = lambda a: pl.BlockSpec(a.shape, const)
    row = lambda w: pl.BlockSpec((tm, w), lambda i: (i, 0))
    vt_spec = lambda w: pl.BlockSpec((None, tm // ta, w, ta), lambda i: (i // steps, i % steps, 0, 0))
    return pl.pallas_call(
        _pre_kernel,
        grid=(n // tm,),
        in_specs=[row(d), row(1), full(gmix), full(w1), full(w1v), full(w2), full(w3), full(gcq), full(gckv),
                  full(wuq), full(wuk), full(wuv), full(gq), full(gk), full(invf), full(sgn)],
        out_specs=[row(2 * SB_WIDTH), vt_spec(SB_WIDTH), row(MLA_HEADS * HEAD_PAD), row(MLA_HEADS * HEAD_PAD),
                   vt_spec(MLA_HEADS * MLA_V), row(2 * d)],
        out_shape=[jax.ShapeDtypeStruct((n, 2 * SB_WIDTH), BF16),
                   jax.ShapeDtypeStruct((bsz, nt, SB_WIDTH, ta), BF16),
                   jax.ShapeDtypeStruct((n, MLA_HEADS * HEAD_PAD), BF16),
                   jax.ShapeDtypeStruct((n, MLA_HEADS * HEAD_PAD), BF16),
                   jax.ShapeDtypeStruct((bsz, nt, MLA_HEADS * MLA_V, ta), BF16),
                   jax.ShapeDtypeStruct((n, 2 * d), BF16)],
        compiler_params=pltpu.CompilerParams(dimension_semantics=("arbitrary",), vmem_limit_bytes=VMEM_LIMIT),
        name="pre",
    )(x2, pos2, gmix, w1, w1v, w2, w3, gcq, gckv, wuq, wuk, wuv, gq, gk, invf, sgn)


def _sb_kernel(q_ref, k_ref, v_ref, o_ref, acc_ref, car_ref):
    t = ATTN_TILE
    i = pl.program_id(1)
    lane = lax.broadcasted_iota(jnp.int32, (1, LANES), 1)
    lo = lane < SB_DIM
    rr = lax.broadcasted_iota(jnp.int32, (t, t), 0)
    cc = lax.broadcasted_iota(jnp.int32, (t, t), 1)
    later_sum = (cc > rr).astype(BF16)
    strict = rr < cc

    acc_ref[...] = jnp.zeros_like(acc_ref)
    car_ref[...] = jnp.zeros_like(car_ref)

    def tile(j, masked):
        krows = pl.ds(pl.multiple_of(j * t, t), t)

        def scores(hd):
            pair = slice((hd // 2) * LANES, (hd // 2 + 1) * LANES)
            q2 = q_ref[:, pair]
            zero = jnp.zeros_like(q2)
            qh = jnp.where(lo, q2, zero) if hd % 2 == 0 else jnp.where(lo, zero, q2)
            return _nt_dot(k_ref[krows, pair], qh)

        def stay(z):
            neg_abs = pltpu.bitcast(pltpu.bitcast(z, jnp.uint32) | jnp.uint32(0x80000000), F32)
            sp = jnp.log2(1.0 + jnp.exp2(neg_abs))
            log_beta = jnp.minimum(z, 0.0) - sp
            log_stay = log_beta - z
            if masked:
                log_stay = jnp.where(strict, log_stay, 0.0)
            later = _dot(later_sum, log_stay.astype(BF16))
            return log_beta, later, jnp.sum(log_stay, axis=0, keepdims=True)

        def weigh(hd, log_beta, later, total):
            carry = car_ref[hd]
            a = jnp.exp2(log_beta + later + carry)
            if masked:
                a = jnp.where(strict, a, 0.0)
            rows = slice(hd * SB_DIM, (hd + 1) * SB_DIM)
            acc_ref[rows, :] += _dot(v_ref[j, rows, :], a.astype(BF16))
            car_ref[hd] = carry + total

        zs, mids = {}, {}
        for step in range(SB_HEADS + 2 * SB_SKEW):
            if step < SB_HEADS:
                zs[step] = scores(step)
            hd = step - SB_SKEW
            if 0 <= hd < SB_HEADS:
                mids[hd] = stay(zs.pop(hd))
            hd = step - 2 * SB_SKEW
            if 0 <= hd < SB_HEADS:
                weigh(hd, *mids.pop(hd))

    tile(i, True)

    def cond(c):
        j, alive = c
        return jnp.logical_and(j >= 0, alive > SB_DEAD)

    def body(c):
        j, _ = c
        tile(j, False)
        return j - 1, jnp.max(car_ref[...])

    lax.while_loop(cond, body, (i - 1, jnp.max(car_ref[...])))
    o_ref[...] = acc_ref[...].astype(o_ref.dtype)


def _sb_call(sb, sbv, bsz, seq):
    t = ATTN_TILE
    nq = seq // t
    return pl.pallas_call(
        _sb_kernel,
        grid=(bsz, nq),
        in_specs=[pl.BlockSpec((t, SB_WIDTH), lambda b, i: (b * nq + i, 0)),
                  pl.BlockSpec((seq, SB_WIDTH), lambda b, i: (b, 1)),
                  pl.BlockSpec((None, nq, SB_WIDTH, t), lambda b, i: (b, 0, 0, 0))],
        out_specs=pl.BlockSpec((None, SB_WIDTH, t), lambda b, i: (b, 0, i)),
        out_shape=jax.ShapeDtypeStruct((bsz, SB_WIDTH, seq), BF16),
        scratch_shapes=[pltpu.VMEM((SB_WIDTH, t), F32), pltpu.VMEM((SB_HEADS, 1, t), F32)],
        compiler_params=pltpu.CompilerParams(dimension_semantics=("arbitrary",) * 2, vmem_limit_bytes=VMEM_LIMIT),
        name="sb_attn",
    )(sb, sb, sbv)


def _mla_kernel(q_ref, k_ref, v_ref, o_ref, m_ref, l_ref, acc_ref, sa_ref, sb_ref):
    t = ATTN_TILE
    i = pl.program_id(1)
    rr = lax.broadcasted_iota(jnp.int32, (t, t), 0)
    cc = lax.broadcasted_iota(jnp.int32, (t, t), 1)
    visible = (rr // CHUNK) <= (cc // CHUNK)

    m_ref[...] = jnp.full_like(m_ref, -jnp.inf)
    l_ref[...] = jnp.zeros_like(l_ref)
    acc_ref[...] = jnp.zeros_like(acc_ref)

    def produce(dst, j):
        krows = pl.ds(pl.multiple_of(j * t, t), t)
        for hd in range(MLA_HEADS):
            sl = slice(hd * HEAD_PAD, (hd + 1) * HEAD_PAD)
            dst[hd] = _nt_dot(k_ref[krows, sl], q_ref[:, sl])

    def consume(src, j, masked):
        for hd in range(MLA_HEADS):
            s = src[hd]
            if masked:
                s = jnp.where(visible, s, -jnp.inf)
            m_old = m_ref[hd]
            m_new = jnp.maximum(m_old, jnp.max(s, axis=0, keepdims=True))
            alpha = jnp.exp2(m_old - m_new)
            p = jnp.exp2(s - m_new)
            l_ref[hd] = alpha * l_ref[hd] + jnp.sum(p, axis=0, keepdims=True)
            rows = slice(hd * MLA_V, (hd + 1) * MLA_V)
            acc_ref[rows, :] = alpha * acc_ref[rows, :] + _dot(v_ref[j, rows, :], p.astype(BF16))
            m_ref[hd] = m_new

    def step(src, dst, j):
        produce(dst, j + 1)
        consume(src, j, False)

    produce(sa_ref, 0)

    def pair(jj, c):
        step(sa_ref, sb_ref, 2 * jj)
        step(sb_ref, sa_ref, 2 * jj + 1)
        return c

    lax.fori_loop(0, i // 2, pair, 0)
    odd = i % 2

    @pl.when(odd == 1)
    def _():
        step(sa_ref, sb_ref, i - 1)
        consume(sb_ref, i, True)

    @pl.when(odd == 0)
    def _():
        consume(sa_ref, i, True)

    for hd in range(MLA_HEADS):
        rows = slice(hd * MLA_V, (hd + 1) * MLA_V)
        o_ref[rows, :] = (acc_ref[rows, :] / l_ref[hd]).astype(o_ref.dtype)


def _mla_call(qm, km, vm, bsz, seq):
    t = ATTN_TILE
    nq = seq // t
    width = MLA_HEADS * HEAD_PAD
    return pl.pallas_call(
        _mla_kernel,
        grid=(bsz, nq),
        in_specs=[pl.BlockSpec((t, width), lambda b, i: (b * nq + i, 0)),
                  pl.BlockSpec((seq, width), lambda b, i: (b, 0)),
                  pl.BlockSpec((None, nq, MLA_HEADS * MLA_V, t), lambda b, i: (b, 0, 0, 0))],
        out_specs=pl.BlockSpec((None, MLA_HEADS * MLA_V, t), lambda b, i: (b, 0, i)),
        out_shape=jax.ShapeDtypeStruct((bsz, MLA_HEADS * MLA_V, seq), BF16),
        scratch_shapes=[pltpu.VMEM((MLA_HEADS, 1, t), F32), pltpu.VMEM((MLA_HEADS, 1, t), F32),
                        pltpu.VMEM((MLA_HEADS * MLA_V, t), F32),
                        pltpu.VMEM((MLA_HEADS, t, t), F32), pltpu.VMEM((MLA_HEADS, t, t), F32)],
        compiler_params=pltpu.CompilerParams(dimension_semantics=("arbitrary",) * 2, vmem_limit_bytes=VMEM_LIMIT),
        name="mla_attn",
    )(qm, km, vm)


def _post_kernel(osb_ref, omla_ref, gate_ref, x_ref, wosb_ref, womla_ref, wout_ref, gffn_ref, wrh_ref, wrl_ref,
                 br_ref, x1_ref, h2_ref, topw_ref, tope_ref, pos_ref, run_ref):
    tm = TOKEN_TILE
    d = x_ref.shape[1]

    @pl.when(pl.program_id(0) == 0)
    def _():
        run_ref[...] = jnp.zeros_like(run_ref)

    def project(cols):
        tn = (((0,), (0,)), ((), ()))
        a = lax.dot_general(osb_ref[:, cols], wosb_ref[...], tn, preferred_element_type=F32)
        b = lax.dot_general(omla_ref[:, cols], womla_ref[...], tn, preferred_element_type=F32)
        mixed = gate_ref[cols, :d].astype(F32) * a + gate_ref[cols, d:].astype(F32) * b
        x1 = x_ref[cols, :] + _dot(mixed.astype(BF16), wout_ref[...])
        x1_ref[cols, :] = x1
        return x1

    def route(cols, x1):
        h2 = _rms(x1, gffn_ref[...])
        h2_ref[cols, :] = h2

        h_hi = h2.astype(BF16)
        h_lo = (h2 - h_hi.astype(F32)).astype(BF16)
        logits = (_nt_dot(wrh_ref[...], h_hi) + _nt_dot(wrh_ref[...], h_lo) + _nt_dot(wrl_ref[...], h_hi)
                  + br_ref[...])

        eidx = lax.broadcasted_iota(jnp.int32, (N_EXPERTS, tm), 0)
        work = logits
        tops, sels = [], []
        for _ in range(TOP_K):
            mk = jnp.max(work, axis=0, keepdims=True)
            ik = jnp.min(jnp.where(work == mk, eidx, N_EXPERTS), axis=0, keepdims=True)
            sel = eidx == ik
            work = jnp.where(sel, -jnp.inf, work)
            tops.append(mk)
            sels.append(sel)
        exps = [jnp.exp(mk - tops[0]) for mk in tops]
        denom = exps[0] + exps[1] + exps[2] + exps[3]

        chosen = jnp.zeros((N_EXPERTS, tm), F32)
        for sel in sels:
            chosen = chosen + sel.astype(F32)
        rr = lax.broadcasted_iota(jnp.int32, (tm, tm), 0)
        cc = lax.broadcasted_iota(jnp.int32, (tm, tm), 1)
        before = (rr < cc).astype(BF16)
        rank = _dot(chosen.astype(BF16), before) + run_ref[...]
        for k in range(TOP_K):
            topw_ref[k:k + 1, cols] = exps[k] / denom
            tope_ref[k:k + 1, cols] = jnp.sum(jnp.where(sels[k], eidx, 0), axis=0, keepdims=True)
            pos_ref[k:k + 1, cols] = jnp.sum(jnp.where(sels[k], rank, 0.0), axis=0, keepdims=True).astype(jnp.int32)
        run_ref[...] = run_ref[...] + jnp.sum(chosen, axis=1, keepdims=True)

    subs = [slice(j * tm, (j + 1) * tm) for j in range(POST_TILE // tm)]
    x1s = {}
    for j in range(len(subs) + 1):
        if j < len(subs):
            x1s[j] = project(subs[j])
        if j >= 1:
            route(subs[j - 1], x1s.pop(j - 1))


def _post_call(osb, omla, gates, x2, wosb, womla, wout, gffn, wrh, wrl, br):
    n, d = x2.shape
    tm = POST_TILE
    nt = osb.shape[2] // tm
    const = lambda i: (0, 0)
    full = lambda a: pl.BlockSpec(a.shape, const)
    row = lambda w: pl.BlockSpec((tm, w), lambda i: (i, 0))
    col = lambda: pl.BlockSpec((TOP_K, tm), lambda i: (0, i))
    feat = lambda a: pl.BlockSpec((None, a.shape[1], tm), lambda i: (i // nt, 0, i % nt))
    return pl.pallas_call(
        _post_kernel,
        grid=(n // tm,),
        in_specs=[feat(osb), feat(omla), row(2 * d), row(d), full(wosb), full(womla), full(wout),
                  full(gffn), full(wrh), full(wrl), full(br)],
        out_specs=[row(d), row(d), col(), col(), col()],
        out_shape=[jax.ShapeDtypeStruct((n, d), F32), jax.ShapeDtypeStruct((n, d), F32),
                   jax.ShapeDtypeStruct((TOP_K, n), F32), jax.ShapeDtypeStruct((TOP_K, n), jnp.int32),
                   jax.ShapeDtypeStruct((TOP_K, n), jnp.int32)],
        scratch_shapes=[pltpu.VMEM((N_EXPERTS, 1), F32)],
        compiler_params=pltpu.CompilerParams(dimension_semantics=("arbitrary",), vmem_limit_bytes=VMEM_LIMIT),
        name="post",
    )(osb, omla, gates, x2, wosb, womla, wout, gffn, wrh, wrl, br)


SEG = 8
MAX_SLABS = TOKEN_TILE * TOP_K // SEG + N_EXPERTS
SORT_ROWS = MAX_SLABS * SEG


def _for_each(count, fn, unroll=4):
    shift = unroll.bit_length() - 1
    assert unroll == 1 << shift
    groups = lax.shift_right_logical(count, shift)

    def group(i, c):
        for u in range(unroll):
            fn(i * unroll + u)
        return c
    lax.fori_loop(0, groups, group, 0)

    def single(q, c):
        fn(q)
        return c
    lax.fori_loop(lax.shift_left(groups, shift), count, single, 0)


def _sorted_rows(n):
    nt = n // TOKEN_TILE
    return -(-(n * TOP_K + nt * N_EXPERTS * (SEG - 1)) // EXPERT_TILE) * EXPERT_TILE


def _route(top_e, pos, n):
    tm = TOKEN_TILE
    nt = n // tm
    e_ids = jnp.arange(N_EXPERTS, dtype=jnp.int32)
    onehot = top_e[:, :, None] == e_ids[None, None, :]
    c_tile = jnp.sum(onehot.reshape(TOP_K, nt, tm, N_EXPERTS).astype(jnp.int32), axis=(0, 2))
    slabs = (c_tile + SEG - 1) // SEG
    rank_base = jnp.cumsum(c_tile, axis=0) - c_tile
    row_base = (jnp.cumsum(slabs, axis=0) - slabs) * SEG
    padded = jnp.sum(slabs, axis=0) * SEG
    starts = jnp.cumsum(padded) - padded
    seg_global = starts[None, :] + row_base
    slab_end = jnp.cumsum(slabs, axis=1)
    seg_local = (slab_end - slabs) * SEG

    per_token = lambda tab: jnp.broadcast_to(tab[:, None, :], (nt, tm, N_EXPERTS)).reshape(n, N_EXPERTS)
    local = jnp.sum(jnp.where(onehot, per_token(seg_local - rank_base)[None], 0), axis=2) + pos

    q = jnp.arange(MAX_SLABS, dtype=jnp.int32)
    e_q = jnp.minimum(jnp.sum((slab_end[:, None, :] <= q[None, :, None]).astype(jnp.int32), axis=2), N_EXPERTS - 1)
    hit = e_q[:, :, None] == e_ids[None, None, :]
    pick = lambda tab: jnp.sum(jnp.where(hit, tab[:, None, :], 0), axis=2)
    within = (q[None, :] - pick(slab_end - slabs)) * SEG
    n_slabs = slab_end[:, -1]
    live = q[None, :] < n_slabs[:, None]
    slab_local = jnp.where(live, pick(seg_local) + within, 0).astype(jnp.int32)
    slab_global = jnp.where(live, pick(seg_global) + within, 0).astype(jnp.int32)
    return (padded.astype(jnp.int32), local.astype(jnp.int32), n_slabs.astype(jnp.int32),
            slab_local.reshape(nt, 1, MAX_SLABS), slab_global.reshape(nt, 1, MAX_SLABS))


def _disp_kernel(ns_ref, used_ref, sl_ref, sg_ref, local_ref, h2_ref, xs_hbm, xsort, zeros, sem):
    tm = TOKEN_TILE
    g = pl.program_id(0)
    nt = pl.num_programs(0)
    slot = g % 2

    m_iota = lax.broadcasted_iota(jnp.int32, (SORT_ROWS, tm), 0)
    place = jnp.zeros((SORT_ROWS, tm), F32)
    for k in range(TOP_K):
        place = jnp.where(m_iota == local_ref[k:k + 1, :], 1.0, place)
    xsort[slot] = _dot(place.astype(BF16), h2_ref[...].astype(BF16))

    def slab(q, s):
        return pltpu.make_async_copy(xsort.at[s, pl.ds(pl.multiple_of(sl_ref[0, 0, q], SEG), SEG)],
                                     xs_hbm.at[pl.ds(pl.multiple_of(sg_ref[0, 0, q], SEG), SEG)], sem.at[s])

    _for_each(ns_ref[g], lambda q: slab(q, slot).start())

    def drain(count, s):
        _for_each(count, lambda q: pltpu.make_async_copy(
            xsort.at[s, pl.ds(0, SEG)], xs_hbm.at[pl.ds(0, SEG)], sem.at[s]).wait())

    @pl.when(g > 0)
    def _():
        drain(ns_ref[g - 1], 1 - slot)

    @pl.when(g == nt - 1)
    def _():
        drain(ns_ref[g], slot)
        bt = EXPERT_TILE
        zeros[...] = jnp.zeros_like(zeros)
        used = used_ref[0]
        boundary = lax.shift_left(lax.shift_right_logical(used + (bt - 1), bt.bit_length() - 1), bt.bit_length() - 1)
        n_small = lax.shift_right_logical(boundary - used, SEG.bit_length() - 1)
        n_big = lax.shift_right_logical(xs_hbm.shape[0] - boundary, bt.bit_length() - 1)

        def small(q):
            row = pl.multiple_of(used + q * SEG, SEG)
            return pltpu.make_async_copy(zeros.at[pl.ds(0, SEG)], xs_hbm.at[pl.ds(row, SEG)], sem.at[2])

        def big(q):
            row = pl.multiple_of(boundary + q * bt, bt)
            return pltpu.make_async_copy(zeros, xs_hbm.at[pl.ds(row, bt)], sem.at[2])

        for piece, count in ((small, n_small), (big, n_big)):
            lax.fori_loop(0, count, lambda q, c, piece=piece: (piece(q).start(), c)[1], 0)
        for piece, count in ((small, n_small), (big, n_big)):
            lax.fori_loop(0, count, lambda q, c, piece=piece: (piece(q).wait(), c)[1], 0)


def _disp_call(n_slabs, used_rows, slab_local, slab_global, local, h2):
    n, d = h2.shape
    tm = TOKEN_TILE
    assert EXPERT_TILE & (EXPERT_TILE - 1) == 0 and SEG & (SEG - 1) == 0
    lists = pl.BlockSpec((1, 1, MAX_SLABS), lambda g, ns, used: (g, 0, 0), memory_space=pltpu.SMEM)
    grid_spec = pltpu.PrefetchScalarGridSpec(
        num_scalar_prefetch=2,
        grid=(n // tm,),
        in_specs=[lists, lists,
                  pl.BlockSpec((TOP_K, tm), lambda g, ns, used: (0, g)),
                  pl.BlockSpec((tm, d), lambda g, ns, used: (g, 0))],
        out_specs=pl.BlockSpec(memory_space=pl.ANY),
        scratch_shapes=[pltpu.VMEM((2, SORT_ROWS, d), F32), pltpu.VMEM((EXPERT_TILE, d), F32),
                        pltpu.SemaphoreType.DMA((3,))],
    )
    return pl.pallas_call(
        _disp_kernel,
        grid_spec=grid_spec,
        out_shape=jax.ShapeDtypeStruct((_sorted_rows(n), d), F32),
        compiler_params=pltpu.CompilerParams(dimension_semantics=("arbitrary",), vmem_limit_bytes=VMEM_LIMIT),
        name="disp",
    )(n_slabs, used_rows, slab_local, slab_global, local, h2)


def _moe_kernel(blk_ref, xblk_ref, exp_ref, lo_ref, hi_ref, first_ref, new_ref, nxt_ref, slot_ref,
                xs_ref, wup_hbm, bup_ref, wdn_hbm, bdn_ref, y_ref, wupf, wdnf, wupb, wdnp, wdnb, sem):
    t = EXPERT_TILE
    w = pl.program_id(0)
    lo = lo_ref[w]
    hi = hi_ref[w]
    half = wdnb.shape[0]
    d = wdnb.shape[1]

    def weight_copies(e, s):
        return (pltpu.make_async_copy(wup_hbm.at[e], wupf.at[s], sem.at[0, s]),
                pltpu.make_async_copy(wdn_hbm.at[e], wdnf.at[s], sem.at[1, s]))

    @pl.when(w == 0)
    def _():
        for c in weight_copies(exp_ref[0], 0):
            c.start()

    @pl.when(new_ref[w] == 1)
    def _():
        s = slot_ref[w]
        for c in weight_copies(exp_ref[w], s):
            c.wait()

        @pl.when(nxt_ref[w] >= 0)
        def _():
            for c in weight_copies(nxt_ref[w], 1 - s):
                c.start()

        wupb[...] = wupf[s].astype(BF16)
        for c in range(d // LANES):
            sl = slice(c * LANES, (c + 1) * LANES)
            wdnp[c, pl.ds(0, half // 2, stride=2), :] = wdnf[s, :half // 2, sl]
            wdnp[c, pl.ds(1, half // 2, stride=2), :] = wdnf[s, half // 2:, sl]
            wdnb[:, sl] = wdnp[c].astype(BF16)

    @pl.when(hi > lo)
    def _():
        x = xs_ref[...].astype(BF16)
        wup_ref = wupb
        wdn_ref = wdnb
        gu = _dot(x, wup_ref[...]) + bup_ref[...]
        lane = lax.broadcasted_iota(jnp.int32, (1, LANES), 1)
        even = (lane % 2) == 0
        acts = []
        for c in range(half // LANES):
            a = gu[:, c * LANES:(c + 1) * LANES]
            b = gu[:, half + c * LANES:half + (c + 1) * LANES]
            x_glu = jnp.minimum(jnp.where(even, a, pltpu.roll(b, 1, 1)), SWIGLU_LIMIT)
            x_lin = jnp.clip(jnp.where(even, pltpu.roll(a, LANES - 1, 1), b), -SWIGLU_LIMIT, SWIGLU_LIMIT)
            acts.append((x_glu * jax.nn.sigmoid(SWIGLU_ALPHA * x_glu) * (x_lin + 1.0)).astype(BF16))
        act = jnp.concatenate(acts, axis=1)
        y = _dot(act, wdn_ref[...]) + bdn_ref[...]
        row = lax.broadcasted_iota(jnp.int32, (t, 1), 0)
        mine = jnp.logical_and(row >= lo, row < hi)
        keep = jnp.where(first_ref[w] == 1, jnp.zeros_like(y), y_ref[...])
        y_ref[...] = jnp.where(mine, y, keep)

    @pl.when(jnp.logical_and(hi <= lo, first_ref[w] == 1))
    def _():
        y_ref[...] = jnp.zeros_like(y_ref)


def _moe_call(items, xs, wup, bup, wdn, bdn):
    t = EXPERT_TILE
    n_rows, d = xs.shape
    ff = wdn.shape[1]
    n_items = items[0].shape[0]
    by_blk = lambda w, blk, xblk, ex, *_: (blk[w], 0)
    by_xblk = lambda w, blk, xblk, ex, *_: (xblk[w], 0)
    by_exp = lambda w, blk, xblk, ex, *_: (ex[w], 0, 0)
    grid_spec = pltpu.PrefetchScalarGridSpec(
        num_scalar_prefetch=len(items),
        grid=(n_items,),
        in_specs=[pl.BlockSpec((t, d), by_xblk),
                  pl.BlockSpec(memory_space=pl.ANY),
                  pl.BlockSpec((None, 1, 2 * ff), by_exp),
                  pl.BlockSpec(memory_space=pl.ANY),
                  pl.BlockSpec((None, 1, d), by_exp)],
        out_specs=pl.BlockSpec((t, d), by_blk),
        scratch_shapes=[pltpu.VMEM((2, d, 2 * ff), F32),
                        pltpu.VMEM((2, ff, d), F32),
                        pltpu.VMEM((d, 2 * ff), BF16),
                        pltpu.VMEM((d // LANES, ff, LANES), F32),
                        pltpu.VMEM((ff, d), BF16),
                        pltpu.SemaphoreType.DMA((2, 2))],
    )
    return pl.pallas_call(
        _moe_kernel,
        grid_spec=grid_spec,
        out_shape=jax.ShapeDtypeStruct((n_rows, d), F32),
        compiler_params=pltpu.CompilerParams(dimension_semantics=("arbitrary",), vmem_limit_bytes=MOE_VMEM_LIMIT),
        name="moe",
    )(*items, xs, wup, bup, wdn, bdn)


def _moe_items(counts, n_rows):
    t = EXPERT_TILE
    nb = n_rows // t
    n_items = nb + N_EXPERTS - 1
    ends = jnp.cumsum(counts)
    starts = ends - counts
    first_blk = starts // t
    tiles = jnp.where(counts > 0, (ends - 1) // t - first_blk + 1, 0)
    item_end = jnp.cumsum(tiles)
    item_start = item_end - tiles
    total = item_end[-1]
    w = jnp.arange(n_items, dtype=jnp.int32)
    wc = jnp.minimum(w, total - 1)
    ex = jnp.sum((item_end[None, :] <= wc[:, None]).astype(jnp.int32), axis=1)
    onehot = (ex[:, None] == jnp.arange(N_EXPERTS, dtype=jnp.int32)[None, :]).astype(jnp.int32)
    pick = lambda v: jnp.sum(onehot * v[None, :], axis=1)
    xblk = pick(first_blk) + wc - pick(item_start)
    lo = jnp.clip(pick(starts) - xblk * t, 0, t)
    hi = jnp.where(w < total, jnp.clip(pick(ends) - xblk * t, 0, t), lo)
    used_blocks = (ends[-1] + t - 1) // t
    blk = jnp.where(w < total, xblk, jnp.minimum(used_blocks + w - total, nb - 1))
    first = jnp.concatenate([jnp.ones((1,), jnp.int32), (blk[1:] != blk[:-1]).astype(jnp.int32)])
    new = jnp.logical_and(w < total, w == pick(item_start)).astype(jnp.int32)
    e_ids = jnp.arange(N_EXPERTS, dtype=jnp.int32)
    later = jnp.logical_and(e_ids[None, :] > e_ids[:, None], counts[None, :] > 0)
    next_e = jnp.min(jnp.where(later, e_ids[None, :], N_EXPERTS), axis=1)
    next_e = jnp.where(next_e == N_EXPERTS, -1, next_e)
    ordinal = jnp.cumsum((counts > 0).astype(jnp.int32)) - 1
    as_i32 = lambda v: v.astype(jnp.int32)
    return tuple(map(as_i32, (blk, xblk, ex, lo, hi, first, new, pick(next_e), pick(ordinal) % 2)))


def _comb_kernel(ns_ref, sl_ref, sg_ref, nsl_ref, nsg_ref, local_ref, w_ref, y_hbm, x1_ref, o_ref, ysort, sem):
    tm = TOKEN_TILE
    g = pl.program_id(0)
    nt = pl.num_programs(0)
    slot = g % 2

    def slab(lref, gref, q, s):
        return pltpu.make_async_copy(y_hbm.at[pl.ds(pl.multiple_of(gref[0, 0, q], SEG), SEG)],
                                     ysort.at[s, pl.ds(pl.multiple_of(lref[0, 0, q], SEG), SEG)], sem.at[s])

    def fetch(lref, gref, count, s):
        _for_each(count, lambda q: slab(lref, gref, q, s).start())

    @pl.when(g == 0)
    def _():
        ysort[...] = jnp.zeros_like(ysort)
        fetch(sl_ref, sg_ref, ns_ref[0], 0)

    @pl.when(g + 1 < nt)
    def _():
        fetch(nsl_ref, nsg_ref, ns_ref[jnp.minimum(g + 1, nt - 1)], 1 - slot)

    _for_each(ns_ref[g], lambda q: pltpu.make_async_copy(
        y_hbm.at[pl.ds(0, SEG)], ysort.at[slot, pl.ds(0, SEG)], sem.at[slot]).wait())

    m_iota = lax.broadcasted_iota(jnp.int32, (tm, SORT_ROWS), 1)
    mix = jnp.zeros((tm, SORT_ROWS), F32)
    for k in range(TOP_K):
        mix = jnp.where(m_iota == local_ref[:, k:k + 1], w_ref[:, k:k + 1], mix)
    mix = mix.astype(BF16)
    y = ysort[slot]
    y_hi = y.astype(BF16)
    y_lo = (y - y_hi.astype(F32)).astype(BF16)
    o_ref[...] = x1_ref[...] + _dot(mix, y_hi) + _dot(mix, y_lo)


def _comb_call(n_slabs, slab_local, slab_global, local_t, w_t, y, x1):
    n, d = x1.shape
    tm = TOKEN_TILE
    nt = n // tm
    cur = lambda g, ns: (g, 0, 0)
    nxt = lambda g, ns: (jnp.minimum(g + 1, nt - 1), 0, 0)
    lists = lambda index_map: pl.BlockSpec((1, 1, MAX_SLABS), index_map, memory_space=pltpu.SMEM)
    grid_spec = pltpu.PrefetchScalarGridSpec(
        num_scalar_prefetch=1,
        grid=(nt,),
        in_specs=[lists(cur), lists(cur), lists(nxt), lists(nxt),
                  pl.BlockSpec((tm, TOP_K), lambda g, ns: (g, 0)),
                  pl.BlockSpec((tm, TOP_K), lambda g, ns: (g, 0)),
                  pl.BlockSpec(memory_space=pl.ANY),
                  pl.BlockSpec((tm, d), lambda g, ns: (g, 0))],
        out_specs=pl.BlockSpec((tm, d), lambda g, ns: (g, 0)),
        scratch_shapes=[pltpu.VMEM((2, SORT_ROWS, d), F32), pltpu.SemaphoreType.DMA((2,))],
    )
    return pl.pallas_call(
        _comb_kernel,
        grid_spec=grid_spec,
        out_shape=jax.ShapeDtypeStruct((n, d), F32),
        compiler_params=pltpu.CompilerParams(dimension_semantics=("arbitrary",), vmem_limit_bytes=VMEM_LIMIT),
        name="comb",
    )(n_slabs, slab_local, slab_global, slab_local, slab_global, local_t, w_t, y, x1)


def _layer(x2, pos2, bsz, seq, g_mix, w_in, g_cq, w_uq, g_ckv, w_ukv, g_qnorm, g_knorm, w_o_sb, w_o_mla, w_out,
           g_ffn, w_router, b_router, w_up, b_up, w_down, b_down):
    n, d = x2.shape
    c_sb = 3 * SB_WIDTH
    c_q = c_sb + MLA_Q_LORA
    c_kv = c_q + MLA_KV_LORA
    c_kr = c_kv + MLA_ROPE
    w1 = w_in[:, :2 * SB_WIDTH].astype(BF16)
    w1v = w_in[:, 2 * SB_WIDTH:c_sb].T.astype(BF16)
    w_kr = jnp.pad(w_in[:, c_kv:c_kr], ((0, 0), (MLA_NOPE, HEAD_PAD - MLA_QK)))
    w2 = jnp.concatenate([w_in[:, c_sb:c_kv], w_kr], axis=1).astype(BF16)
    w3 = w_in[:, c_kr:].astype(BF16)
    wuq = jnp.pad(w_uq.reshape(MLA_Q_LORA, MLA_HEADS, MLA_QK), ((0, 0), (0, 0), (0, HEAD_PAD - MLA_QK)))
    wuq = wuq.reshape(MLA_Q_LORA, MLA_HEADS * HEAD_PAD).astype(BF16)
    wukv = w_ukv.reshape(MLA_KV_LORA, MLA_HEADS, MLA_NOPE + MLA_V)
    wuk = jnp.pad(wukv[:, :, :MLA_NOPE], ((0, 0), (0, 0), (0, HEAD_PAD - MLA_NOPE)))
    wuk = wuk.reshape(MLA_KV_LORA, MLA_HEADS * HEAD_PAD).astype(BF16)
    wuv = wukv[:, :, MLA_NOPE:].reshape(MLA_KV_LORA, MLA_HEADS * MLA_V).T.astype(BF16)
    gq = jnp.tile(jnp.pad(g_qnorm, (0, HEAD_PAD - MLA_QK)), MLA_HEADS)[None, :]
    gk = jnp.tile(jnp.pad(g_knorm, (0, HEAD_PAD - MLA_QK)), MLA_HEADS)[None, :]
    half = MLA_ROPE // 2
    inv_freq = ROPE_THETA ** (-jnp.arange(half, dtype=F32) / half)
    invf = jnp.pad(jnp.concatenate([inv_freq, inv_freq]), (MLA_NOPE, HEAD_PAD - MLA_QK))[None, :]
    sgn = jnp.pad(jnp.concatenate([-jnp.ones((half,), F32), jnp.ones((half,), F32)]),
                  (MLA_NOPE, HEAD_PAD - MLA_QK))[None, :]

    sb, sbv, qm, km, vm, gates = _pre_call(x2, pos2, bsz, seq, g_mix[None, :], w1, w1v, w2, w3, g_cq[None, :],
                                           g_ckv[None, :], wuq, wuk, wuv, gq, gk, invf, sgn)
    o_sb = _sb_call(sb, sbv, bsz, seq)
    o_mla = _mla_call(qm, km, vm, bsz, seq)

    wr_t = w_router.T
    wr_hi = wr_t.astype(BF16)
    wr_lo = (wr_t - wr_hi.astype(F32)).astype(BF16)
    x1, h2, top_w, top_e, pos = _post_call(
        o_sb, o_mla, gates, x2, w_o_sb.astype(BF16), w_o_mla.astype(BF16), w_out.astype(BF16), g_ffn[None, :],
        wr_hi, wr_lo, b_router[:, None])

    padded, local, n_slabs, slab_local, slab_global = _route(top_e, pos, n)
    xs = _disp_call(n_slabs, jnp.sum(padded)[None], slab_local, slab_global, local, h2)
    y = _moe_call(_moe_items(padded, xs.shape[0]), xs, w_up, b_up[:, None, :], w_down, b_down[:, None, :])
    return _comb_call(n_slabs, slab_local, slab_global, local.T, top_w.T, y, x1)


def kernel(x, positions, g_mix, w_in, g_cq, w_uq, g_ckv, w_ukv, g_qnorm, g_knorm, w_o_sb, w_o_mla, w_out, g_ffn,
           w_router, b_router, w_up, b_up, w_down, b_down):
    bsz, seq, d = x.shape
    x2 = x.reshape(bsz * seq, d)
    pos2 = positions.reshape(bsz * seq, 1)
    for l in range(g_mix.shape[0]):
        x2 = _layer(x2, pos2, bsz, seq, g_mix[l], w_in[l], g_cq[l], w_uq[l], g_ckv[l], w_ukv[l], g_qnorm[l],
                    g_knorm[l], w_o_sb[l], w_o_mla[l], w_out[l], g_ffn[l], w_router[l], b_router[l], w_up[l],
                    b_up[l], w_down[l], b_down[l])
    return x2.reshape(bsz, seq, d)
```

```python
import functools
import math

import jax
import jax.numpy as jnp
from jax import lax
from jax.experimental import pallas as pl
from jax.experimental.pallas import tpu as pltpu

F32 = jnp.float32
BF16 = jnp.bfloat16

EPS = 1e-6
CHUNK = 64
SB_HEADS = 8
SB_DIM = 64
SB_WIDTH = SB_HEADS * SB_DIM
MLA_HEADS = 8
MLA_NOPE = 64
MLA_ROPE = 32
MLA_QK = MLA_NOPE + MLA_ROPE
MLA_V = 64
MLA_Q_LORA = 384
MLA_KV_LORA = 256
ROPE_THETA = 10000.0
N_EXPERTS = 32
TOP_K = 4
SWIGLU_LIMIT = 7.0
SWIGLU_ALPHA = 1.702

LANES = 128
HEAD_PAD = LANES
TOKEN_TILE = 256
PRE_TILE = 512
POST_TILE = 512
ATTN_TILE = 256
EXPERT_TILE = 256
MLA_AHEAD = 4
SB_SKEW = 1
LOG2E = math.log2(math.e)
SB_DEAD = -160.0
VMEM_LIMIT = 48 * 1024 * 1024
MOE_VMEM_LIMIT = 56 * 1024 * 1024


def _nt_dot(a, b):
    return lax.dot_general(a, b, (((1,), (1,)), ((), ())), preferred_element_type=F32)


def _dot(a, b):
    return jnp.dot(a, b, preferred_element_type=F32)


def _rms(t, g):
    return t * lax.rsqrt(jnp.mean(t * t, axis=-1, keepdims=True) + EPS) * g


def _pre_kernel(x_ref, pos_ref, gmix_ref, w1_ref, w1v_ref, w2_ref, w3_ref, gcq_ref, gckv_ref, wuq_ref, wuk_ref,
                wuv_ref, gq_ref, gk_ref, invf_ref, sgn_ref,
                sb_ref, sbv_ref, qm_ref, km_ref, vm_ref, gate_ref):
    h = _rms(x_ref[...], gmix_ref[...]).astype(BF16)

    lat = _dot(h, w2_ref[...])
    cqn = _rms(lat[:, :MLA_Q_LORA], gcq_ref[...]).astype(BF16)
    ckvn = _rms(lat[:, MLA_Q_LORA:MLA_Q_LORA + MLA_KV_LORA], gckv_ref[...]).astype(BF16)
    kr = lat[:, MLA_Q_LORA + MLA_KV_LORA:]
    qf = _dot(cqn, wuq_ref[...])
    kf = _dot(ckvn, wuk_ref[...])
    key_tiles = [slice(j * ATTN_TILE, (j + 1) * ATTN_TILE) for j in range(PRE_TILE // ATTN_TILE)]
    for j, rows in enumerate(key_tiles):
        vm_ref[j] = _nt_dot(wuv_ref[...], ckvn[rows, :]).astype(BF16)

    sb_ref[:, :SB_WIDTH] = (_dot(h, w1_ref[:, :SB_WIDTH]) * (LOG2E / math.sqrt(SB_DIM))).astype(BF16)
    sb_ref[:, SB_WIDTH:] = _dot(h, w1_ref[:, SB_WIDTH:]).astype(BF16)
    for j, rows in enumerate(key_tiles):
        sbv_ref[j] = _nt_dot(w1v_ref[...], h[rows, :]).astype(BF16)

    gate_ref[...] = jax.nn.sigmoid(_dot(h, w3_ref[...])).astype(BF16)

    ang = pos_ref[...].astype(F32) * invf_ref[...]
    cos = jnp.cos(ang)
    sin = jnp.sin(ang) * sgn_ref[...]
    lane = lax.broadcasted_iota(jnp.int32, (1, LANES), 1)
    first_half = lane < MLA_NOPE + MLA_ROPE // 2

    def rope(t):
        partner = jnp.where(first_half, pltpu.roll(t, LANES - MLA_ROPE // 2, 1), pltpu.roll(t, MLA_ROPE // 2, 1))
        return t * cos + partner * sin

    def head_norm(t, g):
        return t * lax.rsqrt(jnp.sum(t * t, axis=-1, keepdims=True) * (1.0 / MLA_QK) + EPS) * g

    mla_scale = LOG2E / math.sqrt(MLA_QK)
    for hd in range(MLA_HEADS):
        sl = slice(hd * HEAD_PAD, (hd + 1) * HEAD_PAD)
        qm_ref[:, sl] = (rope(head_norm(qf[:, sl], gq_ref[:, sl])) * mla_scale).astype(BF16)
        km_ref[:, sl] = rope(head_norm(kf[:, sl] + kr, gk_ref[:, sl])).astype(BF16)


def _pre_call(x2, pos2, bsz, seq, gmix, w1, w1v, w2, w3, gcq, gckv, wuq, wuk, wuv, gq, gk, invf, sgn):
    n, d = x2.shape
    tm = PRE_TILE
    ta = ATTN_TILE
    assert tm % ta == 0 and seq % tm == 0
    nt = seq // ta
    steps = seq // tm
    const = lambda i: (0, 0)
    full = lambda a: pl.BlockSpec(a.shape, const)
    row = lambda w: pl.BlockSpec((tm, w), lambda i: (i, 0))
    vt_spec = lambda w: pl.BlockSpec((None, tm // ta, w, ta), lambda i: (i // steps, i % steps, 0, 0))
    return pl.pallas_call(
        _pre_kernel,
        grid=(n // tm,),
        in_specs=[row(d), row(1), full(gmix), full(w1), full(w1v), full(w2), full(w3), full(gcq), full(gckv),
                  full(wuq), full(wuk), full(wuv), full(gq), full(gk), full(invf), full(sgn)],
        out_specs=[row(2 * SB_WIDTH), vt_spec(SB_WIDTH), row(MLA_HEADS * HEAD_PAD), row(MLA_HEADS * HEAD_PAD),
                   vt_spec(MLA_HEADS * MLA_V), row(2 * d)],
        out_shape=[jax.ShapeDtypeStruct((n, 2 * SB_WIDTH), BF16),
                   jax.ShapeDtypeStruct((bsz, nt, SB_WIDTH, ta), BF16),
                   jax.ShapeDtypeStruct((n, MLA_HEADS * HEAD_PAD), BF16),
                   jax.ShapeDtypeStruct((n, MLA_HEADS * HEAD_PAD), BF16),
                   jax.ShapeDtypeStruct((bsz, nt, MLA_HEADS * MLA_V, ta), BF16),
                   jax.ShapeDtypeStruct((n, 2 * d), BF16)],
        compiler_params=pltpu.CompilerParams(dimension_semantics=("arbitrary",), vmem_limit_bytes=VMEM_LIMIT),
        name="pre",
    )(x2, pos2, gmix, w1, w1v, w2, w3, gcq, gckv, wuq, wuk, wuv, gq, gk, invf, sgn)


def _sb_kernel(q_ref, k_ref, v_ref, o_ref, acc_ref, car_ref, za_ref, zb_ref):
    t = ATTN_TILE
    i = pl.program_id(1)
    lane = lax.broadcasted_iota(jnp.int32, (1, LANES), 1)
    lo = lane < SB_DIM
    rr = lax.broadcasted_iota(jnp.int32, (t, t), 0)
    cc = lax.broadcasted_iota(jnp.int32, (t, t), 1)
    later_sum = (cc > rr).astype(BF16)
    strict = rr < cc

    acc_ref[...] = jnp.zeros_like(acc_ref)
    car_ref[...] = jnp.zeros_like(car_ref)

    def produce(dst, j, hd):
        krows = pl.ds(pl.multiple_of(jnp.maximum(j, 0) * t, t), t)
        pair = slice((hd // 2) * LANES, (hd // 2 + 1) * LANES)
        q2 = q_ref[:, pair]
        zero = jnp.zeros_like(q2)
        qh = jnp.where(lo, q2, zero) if hd % 2 == 0 else jnp.where(lo, zero, q2)
        dst[hd] = _nt_dot(k_ref[krows, pair], qh)

    def step(src, dst, j, masked):
        def stay(z):
            neg_abs = pltpu.bitcast(pltpu.bitcast(z, jnp.uint32) | jnp.uint32(0x80000000), F32)
            sp = jnp.log2(1.0 + jnp.exp2(neg_abs))
            log_beta = jnp.minimum(z, 0.0) - sp
            log_stay = log_beta - z
            if masked:
                log_stay = jnp.where(strict, log_stay, 0.0)
            later = _dot(later_sum, log_stay.astype(BF16))
            return log_beta, later, jnp.sum(log_stay, axis=0, keepdims=True)

        def weigh(hd, log_beta, later, total):
            carry = car_ref[hd]
            a = jnp.exp2(log_beta + later + carry)
            if masked:
                a = jnp.where(strict, a, 0.0)
            rows = slice(hd * SB_DIM, (hd + 1) * SB_DIM)
            acc_ref[rows, :] += _dot(v_ref[j, rows, :], a.astype(BF16))
            car_ref[hd] = carry + total

        mids = {}
        for s in range(SB_HEADS + SB_SKEW):
            if s < SB_HEADS:
                mids[s] = stay(src[s])
                produce(dst, j - 1, s)
            hd = s - SB_SKEW
            if 0 <= hd < SB_HEADS:
                weigh(hd, *mids.pop(hd))

    for hd in range(SB_HEADS):
        produce(za_ref, i, hd)
    step(za_ref, zb_ref, i, True)

    def cond(c):
        j, alive = c
        return jnp.logical_and(j >= 0, alive > SB_DEAD)

    def body(c):
        j, _ = c
        step(zb_ref, za_ref, j, False)

        @pl.when(jnp.logical_and(j >= 1, jnp.max(car_ref[...]) > SB_DEAD))
        def _():
            step(za_ref, zb_ref, j - 1, False)

        return j - 2, jnp.max(car_ref[...])

    lax.while_loop(cond, body, (i - 1, jnp.max(car_ref[...])))
    o_ref[...] = acc_ref[...].astype(o_ref.dtype)


def _sb_call(sb, sbv, bsz, seq):
    t = ATTN_TILE
    nq = seq // t
    return pl.pallas_call(
        _sb_kernel,
        grid=(bsz, nq),
        in_specs=[pl.BlockSpec((t, SB_WIDTH), lambda b, i: (b * nq + i, 0)),
                  pl.BlockSpec((seq, SB_WIDTH), lambda b, i: (b, 1)),
                  pl.BlockSpec((None, nq, SB_WIDTH, t), lambda b, i: (b, 0, 0, 0))],
        out_specs=pl.BlockSpec((None, SB_WIDTH, t), lambda b, i: (b, 0, i)),
        out_shape=jax.ShapeDtypeStruct((bsz, SB_WIDTH, seq), BF16),
        scratch_shapes=[pltpu.VMEM((SB_WIDTH, t), F32), pltpu.VMEM((SB_HEADS, 1, t), F32),
                        pltpu.VMEM((SB_HEADS, t, t), F32), pltpu.VMEM((SB_HEADS, t, t), F32)],
        compiler_params=pltpu.CompilerParams(dimension_semantics=("arbitrary",) * 2, vmem_limit_bytes=VMEM_LIMIT),
        name="sb_attn",
    )(sb, sb, sbv)


def _mla_kernel(q_ref, k_ref, v_ref, o_ref, m_ref, l_ref, acc_ref, sa_ref, sb_ref):
    t = ATTN_TILE
    i = pl.program_id(1)
    rr = lax.broadcasted_iota(jnp.int32, (t, t), 0)
    cc = lax.broadcasted_iota(jnp.int32, (t, t), 1)
    visible = (rr // CHUNK) <= (cc // CHUNK)

    m_ref[...] = jnp.full_like(m_ref, -jnp.inf)
    l_ref[...] = jnp.zeros_like(l_ref)
    acc_ref[...] = jnp.zeros_like(acc_ref)

    def produce(dst, j, heads=range(MLA_HEADS)):
        krows = pl.ds(pl.multiple_of(j * t, t), t)
        for hd in heads:
            sl = slice(hd * HEAD_PAD, (hd + 1) * HEAD_PAD)
            dst[hd] = _nt_dot(k_ref[krows, sl], q_ref[:, sl])

    def consume(src, j, masked, heads=range(MLA_HEADS)):
        for hd in heads:
            s = src[hd]
            if masked:
                s = jnp.where(visible, s, -jnp.inf)
            m_old = m_ref[hd]
            m_new = jnp.maximum(m_old, jnp.max(s, axis=0, keepdims=True))
            alpha = jnp.exp2(m_old - m_new)
            p = jnp.exp2(s - m_new)
            l_ref[hd] = alpha * l_ref[hd] + jnp.sum(p, axis=0, keepdims=True)
            rows = slice(hd * MLA_V, (hd + 1) * MLA_V)
            acc_ref[rows, :] = alpha * acc_ref[rows, :] + _dot(v_ref[j, rows, :], p.astype(BF16))
            m_ref[hd] = m_new

    def step(src, dst, j):
        for hd in range(MLA_HEADS):
            consume(src, j, False, [hd])
            produce(dst, j + 1, [hd])

    produce(sa_ref, 0)

    def pair(jj, c):
        step(sa_ref, sb_ref, 2 * jj)
        step(sb_ref, sa_ref, 2 * jj + 1)
        return c

    lax.fori_loop(0, i // 2, pair, 0)
    odd = i % 2

    @pl.when(odd == 1)
    def _():
        step(sa_ref, sb_ref, i - 1)
        consume(sb_ref, i, True)

    @pl.when(odd == 0)
    def _():
        consume(sa_ref, i, True)

    for hd in range(MLA_HEADS):
        rows = slice(hd * MLA_V, (hd + 1) * MLA_V)
        o_ref[rows, :] = (acc_ref[rows, :] / l_ref[hd]).astype(o_ref.dtype)


def _mla_call(qm, km, vm, bsz, seq):
    t = ATTN_TILE
    nq = seq // t
    width = MLA_HEADS * HEAD_PAD
    return pl.pallas_call(
        _mla_kernel,
        grid=(bsz, nq),
        in_specs=[pl.BlockSpec((t, width), lambda b, i: (b * nq + i, 0)),
                  pl.BlockSpec((seq, width), lambda b, i: (b, 0)),
                  pl.BlockSpec((None, nq, MLA_HEADS * MLA_V, t), lambda b, i: (b, 0, 0, 0))],
        out_specs=pl.BlockSpec((None, MLA_HEADS * MLA_V, t), lambda b, i: (b, 0, i)),
        out_shape=jax.ShapeDtypeStruct((bsz, MLA_HEADS * MLA_V, seq), BF16),
        scratch_shapes=[pltpu.VMEM((MLA_HEADS, 1, t), F32), pltpu.VMEM((MLA_HEADS, 1, t), F32),
                        pltpu.VMEM((MLA_HEADS * MLA_V, t), F32),
                        pltpu.VMEM((MLA_HEADS, t, t), F32), pltpu.VMEM((MLA_HEADS, t, t), F32)],
        compiler_params=pltpu.CompilerParams(dimension_semantics=("arbitrary",) * 2, vmem_limit_bytes=VMEM_LIMIT),
        name="mla_attn",
    )(qm, km, vm)


def _post_kernel(osb_ref, omla_ref, gate_ref, x_ref, wosb_ref, womla_ref, wout_ref, gffn_ref, wrh_ref, wrl_ref,
                 br_ref, x1_ref, h2_ref, topw_ref, tope_ref, pos_ref, run_ref):
    tm = TOKEN_TILE
    d = x_ref.shape[1]

    @pl.when(pl.program_id(0) == 0)
    def _():
        run_ref[...] = jnp.zeros_like(run_ref)

    def project(cols):
        tn = (((0,), (0,)), ((), ()))
        a = lax.dot_general(osb_ref[:, cols], wosb_ref[...], tn, preferred_element_type=F32)
        b = lax.dot_general(omla_ref[:, cols], womla_ref[...], tn, preferred_element_type=F32)
        mixed = gate_ref[cols, :d].astype(F32) * a + gate_ref[cols, d:].astype(F32) * b
        x1 = x_ref[cols, :] + _dot(mixed.astype(BF16), wout_ref[...])
        x1_ref[cols, :] = x1
        return x1

    def route(cols, x1):
        h2 = _rms(x1, gffn_ref[...])
        h2_ref[cols, :] = h2

        h_hi = h2.astype(BF16)
        h_lo = (h2 - h_hi.astype(F32)).astype(BF16)
        logits = (_nt_dot(wrh_ref[...], h_hi) + _nt_dot(wrh_ref[...], h_lo) + _nt_dot(wrl_ref[...], h_hi)
                  + br_ref[...])

        eidx = lax.broadcasted_iota(jnp.int32, (N_EXPERTS, tm), 0)
        work = logits
        tops, sels = [], []
        for _ in range(TOP_K):
            mk = jnp.max(work, axis=0, keepdims=True)
            ik = jnp.min(jnp.where(work == mk, eidx, N_EXPERTS), axis=0, keepdims=True)
            sel = eidx == ik
            work = jnp.where(sel, -jnp.inf, work)
            tops.append(mk)
            sels.append(sel)
        exps = [jnp.exp(mk - tops[0]) for mk in tops]
        denom = exps[0] + exps[1] + exps[2] + exps[3]

        chosen = jnp.zeros((N_EXPERTS, tm), F32)
        for sel in sels:
            chosen = chosen + sel.astype(F32)
        rr = lax.broadcasted_iota(jnp.int32, (tm, tm), 0)
        cc = lax.broadcasted_iota(jnp.int32, (tm, tm), 1)
        before = (rr < cc).astype(BF16)
        rank = _dot(chosen.astype(BF16), before) + run_ref[...]
        for k in range(TOP_K):
            topw_ref[k:k + 1, cols] = exps[k] / denom
            tope_ref[k:k + 1, cols] = jnp.sum(jnp.where(sels[k], eidx, 0), axis=0, keepdims=True)
            pos_ref[k:k + 1, cols] = jnp.sum(jnp.where(sels[k], rank, 0.0), axis=0, keepdims=True).astype(jnp.int32)
        run_ref[...] = run_ref[...] + jnp.sum(chosen, axis=1, keepdims=True)

    subs = [slice(j * tm, (j + 1) * tm) for j in range(POST_TILE // tm)]
    x1s = {}
    for j in range(len(subs) + 1):
        if j < len(subs):
            x1s[j] = project(subs[j])
        if j >= 1:
            route(subs[j - 1], x1s.pop(j - 1))


def _post_call(osb, omla, gates, x2, wosb, womla, wout, gffn, wrh, wrl, br):
    n, d = x2.shape
    tm = POST_TILE
    nt = osb.shape[2] // tm
    const = lambda i: (0, 0)
    full = lambda a: pl.BlockSpec(a.shape, const)
    row = lambda w: pl.BlockSpec((tm, w), lambda i: (i, 0))
    col = lambda: pl.BlockSpec((TOP_K, tm), lambda i: (0, i))
    feat = lambda a: pl.BlockSpec((None, a.shape[1], tm), lambda i: (i // nt, 0, i % nt))
    return pl.pallas_call(
        _post_kernel,
        grid=(n // tm,),
        in_specs=[feat(osb), feat(omla), row(2 * d), row(d), full(wosb), full(womla), full(wout),
                  full(gffn), full(wrh), full(wrl), full(br)],
        out_specs=[row(d), row(d), col(), col(), col()],
        out_shape=[jax.ShapeDtypeStruct((n, d), F32), jax.ShapeDtypeStruct((n, d), F32),
                   jax.ShapeDtypeStruct((TOP_K, n), F32), jax.ShapeDtypeStruct((TOP_K, n), jnp.int32),
                   jax.ShapeDtypeStruct((TOP_K, n), jnp.int32)],
        scratch_shapes=[pltpu.VMEM((N_EXPERTS, 1), F32)],
        compiler_params=pltpu.CompilerParams(dimension_semantics=("arbitrary",), vmem_limit_bytes=VMEM_LIMIT),
        name="post",
    )(osb, omla, gates, x2, wosb, womla, wout, gffn, wrh, wrl, br)


SEG = 8
MAX_SLABS = TOKEN_TILE * TOP_K // SEG + N_EXPERTS
SORT_ROWS = MAX_SLABS * SEG


def _for_each(count, fn, unroll=4):
    shift = unroll.bit_length() - 1
    assert unroll == 1 << shift
    groups = lax.shift_right_logical(count, shift)

    def group(i, c):
        for u in range(unroll):
            fn(i * unroll + u)
        return c
    lax.fori_loop(0, groups, group, 0)

    def single(q, c):
        fn(q)
        return c
    lax.fori_loop(lax.shift_left(groups, shift), count, single, 0)


def _sorted_rows(n):
    nt = n // TOKEN_TILE
    return -(-(n * TOP_K + nt * N_EXPERTS * (SEG - 1)) // EXPERT_TILE) * EXPERT_TILE


def _route(top_e, pos, n):
    tm = TOKEN_TILE
    nt = n // tm
    e_ids = jnp.arange(N_EXPERTS, dtype=jnp.int32)
    onehot = top_e[:, :, None] == e_ids[None, None, :]
    c_tile = jnp.sum(onehot.reshape(TOP_K, nt, tm, N_EXPERTS).astype(jnp.int32), axis=(0, 2))
    slabs = (c_tile + SEG - 1) // SEG
    rank_base = jnp.cumsum(c_tile, axis=0) - c_tile
    row_base = (jnp.cumsum(slabs, axis=0) - slabs) * SEG
    padded = jnp.sum(slabs, axis=0) * SEG
    starts = jnp.cumsum(padded) - padded
    seg_global = starts[None, :] + row_base
    slab_end = jnp.cumsum(slabs, axis=1)
    seg_local = (slab_end - slabs) * SEG

    per_token = lambda tab: jnp.broadcast_to(tab[:, None, :], (nt, tm, N_EXPERTS)).reshape(n, N_EXPERTS)
    local = jnp.sum(jnp.where(onehot, per_token(seg_local - rank_base)[None], 0), axis=2) + pos

    q = jnp.arange(MAX_SLABS, dtype=jnp.int32)
    e_q = jnp.minimum(jnp.sum((slab_end[:, None, :] <= q[None, :, None]).astype(jnp.int32), axis=2), N_EXPERTS - 1)
    hit = e_q[:, :, None] == e_ids[None, None, :]
    pick = lambda tab: jnp.sum(jnp.where(hit, tab[:, None, :], 0), axis=2)
    within = (q[None, :] - pick(slab_end - slabs)) * SEG
    n_slabs = slab_end[:, -1]
    live = q[None, :] < n_slabs[:, None]
    slab_local = jnp.where(live, pick(seg_local) + within, 0).astype(jnp.int32)
    slab_global = jnp.where(live, pick(seg_global) + within, 0).astype(jnp.int32)
    return (padded.astype(jnp.int32), local.astype(jnp.int32), n_slabs.astype(jnp.int32),
            slab_local.reshape(nt, 1, MAX_SLABS), slab_global.reshape(nt, 1, MAX_SLABS))


def _disp_kernel(ns_ref, used_ref, sl_ref, sg_ref, local_ref, h2_ref, xs_hbm, xsort, zeros, sem):
    tm = TOKEN_TILE
    g = pl.program_id(0)
    nt = pl.num_programs(0)
    slot = g % 2

    m_iota = lax.broadcasted_iota(jnp.int32, (SORT_ROWS, tm), 0)
    place = jnp.zeros((SORT_ROWS, tm), F32)
    for k in range(TOP_K):
        place = jnp.where(m_iota == local_ref[k:k + 1, :], 1.0, place)
    xsort[slot] = _dot(place.astype(BF16), h2_ref[...].astype(BF16))

    def slab(q, s):
        return pltpu.make_async_copy(xsort.at[s, pl.ds(pl.multiple_of(sl_ref[0, 0, q], SEG), SEG)],
                                     xs_hbm.at[pl.ds(pl.multiple_of(sg_ref[0, 0, q], SEG), SEG)], sem.at[s])

    _for_each(ns_ref[g], lambda q: slab(q, slot).start())

    def drain(count, s):
        _for_each(count, lambda q: pltpu.make_async_copy(
            xsort.at[s, pl.ds(0, SEG)], xs_hbm.at[pl.ds(0, SEG)], sem.at[s]).wait())

    @pl.when(g > 0)
    def _():
        drain(ns_ref[g - 1], 1 - slot)

    @pl.when(g == nt - 1)
    def _():
        drain(ns_ref[g], slot)
        bt = EXPERT_TILE
        zeros[...] = jnp.zeros_like(zeros)
        used = used_ref[0]
        boundary = lax.shift_left(lax.shift_right_logical(used + (bt - 1), bt.bit_length() - 1), bt.bit_length() - 1)
        n_small = lax.shift_right_logical(boundary - used, SEG.bit_length() - 1)
        n_big = lax.shift_right_logical(xs_hbm.shape[0] - boundary, bt.bit_length() - 1)

        def small(q):
            row = pl.multiple_of(used + q * SEG, SEG)
            return pltpu.make_async_copy(zeros.at[pl.ds(0, SEG)], xs_hbm.at[pl.ds(row, SEG)], sem.at[2])

        def big(q):
            row = pl.multiple_of(boundary + q * bt, bt)
            return pltpu.make_async_copy(zeros, xs_hbm.at[pl.ds(row, bt)], sem.at[2])

        for piece, count in ((small, n_small), (big, n_big)):
            lax.fori_loop(0, count, lambda q, c, piece=piece: (piece(q).start(), c)[1], 0)
        for piece, count in ((small, n_small), (big, n_big)):
            lax.fori_loop(0, count, lambda q, c, piece=piece: (piece(q).wait(), c)[1], 0)


def _disp_call(n_slabs, used_rows, slab_local, slab_global, local, h2):
    n, d = h2.shape
    tm = TOKEN_TILE
    assert EXPERT_TILE & (EXPERT_TILE - 1) == 0 and SEG & (SEG - 1) == 0
    lists = pl.BlockSpec((1, 1, MAX_SLABS), lambda g, ns, used: (g, 0, 0), memory_space=pltpu.SMEM)
    grid_spec = pltpu.PrefetchScalarGridSpec(
        num_scalar_prefetch=2,
        grid=(n // tm,),
        in_specs=[lists, lists,
                  pl.BlockSpec((TOP_K, tm), lambda g, ns, used: (0, g)),
                  pl.BlockSpec((tm, d), lambda g, ns, used: (g, 0))],
        out_specs=pl.BlockSpec(memory_space=pl.ANY),
        scratch_shapes=[pltpu.VMEM((2, SORT_ROWS, d), F32), pltpu.VMEM((EXPERT_TILE, d), F32),
                        pltpu.SemaphoreType.DMA((3,))],
    )
    return pl.pallas_call(
        _disp_kernel,
        grid_spec=grid_spec,
        out_shape=jax.ShapeDtypeStruct((_sorted_rows(n), d), F32),
        compiler_params=pltpu.CompilerParams(dimension_semantics=("arbitrary",), vmem_limit_bytes=VMEM_LIMIT),
        name="disp",
    )(n_slabs, used_rows, slab_local, slab_global, local, h2)


def _moe_kernel(blk_ref, xblk_ref, exp_ref, lo_ref, hi_ref, first_ref, new_ref, nxt_ref, slot_ref,
                xs_ref, wup_hbm, bup_ref, wdn_hbm, bdn_ref, y_ref, wupf, wdnf, wupb, wdnp, wdnb, sem):
    t = EXPERT_TILE
    w = pl.program_id(0)
    lo = lo_ref[w]
    hi = hi_ref[w]
    half = wdnb.shape[0]
    d = wdnb.shape[1]

    def weight_copies(e, s):
        return (pltpu.make_async_copy(wup_hbm.at[e], wupf.at[s], sem.at[0, s]),
                pltpu.make_async_copy(wdn_hbm.at[e], wdnf.at[s], sem.at[1, s]))

    @pl.when(w == 0)
    def _():
        for c in weight_copies(exp_ref[0], 0):
            c.start()

    @pl.when(new_ref[w] == 1)
    def _():
        s = slot_ref[w]
        for c in weight_copies(exp_ref[w], s):
            c.wait()

        @pl.when(nxt_ref[w] >= 0)
        def _():
            for c in weight_copies(nxt_ref[w], 1 - s):
                c.start()

        wupb[...] = wupf[s].astype(BF16)
        for c in range(d // LANES):
            sl = slice(c * LANES, (c + 1) * LANES)
            wdnp[c, pl.ds(0, half // 2, stride=2), :] = wdnf[s, :half // 2, sl]
            wdnp[c, pl.ds(1, half // 2, stride=2), :] = wdnf[s, half // 2:, sl]
            wdnb[:, sl] = wdnp[c].astype(BF16)

    @pl.when(hi > lo)
    def _():
        x = xs_ref[...].astype(BF16)
        wup_ref = wupb
        wdn_ref = wdnb
        gu = _dot(x, wup_ref[...]) + bup_ref[...]
        lane = lax.broadcasted_iota(jnp.int32, (1, LANES), 1)
        even = (lane % 2) == 0
        acts = []
        for c in range(half // LANES):
            a = gu[:, c * LANES:(c + 1) * LANES]
            b = gu[:, half + c * LANES:half + (c + 1) * LANES]
            x_glu = jnp.minimum(jnp.where(even, a, pltpu.roll(b, 1, 1)), SWIGLU_LIMIT)
            x_lin = jnp.clip(jnp.where(even, pltpu.roll(a, LANES - 1, 1), b), -SWIGLU_LIMIT, SWIGLU_LIMIT)
            acts.append((x_glu * jax.nn.sigmoid(SWIGLU_ALPHA * x_glu) * (x_lin + 1.0)).astype(BF16))
        act = jnp.concatenate(acts, axis=1)
        y = _dot(act, wdn_ref[...]) + bdn_ref[...]
        row = lax.broadcasted_iota(jnp.int32, (t, 1), 0)
        mine = jnp.logical_and(row >= lo, row < hi)
        keep = jnp.where(first_ref[w] == 1, jnp.zeros_like(y), y_ref[...])
        y_ref[...] = jnp.where(mine, y, keep)

    @pl.when(jnp.logical_and(hi <= lo, first_ref[w] == 1))
    def _():
        y_ref[...] = jnp.zeros_like(y_ref)


def _moe_call(items, xs, wup, bup, wdn, bdn):
    t = EXPERT_TILE
    n_rows, d = xs.shape
    ff = wdn.shape[1]
    n_items = items[0].shape[0]
    by_blk = lambda w, blk, xblk, ex, *_: (blk[w], 0)
    by_xblk = lambda w, blk, xblk, ex, *_: (xblk[w], 0)
    by_exp = lambda w, blk, xblk, ex, *_: (ex[w], 0, 0)
    grid_spec = pltpu.PrefetchScalarGridSpec(
        num_scalar_prefetch=len(items),
        grid=(n_items,),
        in_specs=[pl.BlockSpec((t, d), by_xblk),
                  pl.BlockSpec(memory_space=pl.ANY),
                  pl.BlockSpec((None, 1, 2 * ff), by_exp),
                  pl.BlockSpec(memory_space=pl.ANY),
                  pl.BlockSpec((None, 1, d), by_exp)],
        out_specs=pl.BlockSpec((t, d), by_blk),
        scratch_shapes=[pltpu.VMEM((2, d, 2 * ff), F32),
                        pltpu.VMEM((2, ff, d), F32),
                        pltpu.VMEM((d, 2 * ff), BF16),
                        pltpu.VMEM((d // LANES, ff, LANES), F32),
                        pltpu.VMEM((ff, d), BF16),
                        pltpu.SemaphoreType.DMA((2, 2))],
    )
    return pl.pallas_call(
        _moe_kernel,
        grid_spec=grid_spec,
        out_shape=jax.ShapeDtypeStruct((n_rows, d), F32),
        compiler_params=pltpu.CompilerParams(dimension_semantics=("arbitrary",), vmem_limit_bytes=MOE_VMEM_LIMIT),
        name="moe",
    )(*items, xs, wup, bup, wdn, bdn)


def _moe_items(counts, n_rows):
    t = EXPERT_TILE
    nb = n_rows // t
    n_items = nb + N_EXPERTS - 1
    ends = jnp.cumsum(counts)
    starts = ends - counts
    first_blk = starts // t
    tiles = jnp.where(counts > 0, (ends - 1) // t - first_blk + 1, 0)
    item_end = jnp.cumsum(tiles)
    item_start = item_end - tiles
    total = item_end[-1]
    w = jnp.arange(n_items, dtype=jnp.int32)
    wc = jnp.minimum(w, total - 1)
    ex = jnp.sum((item_end[None, :] <= wc[:, None]).astype(jnp.int32), axis=1)
    onehot = (ex[:, None] == jnp.arange(N_EXPERTS, dtype=jnp.int32)[None, :]).astype(jnp.int32)
    pick = lambda v: jnp.sum(onehot * v[None, :], axis=1)
    xblk = pick(first_blk) + wc - pick(item_start)
    lo = jnp.clip(pick(starts) - xblk * t, 0, t)
    hi = jnp.where(w < total, jnp.clip(pick(ends) - xblk * t, 0, t), lo)
    used_blocks = (ends[-1] + t - 1) // t
    blk = jnp.where(w < total, xblk, jnp.minimum(used_blocks + w - total, nb - 1))
    first = jnp.concatenate([jnp.ones((1,), jnp.int32), (blk[1:] != blk[:-1]).astype(jnp.int32)])
    new = jnp.logical_and(w < total, w == pick(item_start)).astype(jnp.int32)
    e_ids = jnp.arange(N_EXPERTS, dtype=jnp.int32)
    later = jnp.logical_and(e_ids[None, :] > e_ids[:, None], counts[None, :] > 0)
    next_e = jnp.min(jnp.where(later, e_ids[None, :], N_EXPERTS), axis=1)
    next_e = jnp.where(next_e == N_EXPERTS, -1, next_e)
    ordinal = jnp.cumsum((counts > 0).astype(jnp.int32)) - 1
    as_i32 = lambda v: v.astype(jnp.int32)
    return tuple(map(as_i32, (blk, xblk, ex, lo, hi, first, new, pick(next_e), pick(ordinal) % 2)))


def _comb_kernel(ns_ref, sl_ref, sg_ref, nsl_ref, nsg_ref, local_ref, w_ref, y_hbm, x1_ref, o_ref, ysort, sem):
    tm = TOKEN_TILE
    g = pl.program_id(0)
    nt = pl.num_programs(0)
    slot = g % 2

    def slab(lref, gref, q, s):
        return pltpu.make_async_copy(y_hbm.at[pl.ds(pl.multiple_of(gref[0, 0, q], SEG), SEG)],
                                     ysort.at[s, pl.ds(pl.multiple_of(lref[0, 0, q], SEG), SEG)], sem.at[s])

    def fetch(lref, gref, count, s):
        _for_each(count, lambda q: slab(lref, gref, q, s).start())

    @pl.when(g == 0)
    def _():
        ysort[...] = jnp.zeros_like(ysort)
        fetch(sl_ref, sg_ref, ns_ref[0], 0)

    @pl.when(g + 1 < nt)
    def _():
        fetch(nsl_ref, nsg_ref, ns_ref[jnp.minimum(g + 1, nt - 1)], 1 - slot)

    _for_each(ns_ref[g], lambda q: pltpu.make_async_copy(
        y_hbm.at[pl.ds(0, SEG)], ysort.at[slot, pl.ds(0, SEG)], sem.at[slot]).wait())

    m_iota = lax.broadcasted_iota(jnp.int32, (tm, SORT_ROWS), 1)
    mix = jnp.zeros((tm, SORT_ROWS), F32)
    for k in range(TOP_K):
        mix = jnp.where(m_iota == local_ref[:, k:k + 1], w_ref[:, k:k + 1], mix)
    mix = mix.astype(BF16)
    y = ysort[slot]
    y_hi = y.astype(BF16)
    y_lo = (y - y_hi.astype(F32)).astype(BF16)
    o_ref[...] = x1_ref[...] + _dot(mix, y_hi) + _dot(mix, y_lo)


def _comb_call(n_slabs, slab_local, slab_global, local_t, w_t, y, x1):
    n, d = x1.shape
    tm = TOKEN_TILE
    nt = n // tm
    cur = lambda g, ns: (g, 0, 0)
    nxt = lambda g, ns: (jnp.minimum(g + 1, nt - 1), 0, 0)
    lists = lambda index_map: pl.BlockSpec((1, 1, MAX_SLABS), index_map, memory_space=pltpu.SMEM)
    grid_spec = pltpu.PrefetchScalarGridSpec(
        num_scalar_prefetch=1,
        grid=(nt,),
        in_specs=[lists(cur), lists(cur), lists(nxt), lists(nxt),
                  pl.BlockSpec((tm, TOP_K), lambda g, ns: (g, 0)),
                  pl.BlockSpec((tm, TOP_K), lambda g, ns: (g, 0)),
                  pl.BlockSpec(memory_space=pl.ANY),
                  pl.BlockSpec((tm, d), lambda g, ns: (g, 0))],
        out_specs=pl.BlockSpec((tm, d), lambda g, ns: (g, 0)),
        scratch_shapes=[pltpu.VMEM((2, SORT_ROWS, d), F32), pltpu.SemaphoreType.DMA((2,))],
    )
    return pl.pallas_call(
        _comb_kernel,
        grid_spec=grid_spec,
        out_shape=jax.ShapeDtypeStruct((n, d), F32),
        compiler_params=pltpu.CompilerParams(dimension_semantics=("arbitrary",), vmem_limit_bytes=VMEM_LIMIT),
        name="comb",
    )(n_slabs, slab_local, slab_global, slab_local, slab_global, local_t, w_t, y, x1)


def _layer(x2, pos2, bsz, seq, g_mix, w_in, g_cq, w_uq, g_ckv, w_ukv, g_qnorm, g_knorm, w_o_sb, w_o_mla, w_out,
           g_ffn, w_router, b_router, w_up, b_up, w_down, b_down):
    n, d = x2.shape
    c_sb = 3 * SB_WIDTH
    c_q = c_sb + MLA_Q_LORA
    c_kv = c_q + MLA_KV_LORA
    c_kr = c_kv + MLA_ROPE
    w1 = w_in[:, :2 * SB_WIDTH].astype(BF16)
    w1v = w_in[:, 2 * SB_WIDTH:c_sb].T.astype(BF16)
    w_kr = jnp.pad(w_in[:, c_kv:c_kr], ((0, 0), (MLA_NOPE, HEAD_PAD - MLA_QK)))
    w2 = jnp.concatenate([w_in[:, c_sb:c_kv], w_kr], axis=1).astype(BF16)
    w3 = w_in[:, c_kr:].astype(BF16)
    wuq = jnp.pad(w_uq.reshape(MLA_Q_LORA, MLA_HEADS, MLA_QK), ((0, 0), (0, 0), (0, HEAD_PAD - MLA_QK)))
    wuq = wuq.reshape(MLA_Q_LORA, MLA_HEADS * HEAD_PAD).astype(BF16)
    wukv = w_ukv.reshape(MLA_KV_LORA, MLA_HEADS, MLA_NOPE + MLA_V)
    wuk = jnp.pad(wukv[:, :, :MLA_NOPE], ((0, 0), (0, 0), (0, HEAD_PAD - MLA_NOPE)))
    wuk = wuk.reshape(MLA_KV_LORA, MLA_HEADS * HEAD_PAD).astype(BF16)
    wuv = wukv[:, :, MLA_NOPE:].reshape(MLA_KV_LORA, MLA_HEADS * MLA_V).T.astype(BF16)
    gq = jnp.tile(jnp.pad(g_qnorm, (0, HEAD_PAD - MLA_QK)), MLA_HEADS)[None, :]
    gk = jnp.tile(jnp.pad(g_knorm, (0, HEAD_PAD - MLA_QK)), MLA_HEADS)[None, :]
    half = MLA_ROPE // 2
    inv_freq = ROPE_THETA ** (-jnp.arange(half, dtype=F32) / half)
    invf = jnp.pad(jnp.concatenate([inv_freq, inv_freq]), (MLA_NOPE, HEAD_PAD - MLA_QK))[None, :]
    sgn = jnp.pad(jnp.concatenate([-jnp.ones((half,), F32), jnp.ones((half,), F32)]),
                  (MLA_NOPE, HEAD_PAD - MLA_QK))[None, :]

    sb, sbv, qm, km, vm, gates = _pre_call(x2, pos2, bsz, seq, g_mix[None, :], w1, w1v, w2, w3, g_cq[None, :],
                                           g_ckv[None, :], wuq, wuk, wuv, gq, gk, invf, sgn)
    o_sb = _sb_call(sb, sbv, bsz, seq)
    o_mla = _mla_call(qm, km, vm, bsz, seq)

    wr_t = w_router.T
    wr_hi = wr_t.astype(BF16)
    wr_lo = (wr_t - wr_hi.astype(F32)).astype(BF16)
    x1, h2, top_w, top_e, pos = _post_call(
        o_sb, o_mla, gates, x2, w_o_sb.astype(BF16), w_o_mla.astype(BF16), w_out.astype(BF16), g_ffn[None, :],
        wr_hi, wr_lo, b_router[:, None])

    padded, local, n_slabs, slab_local, slab_global = _route(top_e, pos, n)
    xs = _disp_call(n_slabs, jnp.sum(padded)[None], slab_local, slab_global, local, h2)
    y = _moe_call(_moe_items(padded, xs.shape[0]), xs, w_up, b_up[:, None, :], w_down, b_down[:, None, :])
    return _comb_call(n_slabs, slab_local, slab_global, local.T, top_w.T, y, x1)


def kernel(x, positions, g_mix, w_in, g_cq, w_uq, g_ckv, w_ukv, g_qnorm, g_knorm, w_o_sb, w_o_mla, w_out, g_ffn,
           w_router, b_router, w_up, b_up, w_down, b_down):
    bsz, seq, d = x.shape
    x2 = x.reshape(bsz * seq, d)
    pos2 = positions.reshape(bsz * seq, 1)
    for l in range(g_mix.shape[0]):
        x2 = _layer(x2, pos2, bsz, seq, g_mix[l], w_in[l], g_cq[l], w_uq[l], g_ckv[l], w_ukv[l], g_qnorm[l],
                    g_knorm[l], w_o_sb[l], w_o_mla[l], w_out[l], g_ffn[l], w_router[l], b_router[l], w_up[l],
                    b_up[l], w_down[l], b_down[l])
    return x2.reshape(bsz, seq, d)
```

```python
import math

import jax
import jax.numpy as jnp
from jax import lax
from jax.experimental import pallas as pl
from jax.experimental.pallas import tpu as pltpu

F32 = jnp.float32
BF16 = jnp.bfloat16

EPS = 1e-6
CHUNK = 64
SB_HEADS = 8
SB_DIM = 64
SB_WIDTH = SB_HEADS * SB_DIM
MLA_HEADS = 8
MLA_NOPE = 64
MLA_ROPE = 32
MLA_QK = MLA_NOPE + MLA_ROPE
MLA_V = 64
MLA_Q_LORA = 384
MLA_KV_LORA = 256
ROPE_THETA = 10000.0
N_EXPERTS = 32
TOP_K = 4
SWIGLU_LIMIT = 7.0
SWIGLU_ALPHA = 1.702

LANES = 128
HEAD_PAD = LANES
TOKEN_TILE = 256
PRE_TILE = 512
POST_TILE = 512
ATTN_TILE = 256
EXPERT_TILE = 256
SB_SKEW = 1
LOG2E = math.log2(math.e)
SB_DEAD = -160.0
VMEM_LIMIT = 48 * 1024 * 1024
MOE_VMEM_LIMIT = 56 * 1024 * 1024


def _nt_dot(a, b):
    return lax.dot_general(a, b, (((1,), (1,)), ((), ())), preferred_element_type=F32)


def _dot(a, b):
    return jnp.dot(a, b, preferred_element_type=F32)


def _rms(t, g):
    return t * lax.rsqrt(jnp.mean(t * t, axis=-1, keepdims=True) + EPS) * g


def _pre_kernel(x_ref, pos_ref, gmix_ref, w1_ref, w1v_ref, w2_ref, w3_ref, gcq_ref, gckv_ref, wuq_ref, wuk_ref,
                wuv_ref, gq_ref, gk_ref, invf_ref, sgn_ref,
                sb_ref, sbv_ref, qm_ref, km_ref, vm_ref, gate_ref):
    h = _rms(x_ref[...], gmix_ref[...]).astype(BF16)

    lat = _dot(h, w2_ref[...])
    cqn = _rms(lat[:, :MLA_Q_LORA], gcq_ref[...]).astype(BF16)
    ckvn = _rms(lat[:, MLA_Q_LORA:MLA_Q_LORA + MLA_KV_LORA], gckv_ref[...]).astype(BF16)
    kr = lat[:, MLA_Q_LORA + MLA_KV_LORA:]
    qf = _dot(cqn, wuq_ref[...])
    kf = _dot(ckvn, wuk_ref[...])
    key_tiles = [slice(j * ATTN_TILE, (j + 1) * ATTN_TILE) for j in range(PRE_TILE // ATTN_TILE)]
    for j, rows in enumerate(key_tiles):
        vm_ref[j] = _nt_dot(wuv_ref[...], ckvn[rows, :]).astype(BF16)

    sb_ref[:, :SB_WIDTH] = (_dot(h, w1_ref[:, :SB_WIDTH]) * (LOG2E / math.sqrt(SB_DIM))).astype(BF16)
    sb_ref[:, SB_WIDTH:] = _dot(h, w1_ref[:, SB_WIDTH:]).astype(BF16)
    for j, rows in enumerate(key_tiles):
        sbv_ref[j] = _nt_dot(w1v_ref[...], h[rows, :]).astype(BF16)

    gate_ref[...] = jax.nn.sigmoid(_dot(h, w3_ref[...])).astype(BF16)

    ang = pos_ref[...].astype(F32) * invf_ref[...]
    cos = jnp.cos(ang)
    sin = jnp.sin(ang) * sgn_ref[...]
    lane = lax.broadcasted_iota(jnp.int32, (1, LANES), 1)
    first_half = lane < MLA_NOPE + MLA_ROPE // 2

    def rope(t):
        partner = jnp.where(first_half, pltpu.roll(t, LANES - MLA_ROPE // 2, 1), pltpu.roll(t, MLA_ROPE // 2, 1))
        return t * cos + partner * sin

    def head_norm(t, g):
        return t * lax.rsqrt(jnp.sum(t * t, axis=-1, keepdims=True) * (1.0 / MLA_QK) + EPS) * g

    mla_scale = LOG2E / math.sqrt(MLA_QK)
    for hd in range(MLA_HEADS):
        sl = slice(hd * HEAD_PAD, (hd + 1) * HEAD_PAD)
        qm_ref[:, sl] = (rope(head_norm(qf[:, sl], gq_ref[:, sl])) * mla_scale).astype(BF16)
        km_ref[:, sl] = rope(head_norm(kf[:, sl] + kr, gk_ref[:, sl])).astype(BF16)


def _pre_call(x2, pos2, bsz, seq, gmix, w1, w1v, w2, w3, gcq, gckv, wuq, wuk, wuv, gq, gk, invf, sgn):
    n, d = x2.shape
    tm = PRE_TILE
    ta = ATTN_TILE
    assert tm % ta == 0 and seq % tm == 0
    nt = seq // ta
    steps = seq // tm
    const = lambda i: (0, 0)
    full = lambda a: pl.BlockSpec(a.shape, const)
    row = lambda w: pl.BlockSpec((tm, w), lambda i: (i, 0))
    vt_spec = lambda w: pl.BlockSpec((None, tm // ta, w, ta), lambda i: (i // steps, i % steps, 0, 0))
    return pl.pallas_call(
        _pre_kernel,
        grid=(n // tm,),
        in_specs=[row(d), row(1), full(gmix), full(w1), full(w1v), full(w2), full(w3), full(gcq), full(gckv),
                  full(wuq), full(wuk), full(wuv), full(gq), full(gk), full(invf), full(sgn)],
        out_specs=[row(2 * SB_WIDTH), vt_spec(SB_WIDTH), row(MLA_HEADS * HEAD_PAD), row(MLA_HEADS * HEAD_PAD),
                   vt_spec(MLA_HEADS * MLA_V), row(2 * d)],
        out_shape=[jax.ShapeDtypeStruct((n, 2 * SB_WIDTH), BF16),
                   jax.ShapeDtypeStruct((bsz, nt, SB_WIDTH, ta), BF16),
                   jax.ShapeDtypeStruct((n, MLA_HEADS * HEAD_PAD), BF16),
                   jax.ShapeDtypeStruct((n, MLA_HEADS * HEAD_PAD), BF16),
                   jax.ShapeDtypeStruct((bsz, nt, MLA_HEADS * MLA_V, ta), BF16),
                   jax.ShapeDtypeStruct((n, 2 * d), BF16)],
        compiler_params=pltpu.CompilerParams(dimension_semantics=("arbitrary",), vmem_limit_bytes=VMEM_LIMIT),
        name="pre",
    )(x2, pos2, gmix, w1, w1v, w2, w3, gcq, gckv, wuq, wuk, wuv, gq, gk, invf, sgn)


def _sb_kernel(q_ref, k_ref, v_ref, o_ref, acc_ref, car_ref, za_ref, zb_ref):
    t = ATTN_TILE
    i = pl.program_id(1)
    lane = lax.broadcasted_iota(jnp.int32, (1, LANES), 1)
    lo = lane < SB_DIM
    rr = lax.broadcasted_iota(jnp.int32, (t, t), 0)
    cc = lax.broadcasted_iota(jnp.int32, (t, t), 1)
    later_sum = (cc > rr).astype(BF16)
    strict = rr < cc

    acc_ref[...] = jnp.zeros_like(acc_ref)
    car_ref[...] = jnp.zeros_like(car_ref)

    def produce(dst, j, hd):
        krows = pl.ds(pl.multiple_of(jnp.maximum(j, 0) * t, t), t)
        pair = slice((hd // 2) * LANES, (hd // 2 + 1) * LANES)
        q2 = q_ref[:, pair]
        zero = jnp.zeros_like(q2)
        qh = jnp.where(lo, q2, zero) if hd % 2 == 0 else jnp.where(lo, zero, q2)
        dst[hd] = _nt_dot(k_ref[krows, pair], qh)

    def step(src, dst, j, masked):
        def stay(z):
            neg_abs = pltpu.bitcast(pltpu.bitcast(z, jnp.uint32) | jnp.uint32(0x80000000), F32)
            sp = jnp.log2(1.0 + jnp.exp2(neg_abs))
            log_beta = jnp.minimum(z, 0.0) - sp
            log_stay = log_beta - z
            if masked:
                log_stay = jnp.where(strict, log_stay, 0.0)
            later = _dot(later_sum, log_stay.astype(BF16))
            return log_beta, later, jnp.sum(log_stay, axis=0, keepdims=True)

        def weigh(hd, log_beta, later, total):
            carry = car_ref[hd]
            a = jnp.exp2(log_beta + later + carry)
            if masked:
                a = jnp.where(strict, a, 0.0)
            rows = slice(hd * SB_DIM, (hd + 1) * SB_DIM)
            acc_ref[rows, :] += _dot(v_ref[j, rows, :], a.astype(BF16))
            car_ref[hd] = carry + total

        mids = {}
        for s in range(SB_HEADS + SB_SKEW):
            if s < SB_HEADS:
                mids[s] = stay(src[s])
                produce(dst, j - 1, s)
            hd = s - SB_SKEW
            if 0 <= hd < SB_HEADS:
                weigh(hd, *mids.pop(hd))

    for hd in range(SB_HEADS):
        produce(za_ref, i, hd)
    step(za_ref, zb_ref, i, True)

    def cond(c):
        j, alive = c
        return jnp.logical_and(j >= 0, alive > SB_DEAD)

    def body(c):
        j, _ = c
        step(zb_ref, za_ref, j, False)

        @pl.when(jnp.logical_and(j >= 1, jnp.max(car_ref[...]) > SB_DEAD))
        def _():
            step(za_ref, zb_ref, j - 1, False)

        return j - 2, jnp.max(car_ref[...])

    lax.while_loop(cond, body, (i - 1, jnp.max(car_ref[...])))
    o_ref[...] = acc_ref[...].astype(o_ref.dtype)


def _sb_call(sb, sbv, bsz, seq):
    t = ATTN_TILE
    nq = seq // t
    return pl.pallas_call(
        _sb_kernel,
        grid=(bsz, nq),
        in_specs=[pl.BlockSpec((t, SB_WIDTH), lambda b, i: (b * nq + i, 0)),
                  pl.BlockSpec((seq, SB_WIDTH), lambda b, i: (b, 1)),
                  pl.BlockSpec((None, nq, SB_WIDTH, t), lambda b, i: (b, 0, 0, 0))],
        out_specs=pl.BlockSpec((None, SB_WIDTH, t), lambda b, i: (b, 0, i)),
        out_shape=jax.ShapeDtypeStruct((bsz, SB_WIDTH, seq), BF16),
        scratch_shapes=[pltpu.VMEM((SB_WIDTH, t), F32), pltpu.VMEM((SB_HEADS, 1, t), F32),
                        pltpu.VMEM((SB_HEADS, t, t), F32), pltpu.VMEM((SB_HEADS, t, t), F32)],
        compiler_params=pltpu.CompilerParams(dimension_semantics=("arbitrary",) * 2, vmem_limit_bytes=VMEM_LIMIT),
        name="sb_attn",
    )(sb, sb, sbv)


def _mla_kernel(q_ref, k_ref, v_ref, o_ref, m_ref, l_ref, acc_ref, sa_ref, sb_ref):
    t = ATTN_TILE
    i = pl.program_id(1)
    rr = lax.broadcasted_iota(jnp.int32, (t, t), 0)
    cc = lax.broadcasted_iota(jnp.int32, (t, t), 1)
    visible = (rr // CHUNK) <= (cc // CHUNK)

    m_ref[...] = jnp.full_like(m_ref, -jnp.inf)
    l_ref[...] = jnp.zeros_like(l_ref)
    acc_ref[...] = jnp.zeros_like(acc_ref)

    def produce(dst, j, heads=range(MLA_HEADS)):
        krows = pl.ds(pl.multiple_of(j * t, t), t)
        for hd in heads:
            sl = slice(hd * HEAD_PAD, (hd + 1) * HEAD_PAD)
            dst[hd] = _nt_dot(k_ref[krows, sl], q_ref[:, sl])

    def consume(src, j, masked, heads=range(MLA_HEADS)):
        for hd in heads:
            s = src[hd]
            if masked:
                s = jnp.where(visible, s, -jnp.inf)
            m_old = m_ref[hd]
            m_new = jnp.maximum(m_old, jnp.max(s, axis=0, keepdims=True))
            alpha = jnp.exp2(m_old - m_new)
            p = jnp.exp2(s - m_new)
            l_ref[hd] = alpha * l_ref[hd] + jnp.sum(p, axis=0, keepdims=True)
            rows = slice(hd * MLA_V, (hd + 1) * MLA_V)
            acc_ref[rows, :] = alpha * acc_ref[rows, :] + _dot(v_ref[j, rows, :], p.astype(BF16))
            m_ref[hd] = m_new

    def step(src, dst, j):
        for hd in range(MLA_HEADS):
            consume(src, j, False, [hd])
            produce(dst, j + 1, [hd])

    produce(sa_ref, 0)

    def pair(jj, c):
        step(sa_ref, sb_ref, 2 * jj)
        step(sb_ref, sa_ref, 2 * jj + 1)
        return c

    lax.fori_loop(0, i // 2, pair, 0)
    odd = i % 2

    @pl.when(odd == 1)
    def _():
        step(sa_ref, sb_ref, i - 1)
        consume(sb_ref, i, True)

    @pl.when(odd == 0)
    def _():
        consume(sa_ref, i, True)

    for hd in range(MLA_HEADS):
        rows = slice(hd * MLA_V, (hd + 1) * MLA_V)
        o_ref[rows, :] = (acc_ref[rows, :] / l_ref[hd]).astype(o_ref.dtype)


def _mla_call(qm, km, vm, bsz, seq):
    t = ATTN_TILE
    nq = seq // t
    width = MLA_HEADS * HEAD_PAD
    return pl.pallas_call(
        _mla_kernel,
        grid=(bsz, nq),
        in_specs=[pl.BlockSpec((t, width), lambda b, i: (b * nq + i, 0)),
                  pl.BlockSpec((seq, width), lambda b, i: (b, 0)),
                  pl.BlockSpec((None, nq, MLA_HEADS * MLA_V, t), lambda b, i: (b, 0, 0, 0))],
        out_specs=pl.BlockSpec((None, MLA_HEADS * MLA_V, t), lambda b, i: (b, 0, i)),
        out_shape=jax.ShapeDtypeStruct((bsz, MLA_HEADS * MLA_V, seq), BF16),
        scratch_shapes=[pltpu.VMEM((MLA_HEADS, 1, t), F32), pltpu.VMEM((MLA_HEADS, 1, t), F32),
                        pltpu.VMEM((MLA_HEADS * MLA_V, t), F32),
                        pltpu.VMEM((MLA_HEADS, t, t), F32), pltpu.VMEM((MLA_HEADS, t, t), F32)],
        compiler_params=pltpu.CompilerParams(dimension_semantics=("arbitrary",) * 2, vmem_limit_bytes=VMEM_LIMIT),
        name="mla_attn",
    )(qm, km, vm)


def _post_kernel(osb_ref, omla_ref, gate_ref, x_ref, wosb_ref, womla_ref, wout_ref, gffn_ref, wrh_ref, wrl_ref,
                 br_ref, x1_ref, h2_ref, topw_ref, tope_ref, pos_ref, run_ref):
    tm = TOKEN_TILE
    d = x_ref.shape[1]

    @pl.when(pl.program_id(0) == 0)
    def _():
        run_ref[...] = jnp.zeros_like(run_ref)

    def project(cols):
        tn = (((0,), (0,)), ((), ()))
        a = lax.dot_general(osb_ref[:, cols], wosb_ref[...], tn, preferred_element_type=F32)
        b = lax.dot_general(omla_ref[:, cols], womla_ref[...], tn, preferred_element_type=F32)
        mixed = gate_ref[cols, :d].astype(F32) * a + gate_ref[cols, d:].astype(F32) * b
        x1 = x_ref[cols, :] + _dot(mixed.astype(BF16), wout_ref[...])
        x1_ref[cols, :] = x1
        return x1

    def route(cols, x1):
        h2 = _rms(x1, gffn_ref[...])
        h2_ref[cols, :] = h2

        h_hi = h2.astype(BF16)
        h_lo = (h2 - h_hi.astype(F32)).astype(BF16)
        logits = (_nt_dot(wrh_ref[...], h_hi) + _nt_dot(wrh_ref[...], h_lo) + _nt_dot(wrl_ref[...], h_hi)
                  + br_ref[...])

        eidx = lax.broadcasted_iota(jnp.int32, (N_EXPERTS, tm), 0)
        work = logits
        tops, sels = [], []
        for _ in range(TOP_K):
            mk = jnp.max(work, axis=0, keepdims=True)
            ik = jnp.min(jnp.where(work == mk, eidx, N_EXPERTS), axis=0, keepdims=True)
            sel = eidx == ik
            work = jnp.where(sel, -jnp.inf, work)
            tops.append(mk)
            sels.append(sel)
        exps = [jnp.exp(mk - tops[0]) for mk in tops]
        denom = exps[0] + exps[1] + exps[2] + exps[3]

        chosen = jnp.zeros((N_EXPERTS, tm), F32)
        for sel in sels:
            chosen = chosen + sel.astype(F32)
        rr = lax.broadcasted_iota(jnp.int32, (tm, tm), 0)
        cc = lax.broadcasted_iota(jnp.int32, (tm, tm), 1)
        before = (rr < cc).astype(BF16)
        rank = _dot(chosen.astype(BF16), before) + run_ref[...]
        for k in range(TOP_K):
            topw_ref[k:k + 1, cols] = exps[k] / denom
            tope_ref[k:k + 1, cols] = jnp.sum(jnp.where(sels[k], eidx, 0), axis=0, keepdims=True)
            pos_ref[k:k + 1, cols] = jnp.sum(jnp.where(sels[k], rank, 0.0), axis=0, keepdims=True).astype(jnp.int32)
        run_ref[...] = run_ref[...] + jnp.sum(chosen, axis=1, keepdims=True)

    subs = [slice(j * tm, (j + 1) * tm) for j in range(POST_TILE // tm)]
    x1s = {}
    for j in range(len(subs) + 1):
        if j < len(subs):
            x1s[j] = project(subs[j])
        if j >= 1:
            route(subs[j - 1], x1s.pop(j - 1))


def _post_call(osb, omla, gates, x2, wosb, womla, wout, gffn, wrh, wrl, br):
    n, d = x2.shape
    tm = POST_TILE
    nt = osb.shape[2] // tm
    const = lambda i: (0, 0)
    full = lambda a: pl.BlockSpec(a.shape, const)
    row = lambda w: pl.BlockSpec((tm, w), lambda i: (i, 0))
    col = lambda: pl.BlockSpec((TOP_K, tm), lambda i: (0, i))
    feat = lambda a: pl.BlockSpec((None, a.shape[1], tm), lambda i: (i // nt, 0, i % nt))
    return pl.pallas_call(
        _post_kernel,
        grid=(n // tm,),
        in_specs=[feat(osb), feat(omla), row(2 * d), row(d), full(wosb), full(womla), full(wout),
                  full(gffn), full(wrh), full(wrl), full(br)],
        out_specs=[row(d), row(d), col(), col(), col()],
        out_shape=[jax.ShapeDtypeStruct((n, d), F32), jax.ShapeDtypeStruct((n, d), F32),
                   jax.ShapeDtypeStruct((TOP_K, n), F32), jax.ShapeDtypeStruct((TOP_K, n), jnp.int32),
                   jax.ShapeDtypeStruct((TOP_K, n), jnp.int32)],
        scratch_shapes=[pltpu.VMEM((N_EXPERTS, 1), F32)],
        compiler_params=pltpu.CompilerParams(dimension_semantics=("arbitrary",), vmem_limit_bytes=VMEM_LIMIT),
        name="post",
    )(osb, omla, gates, x2, wosb, womla, wout, gffn, wrh, wrl, br)


SEG = 8
MAX_SLABS = TOKEN_TILE * TOP_K // SEG + N_EXPERTS
SORT_ROWS = MAX_SLABS * SEG


def _for_each(count, fn, unroll=4):
    shift = unroll.bit_length() - 1
    assert unroll == 1 << shift
    groups = lax.shift_right_logical(count, shift)

    def group(i, c):
        for u in range(unroll):
            fn(i * unroll + u)
        return c
    lax.fori_loop(0, groups, group, 0)

    def single(q, c):
        fn(q)
        return c
    lax.fori_loop(lax.shift_left(groups, shift), count, single, 0)


def _sorted_rows(n):
    nt = n // TOKEN_TILE
    return -(-(n * TOP_K + nt * N_EXPERTS * (SEG - 1)) // EXPERT_TILE) * EXPERT_TILE


def _route(top_e, pos, n):
    tm = TOKEN_TILE
    nt = n // tm
    e_ids = jnp.arange(N_EXPERTS, dtype=jnp.int32)
    onehot = top_e[:, :, None] == e_ids[None, None, :]
    c_tile = jnp.sum(onehot.reshape(TOP_K, nt, tm, N_EXPERTS).astype(jnp.int32), axis=(0, 2))
    slabs = (c_tile + SEG - 1) // SEG
    rank_base = jnp.cumsum(c_tile, axis=0) - c_tile
    row_base = (jnp.cumsum(slabs, axis=0) - slabs) * SEG
    padded = jnp.sum(slabs, axis=0) * SEG
    starts = jnp.cumsum(padded) - padded
    seg_global = starts[None, :] + row_base
    slab_end = jnp.cumsum(slabs, axis=1)
    seg_local = (slab_end - slabs) * SEG

    per_token = lambda tab: jnp.broadcast_to(tab[:, None, :], (nt, tm, N_EXPERTS)).reshape(n, N_EXPERTS)
    local = jnp.sum(jnp.where(onehot, per_token(seg_local - rank_base)[None], 0), axis=2) + pos

    q = jnp.arange(MAX_SLABS, dtype=jnp.int32)
    e_q = jnp.minimum(jnp.sum((slab_end[:, None, :] <= q[None, :, None]).astype(jnp.int32), axis=2), N_EXPERTS - 1)
    hit = e_q[:, :, None] == e_ids[None, None, :]
    pick = lambda tab: jnp.sum(jnp.where(hit, tab[:, None, :], 0), axis=2)
    within = (q[None, :] - pick(slab_end - slabs)) * SEG
    n_slabs = slab_end[:, -1]
    live = q[None, :] < n_slabs[:, None]
    slab_local = jnp.where(live, pick(seg_local) + within, 0).astype(jnp.int32)
    slab_global = jnp.where(live, pick(seg_global) + within, 0).astype(jnp.int32)
    return (padded.astype(jnp.int32), local.astype(jnp.int32), n_slabs.astype(jnp.int32),
            slab_local.reshape(nt, 1, MAX_SLABS), slab_global.reshape(nt, 1, MAX_SLABS))


def _disp_kernel(ns_ref, used_ref, sl_ref, sg_ref, local_ref, h2_ref, xs_hbm, xsort, zeros, sem):
    tm = TOKEN_TILE
    g = pl.program_id(0)
    nt = pl.num_programs(0)
    slot = g % 2

    m_iota = lax.broadcasted_iota(jnp.int32, (SORT_ROWS, tm), 0)
    place = jnp.zeros((SORT_ROWS, tm), F32)
    for k in range(TOP_K):
        place = jnp.where(m_iota == local_ref[k:k + 1, :], 1.0, place)
    xsort[slot] = _dot(place.astype(BF16), h2_ref[...].astype(BF16))

    def slab(q, s):
        return pltpu.make_async_copy(xsort.at[s, pl.ds(pl.multiple_of(sl_ref[0, 0, q], SEG), SEG)],
                                     xs_hbm.at[pl.ds(pl.multiple_of(sg_ref[0, 0, q], SEG), SEG)], sem.at[s])

    _for_each(ns_ref[g], lambda q: slab(q, slot).start())

    def drain(count, s):
        _for_each(count, lambda q: pltpu.make_async_copy(
            xsort.at[s, pl.ds(0, SEG)], xs_hbm.at[pl.ds(0, SEG)], sem.at[s]).wait())

    @pl.when(g > 0)
    def _():
        drain(ns_ref[g - 1], 1 - slot)

    @pl.when(g == nt - 1)
    def _():
        drain(ns_ref[g], slot)
        bt = EXPERT_TILE
        zeros[...] = jnp.zeros_like(zeros)
        used = used_ref[0]
        boundary = lax.shift_left(lax.shift_right_logical(used + (bt - 1), bt.bit_length() - 1), bt.bit_length() - 1)
        n_small = lax.shift_right_logical(boundary - used, SEG.bit_length() - 1)
        n_big = lax.shift_right_logical(xs_hbm.shape[0] - boundary, bt.bit_length() - 1)

        def small(q):
            row = pl.multiple_of(used + q * SEG, SEG)
            return pltpu.make_async_copy(zeros.at[pl.ds(0, SEG)], xs_hbm.at[pl.ds(row, SEG)], sem.at[2])

        def big(q):
            row = pl.multiple_of(boundary + q * bt, bt)
            return pltpu.make_async_copy(zeros, xs_hbm.at[pl.ds(row, bt)], sem.at[2])

        for piece, count in ((small, n_small), (big, n_big)):
            lax.fori_loop(0, count, lambda q, c, piece=piece: (piece(q).start(), c)[1], 0)
        for piece, count in ((small, n_small), (big, n_big)):
            lax.fori_loop(0, count, lambda q, c, piece=piece: (piece(q).wait(), c)[1], 0)


def _disp_call(n_slabs, used_rows, slab_local, slab_global, local, h2):
    n, d = h2.shape
    tm = TOKEN_TILE
    assert EXPERT_TILE & (EXPERT_TILE - 1) == 0 and SEG & (SEG - 1) == 0
    lists = pl.BlockSpec((1, 1, MAX_SLABS), lambda g, ns, used: (g, 0, 0), memory_space=pltpu.SMEM)
    grid_spec = pltpu.PrefetchScalarGridSpec(
        num_scalar_prefetch=2,
        grid=(n // tm,),
        in_specs=[lists, lists,
                  pl.BlockSpec((TOP_K, tm), lambda g, ns, used: (0, g)),
                  pl.BlockSpec((tm, d), lambda g, ns, used: (g, 0))],
        out_specs=pl.BlockSpec(memory_space=pl.ANY),
        scratch_shapes=[pltpu.VMEM((2, SORT_ROWS, d), F32), pltpu.VMEM((EXPERT_TILE, d), F32),
                        pltpu.SemaphoreType.DMA((3,))],
    )
    return pl.pallas_call(
        _disp_kernel,
        grid_spec=grid_spec,
        out_shape=jax.ShapeDtypeStruct((_sorted_rows(n), d), F32),
        compiler_params=pltpu.CompilerParams(dimension_semantics=("arbitrary",), vmem_limit_bytes=VMEM_LIMIT),
        name="disp",
    )(n_slabs, used_rows, slab_local, slab_global, local, h2)


def _moe_kernel(blk_ref, xblk_ref, exp_ref, lo_ref, hi_ref, first_ref, new_ref, nxt_ref, slot_ref,
                xs_ref, wup_hbm, bup_ref, wdn_hbm, bdn_ref, y_ref, wupf, wdnf, wupb, wdnp, wdnb, sem):
    t = EXPERT_TILE
    w = pl.program_id(0)
    lo = lo_ref[w]
    hi = hi_ref[w]
    half = wdnb.shape[0]
    d = wdnb.shape[1]

    def weight_copies(e, s):
        return (pltpu.make_async_copy(wup_hbm.at[e], wupf.at[s], sem.at[0, s]),
                pltpu.make_async_copy(wdn_hbm.at[e], wdnf.at[s], sem.at[1, s]))

    @pl.when(w == 0)
    def _():
        for c in weight_copies(exp_ref[0], 0):
            c.start()

    @pl.when(new_ref[w] == 1)
    def _():
        s = slot_ref[w]
        for c in weight_copies(exp_ref[w], s):
            c.wait()

        @pl.when(nxt_ref[w] >= 0)
        def _():
            for c in weight_copies(nxt_ref[w], 1 - s):
                c.start()

        wupb[...] = wupf[s].astype(BF16)
        for c in range(d // LANES):
            sl = slice(c * LANES, (c + 1) * LANES)
            wdnp[c, pl.ds(0, half // 2, stride=2), :] = wdnf[s, :half // 2, sl]
            wdnp[c, pl.ds(1, half // 2, stride=2), :] = wdnf[s, half // 2:, sl]
            wdnb[:, sl] = wdnp[c].astype(BF16)

    @pl.when(hi > lo)
    def _():
        x = xs_ref[...].astype(BF16)
        gu = _dot(x, wupb[...]) + bup_ref[...]
        lane = lax.broadcasted_iota(jnp.int32, (1, LANES), 1)
        even = (lane % 2) == 0
        acts = []
        for c in range(half // LANES):
            a = gu[:, c * LANES:(c + 1) * LANES]
            b = gu[:, half + c * LANES:half + (c + 1) * LANES]
            x_glu = jnp.minimum(jnp.where(even, a, pltpu.roll(b, 1, 1)), SWIGLU_LIMIT)
            x_lin = jnp.clip(jnp.where(even, pltpu.roll(a, LANES - 1, 1), b), -SWIGLU_LIMIT, SWIGLU_LIMIT)
            acts.append((x_glu * jax.nn.sigmoid(SWIGLU_ALPHA * x_glu) * (x_lin + 1.0)).astype(BF16))
        act = jnp.concatenate(acts, axis=1)
        y = _dot(act, wdnb[...]) + bdn_ref[...]
        row = lax.broadcasted_iota(jnp.int32, (t, 1), 0)
        mine = jnp.logical_and(row >= lo, row < hi)
        keep = jnp.where(first_ref[w] == 1, jnp.zeros_like(y), y_ref[...])
        y_ref[...] = jnp.where(mine, y, keep)

    @pl.when(jnp.logical_and(hi <= lo, first_ref[w] == 1))
    def _():
        y_ref[...] = jnp.zeros_like(y_ref)


def _moe_call(items, xs, wup, bup, wdn, bdn):
    t = EXPERT_TILE
    n_rows, d = xs.shape
    ff = wdn.shape[1]
    n_items = items[0].shape[0]
    by_blk = lambda w, blk, xblk, ex, *_: (blk[w], 0)
    by_xblk = lambda w, blk, xblk, ex, *_: (xblk[w], 0)
    by_exp = lambda w, blk, xblk, ex, *_: (ex[w], 0, 0)
    grid_spec = pltpu.PrefetchScalarGridSpec(
        num_scalar_prefetch=len(items),
        grid=(n_items,),
        in_specs=[pl.BlockSpec((t, d), by_xblk),
                  pl.BlockSpec(memory_space=pl.ANY),
                  pl.BlockSpec((None, 1, 2 * ff), by_exp),
                  pl.BlockSpec(memory_space=pl.ANY),
                  pl.BlockSpec((None, 1, d), by_exp)],
        out_specs=pl.BlockSpec((t, d), by_blk),
        scratch_shapes=[pltpu.VMEM((2, d, 2 * ff), F32),
                        pltpu.VMEM((2, ff, d), F32),
                        pltpu.VMEM((d, 2 * ff), BF16),
                        pltpu.VMEM((d // LANES, ff, LANES), F32),
                        pltpu.VMEM((ff, d), BF16),
                        pltpu.SemaphoreType.DMA((2, 2))],
    )
    return pl.pallas_call(
        _moe_kernel,
        grid_spec=grid_spec,
        out_shape=jax.ShapeDtypeStruct((n_rows, d), F32),
        compiler_params=pltpu.CompilerParams(dimension_semantics=("arbitrary",), vmem_limit_bytes=MOE_VMEM_LIMIT),
        name="moe",
    )(*items, xs, wup, bup, wdn, bdn)


def _moe_items(counts, n_rows):
    t = EXPERT_TILE
    nb = n_rows // t
    n_items = nb + N_EXPERTS - 1
    ends = jnp.cumsum(counts)
    starts = ends - counts
    first_blk = starts // t
    tiles = jnp.where(counts > 0, (ends - 1) // t - first_blk + 1, 0)
    item_end = jnp.cumsum(tiles)
    item_start = item_end - tiles
    total = item_end[-1]
    w = jnp.arange(n_items, dtype=jnp.int32)
    wc = jnp.minimum(w, total - 1)
    ex = jnp.sum((item_end[None, :] <= wc[:, None]).astype(jnp.int32), axis=1)
    onehot = (ex[:, None] == jnp.arange(N_EXPERTS, dtype=jnp.int32)[None, :]).astype(jnp.int32)
    pick = lambda v: jnp.sum(onehot * v[None, :], axis=1)
    xblk = pick(first_blk) + wc - pick(item_start)
    lo = jnp.clip(pick(starts) - xblk * t, 0, t)
    hi = jnp.where(w < total, jnp.clip(pick(ends) - xblk * t, 0, t), lo)
    used_blocks = (ends[-1] + t - 1) // t
    blk = jnp.where(w < total, xblk, jnp.minimum(used_blocks + w - total, nb - 1))
    first = jnp.concatenate([jnp.ones((1,), jnp.int32), (blk[1:] != blk[:-1]).astype(jnp.int32)])
    new = jnp.logical_and(w < total, w == pick(item_start)).astype(jnp.int32)
    e_ids = jnp.arange(N_EXPERTS, dtype=jnp.int32)
    later = jnp.logical_and(e_ids[None, :] > e_ids[:, None], counts[None, :] > 0)
    next_e = jnp.min(jnp.where(later, e_ids[None, :], N_EXPERTS), axis=1)
    next_e = jnp.where(next_e == N_EXPERTS, -1, next_e)
    ordinal = jnp.cumsum((counts > 0).astype(jnp.int32)) - 1
    as_i32 = lambda v: v.astype(jnp.int32)
    return tuple(map(as_i32, (blk, xblk, ex, lo, hi, first, new, pick(next_e), pick(ordinal) % 2)))


def _comb_kernel(ns_ref, sl_ref, sg_ref, nsl_ref, nsg_ref, local_ref, w_ref, y_hbm, x1_ref, o_ref, ysort, sem):
    tm = TOKEN_TILE
    g = pl.program_id(0)
    nt = pl.num_programs(0)
    slot = g % 2

    def slab(lref, gref, q, s):
        return pltpu.make_async_copy(y_hbm.at[pl.ds(pl.multiple_of(gref[0, 0, q], SEG), SEG)],
                                     ysort.at[s, pl.ds(pl.multiple_of(lref[0, 0, q], SEG), SEG)], sem.at[s])

    def fetch(lref, gref, count, s):
        _for_each(count, lambda q: slab(lref, gref, q, s).start())

    @pl.when(g == 0)
    def _():
        ysort[...] = jnp.zeros_like(ysort)
        fetch(sl_ref, sg_ref, ns_ref[0], 0)

    @pl.when(g + 1 < nt)
    def _():
        fetch(nsl_ref, nsg_ref, ns_ref[jnp.minimum(g + 1, nt - 1)], 1 - slot)

    _for_each(ns_ref[g], lambda q: pltpu.make_async_copy(
        y_hbm.at[pl.ds(0, SEG)], ysort.at[slot, pl.ds(0, SEG)], sem.at[slot]).wait())

    m_iota = lax.broadcasted_iota(jnp.int32, (tm, SORT_ROWS), 1)
    mix = jnp.zeros((tm, SORT_ROWS), F32)
    for k in range(TOP_K):
        mix = jnp.where(m_iota == local_ref[:, k:k + 1], w_ref[:, k:k + 1], mix)
    mix = mix.astype(BF16)
    o_ref[...] = x1_ref[...] + _dot(mix, ysort[slot].astype(BF16))


def _comb_call(n_slabs, slab_local, slab_global, local_t, w_t, y, x1):
    n, d = x1.shape
    tm = TOKEN_TILE
    nt = n // tm
    cur = lambda g, ns: (g, 0, 0)
    nxt = lambda g, ns: (jnp.minimum(g + 1, nt - 1), 0, 0)
    lists = lambda index_map: pl.BlockSpec((1, 1, MAX_SLABS), index_map, memory_space=pltpu.SMEM)
    grid_spec = pltpu.PrefetchScalarGridSpec(
        num_scalar_prefetch=1,
        grid=(nt,),
        in_specs=[lists(cur), lists(cur), lists(nxt), lists(nxt),
                  pl.BlockSpec((tm, TOP_K), lambda g, ns: (g, 0)),
                  pl.BlockSpec((tm, TOP_K), lambda g, ns: (g, 0)),
                  pl.BlockSpec(memory_space=pl.ANY),
                  pl.BlockSpec((tm, d), lambda g, ns: (g, 0))],
        out_specs=pl.BlockSpec((tm, d), lambda g, ns: (g, 0)),
        scratch_shapes=[pltpu.VMEM((2, SORT_ROWS, d), F32), pltpu.SemaphoreType.DMA((2,))],
    )
    return pl.pallas_call(
        _comb_kernel,
        grid_spec=grid_spec,
        out_shape=jax.ShapeDtypeStruct((n, d), F32),
        compiler_params=pltpu.CompilerParams(dimension_semantics=("arbitrary",), vmem_limit_bytes=VMEM_LIMIT),
        name="comb",
    )(n_slabs, slab_local, slab_global, slab_local, slab_global, local_t, w_t, y, x1)


def _layer(x2, pos2, bsz, seq, g_mix, w_in, g_cq, w_uq, g_ckv, w_ukv, g_qnorm, g_knorm, w_o_sb, w_o_mla, w_out,
           g_ffn, w_router, b_router, w_up, b_up, w_down, b_down):
    n, d = x2.shape
    c_sb = 3 * SB_WIDTH
    c_q = c_sb + MLA_Q_LORA
    c_kv = c_q + MLA_KV_LORA
    c_kr = c_kv + MLA_ROPE
    w1 = w_in[:, :2 * SB_WIDTH].astype(BF16)
    w1v = w_in[:, 2 * SB_WIDTH:c_sb].T.astype(BF16)
    w_kr = jnp.pad(w_in[:, c_kv:c_kr], ((0, 0), (MLA_NOPE, HEAD_PAD - MLA_QK)))
    w2 = jnp.concatenate([w_in[:, c_sb:c_kv], w_kr], axis=1).astype(BF16)
    w3 = w_in[:, c_kr:].astype(BF16)
    wuq = jnp.pad(w_uq.reshape(MLA_Q_LORA, MLA_HEADS, MLA_QK), ((0, 0), (0, 0), (0, HEAD_PAD - MLA_QK)))
    wuq = wuq.reshape(MLA_Q_LORA, MLA_HEADS * HEAD_PAD).astype(BF16)
    wukv = w_ukv.reshape(MLA_KV_LORA, MLA_HEADS, MLA_NOPE + MLA_V)
    wuk = jnp.pad(wukv[:, :, :MLA_NOPE], ((0, 0), (0, 0), (0, HEAD_PAD - MLA_NOPE)))
    wuk = wuk.reshape(MLA_KV_LORA, MLA_HEADS * HEAD_PAD).astype(BF16)
    wuv = wukv[:, :, MLA_NOPE:].reshape(MLA_KV_LORA, MLA_HEADS * MLA_V).T.astype(BF16)
    gq = jnp.tile(jnp.pad(g_qnorm, (0, HEAD_PAD - MLA_QK)), MLA_HEADS)[None, :]
    gk = jnp.tile(jnp.pad(g_knorm, (0, HEAD_PAD - MLA_QK)), MLA_HEADS)[None, :]
    half = MLA_ROPE // 2
    inv_freq = ROPE_THETA ** (-jnp.arange(half, dtype=F32) / half)
    invf = jnp.pad(jnp.concatenate([inv_freq, inv_freq]), (MLA_NOPE, HEAD_PAD - MLA_QK))[None, :]
    sgn = jnp.pad(jnp.concatenate([-jnp.ones((half,), F32), jnp.ones((half,), F32)]),
                  (MLA_NOPE, HEAD_PAD - MLA_QK))[None, :]

    sb, sbv, qm, km, vm, gates = _pre_call(x2, pos2, bsz, seq, g_mix[None, :], w1, w1v, w2, w3, g_cq[None, :],
                                           g_ckv[None, :], wuq, wuk, wuv, gq, gk, invf, sgn)
    o_sb = _sb_call(sb, sbv, bsz, seq)
    o_mla = _mla_call(qm, km, vm, bsz, seq)

    wr_t = w_router.T
    wr_hi = wr_t.astype(BF16)
    wr_lo = (wr_t - wr_hi.astype(F32)).astype(BF16)
    x1, h2, top_w, top_e, pos = _post_call(
        o_sb, o_mla, gates, x2, w_o_sb.astype(BF16), w_o_mla.astype(BF16), w_out.astype(BF16), g_ffn[None, :],
        wr_hi, wr_lo, b_router[:, None])

    padded, local, n_slabs, slab_local, slab_global = _route(top_e, pos, n)
    xs = _disp_call(n_slabs, jnp.sum(padded)[None], slab_local, slab_global, local, h2)
    y = _moe_call(_moe_items(padded, xs.shape[0]), xs, w_up, b_up[:, None, :], w_down, b_down[:, None, :])
    return _comb_call(n_slabs, slab_local, slab_global, local.T, top_w.T, y, x1)


def kernel(x, positions, g_mix, w_in, g_cq, w_uq, g_ckv, w_ukv, g_qnorm, g_knorm, w_o_sb, w_o_mla, w_out, g_ffn,
           w_router, b_router, w_up, b_up, w_down, b_down):
    bsz, seq, d = x.shape
    x2 = x.reshape(bsz * seq, d)
    pos2 = positions.reshape(bsz * seq, 1)
    for l in range(g_mix.shape[0]):
        x2 = _layer(x2, pos2, bsz, seq, g_mix[l], w_in[l], g_cq[l], w_uq[l], g_ckv[l], w_ukv[l], g_qnorm[l],
                    g_knorm[l], w_o_sb[l], w_o_mla[l], w_out[l], g_ffn[l], w_router[l], b_router[l], w_up[l],
                    b_up[l], w_down[l], b_down[l])
    return x2.reshape(bsz, seq, d)
```

```python
import math

import jax
import jax.numpy as jnp
from jax import lax
from jax.experimental import pallas as pl
from jax.experimental.pallas import tpu as pltpu

F32 = jnp.float32
BF16 = jnp.bfloat16

EPS = 1e-6
CHUNK = 64
SB_HEADS = 8
SB_DIM = 64
SB_WIDTH = SB_HEADS * SB_DIM
MLA_HEADS = 8
MLA_NOPE = 64
MLA_ROPE = 32
MLA_QK = MLA_NOPE + MLA_ROPE
MLA_V = 64
MLA_Q_LORA = 384
MLA_KV_LORA = 256
ROPE_THETA = 10000.0
N_EXPERTS = 32
TOP_K = 4
SWIGLU_LIMIT = 7.0
SWIGLU_ALPHA = 1.702

LANES = 128
HEAD_PAD = LANES
TOKEN_TILE = 256
PRE_TILE = 512
POST_TILE = 512
ATTN_TILE = 256
EXPERT_TILE = 256
SB_SKEW = 1
LOG2E = math.log2(math.e)
SB_DEAD = -160.0
VMEM_LIMIT = 48 * 1024 * 1024
MOE_VMEM_LIMIT = 56 * 1024 * 1024


def _nt_dot(a, b):
    return lax.dot_general(a, b, (((1,), (1,)), ((), ())), preferred_element_type=F32)


def _dot(a, b):
    return jnp.dot(a, b, preferred_element_type=F32)


def _rms(t, g):
    return t * lax.rsqrt(jnp.mean(t * t, axis=-1, keepdims=True) + EPS) * g


def _pre_kernel(x_ref, pos_ref, gmix_ref, w1_ref, w1v_ref, w2_ref, w3_ref, gcq_ref, gckv_ref, wuq_ref, wuk_ref,
                wuv_ref, gq_ref, gk_ref, invf_ref, sgn_ref,
                sb_ref, sbv_ref, qm_ref, km_ref, vm_ref, gate_ref):
    h = _rms(x_ref[...], gmix_ref[...]).astype(BF16)

    lat = _dot(h, w2_ref[...])
    cqn = _rms(lat[:, :MLA_Q_LORA], gcq_ref[...]).astype(BF16)
    ckvn = _rms(lat[:, MLA_Q_LORA:MLA_Q_LORA + MLA_KV_LORA], gckv_ref[...]).astype(BF16)
    kr = lat[:, MLA_Q_LORA + MLA_KV_LORA:]
    qf = _dot(cqn, wuq_ref[...])
    kf = _dot(ckvn, wuk_ref[...])
    key_tiles = [slice(j * ATTN_TILE, (j + 1) * ATTN_TILE) for j in range(PRE_TILE // ATTN_TILE)]
    for j, rows in enumerate(key_tiles):
        vm_ref[j] = _nt_dot(wuv_ref[...], ckvn[rows, :]).astype(BF16)

    sb_ref[:, :SB_WIDTH] = (_dot(h, w1_ref[:, :SB_WIDTH]) * (LOG2E / math.sqrt(SB_DIM))).astype(BF16)
    sb_ref[:, SB_WIDTH:] = _dot(h, w1_ref[:, SB_WIDTH:]).astype(BF16)
    for j, rows in enumerate(key_tiles):
        sbv_ref[j] = _nt_dot(w1v_ref[...], h[rows, :]).astype(BF16)

    gate_ref[...] = jax.nn.sigmoid(_dot(h, w3_ref[...])).astype(BF16)

    ang = pos_ref[...].astype(F32) * invf_ref[...]
    cos = jnp.cos(ang)
    sin = jnp.sin(ang) * sgn_ref[...]
    lane = lax.broadcasted_iota(jnp.int32, (1, LANES), 1)
    first_half = lane < MLA_NOPE + MLA_ROPE // 2

    def rope(t):
        partner = jnp.where(first_half, pltpu.roll(t, LANES - MLA_ROPE // 2, 1), pltpu.roll(t, MLA_ROPE // 2, 1))
        return t * cos + partner * sin

    def head_norm(t, g):
        return t * lax.rsqrt(jnp.sum(t * t, axis=-1, keepdims=True) * (1.0 / MLA_QK) + EPS) * g

    mla_scale = LOG2E / math.sqrt(MLA_QK)
    for hd in range(MLA_HEADS):
        sl = slice(hd * HEAD_PAD, (hd + 1) * HEAD_PAD)
        qm_ref[:, sl] = (rope(head_norm(qf[:, sl], gq_ref[:, sl])) * mla_scale).astype(BF16)
        km_ref[:, sl] = rope(head_norm(kf[:, sl] + kr, gk_ref[:, sl])).astype(BF16)


def _pre_call(x2, pos2, bsz, seq, gmix, w1, w1v, w2, w3, gcq, gckv, wuq, wuk, wuv, gq, gk, invf, sgn):
    n, d = x2.shape
    tm = PRE_TILE
    ta = ATTN_TILE
    assert tm % ta == 0 and seq % tm == 0
    nt = seq // ta
    steps = seq // tm
    const = lambda i: (0, 0)
    full = lambda a: pl.BlockSpec(a.shape, const)
    row = lambda w: pl.BlockSpec((tm, w), lambda i: (i, 0))
    vt_spec = lambda w: pl.BlockSpec((None, tm // ta, w, ta), lambda i: (i // steps, i % steps, 0, 0))
    return pl.pallas_call(
        _pre_kernel,
        grid=(n // tm,),
        in_specs=[row(d), row(1), full(gmix), full(w1), full(w1v), full(w2), full(w3), full(gcq), full(gckv),
                  full(wuq), full(wuk), full(wuv), full(gq), full(gk), full(invf), full(sgn)],
        out_specs=[row(2 * SB_WIDTH), vt_spec(SB_WIDTH), row(MLA_HEADS * HEAD_PAD), row(MLA_HEADS * HEAD_PAD),
                   vt_spec(MLA_HEADS * MLA_V), row(2 * d)],
        out_shape=[jax.ShapeDtypeStruct((n, 2 * SB_WIDTH), BF16),
                   jax.ShapeDtypeStruct((bsz, nt, SB_WIDTH, ta), BF16),
                   jax.ShapeDtypeStruct((n, MLA_HEADS * HEAD_PAD), BF16),
                   jax.ShapeDtypeStruct((n, MLA_HEADS * HEAD_PAD), BF16),
                   jax.ShapeDtypeStruct((bsz, nt, MLA_HEADS * MLA_V, ta), BF16),
                   jax.ShapeDtypeStruct((n, 2 * d), BF16)],
        compiler_params=pltpu.CompilerParams(dimension_semantics=("arbitrary",), vmem_limit_bytes=VMEM_LIMIT),
        name="pre",
    )(x2, pos2, gmix, w1, w1v, w2, w3, gcq, gckv, wuq, wuk, wuv, gq, gk, invf, sgn)


def _sb_kernel(q_ref, k_ref, v_ref, o_ref, acc_ref, car_ref, za_ref, zb_ref):
    t = ATTN_TILE
    i = pl.program_id(1)
    lane = lax.broadcasted_iota(jnp.int32, (1, LANES), 1)
    lo = lane < SB_DIM
    rr = lax.broadcasted_iota(jnp.int32, (t, t), 0)
    cc = lax.broadcasted_iota(jnp.int32, (t, t), 1)
    later_sum = (cc > rr).astype(BF16)
    strict = rr < cc

    acc_ref[...] = jnp.zeros_like(acc_ref)
    car_ref[...] = jnp.zeros_like(car_ref)

    def produce(dst, j, hd):
        krows = pl.ds(pl.multiple_of(jnp.maximum(j, 0) * t, t), t)
        pair = slice((hd // 2) * LANES, (hd // 2 + 1) * LANES)
        q2 = q_ref[:, pair]
        zero = jnp.zeros_like(q2)
        qh = jnp.where(lo, q2, zero) if hd % 2 == 0 else jnp.where(lo, zero, q2)
        dst[hd] = _nt_dot(k_ref[krows, pair], qh)

    def step(src, dst, j, masked):
        def stay(z):
            neg_abs = pltpu.bitcast(pltpu.bitcast(z, jnp.uint32) | jnp.uint32(0x80000000), F32)
            sp = jnp.log2(1.0 + jnp.exp2(neg_abs))
            log_beta = jnp.minimum(z, 0.0) - sp
            log_stay = log_beta - z
            if masked:
                log_stay = jnp.where(strict, log_stay, 0.0)
            later = _dot(later_sum, log_stay.astype(BF16))
            return log_beta, later, jnp.sum(log_stay, axis=0, keepdims=True)

        def weigh(hd, log_beta, later, total):
            carry = car_ref[hd]
            a = jnp.exp2(log_beta + later + carry)
            if masked:
                a = jnp.where(strict, a, 0.0)
            rows = slice(hd * SB_DIM, (hd + 1) * SB_DIM)
            acc_ref[rows, :] += _dot(v_ref[j, rows, :], a.astype(BF16))
            car_ref[hd] = carry + total

        mids = {}
        for s in range(SB_HEADS + SB_SKEW):
            if s < SB_HEADS:
                mids[s] = stay(src[s])
                produce(dst, j - 1, s)
            hd = s - SB_SKEW
            if 0 <= hd < SB_HEADS:
                weigh(hd, *mids.pop(hd))

    for hd in range(SB_HEADS):
        produce(za_ref, i, hd)
    step(za_ref, zb_ref, i, True)

    def cond(c):
        j, alive = c
        return jnp.logical_and(j >= 0, alive > SB_DEAD)

    def body(c):
        j, _ = c
        step(zb_ref, za_ref, j, False)

        @pl.when(jnp.logical_and(j >= 1, jnp.max(car_ref[...]) > SB_DEAD))
        def _():
            step(za_ref, zb_ref, j - 1, False)

        return j - 2, jnp.max(car_ref[...])

    lax.while_loop(cond, body, (i - 1, jnp.max(car_ref[...])))
    o_ref[...] = acc_ref[...].astype(o_ref.dtype)


def _sb_call(sb, sbv, bsz, seq):
    t = ATTN_TILE
    nq = seq // t
    return pl.pallas_call(
        _sb_kernel,
        grid=(bsz, nq),
        in_specs=[pl.BlockSpec((t, SB_WIDTH), lambda b, i: (b * nq + i, 0)),
                  pl.BlockSpec((seq, SB_WIDTH), lambda b, i: (b, 1)),
                  pl.BlockSpec((None, nq, SB_WIDTH, t), lambda b, i: (b, 0, 0, 0))],
        out_specs=pl.BlockSpec((None, SB_WIDTH, t), lambda b, i: (b, 0, i)),
        out_shape=jax.ShapeDtypeStruct((bsz, SB_WIDTH, seq), BF16),
        scratch_shapes=[pltpu.VMEM((SB_WIDTH, t), F32), pltpu.VMEM((SB_HEADS, 1, t), F32),
                        pltpu.VMEM((SB_HEADS, t, t), F32), pltpu.VMEM((SB_HEADS, t, t), F32)],
        compiler_params=pltpu.CompilerParams(dimension_semantics=("arbitrary",) * 2, vmem_limit_bytes=VMEM_LIMIT),
        name="sb_attn",
    )(sb, sb, sbv)


def _mla_kernel(q_ref, k_ref, v_ref, o_ref, m_ref, l_ref, acc_ref, sa_ref, sb_ref):
    t = ATTN_TILE
    i = pl.program_id(1)
    rr = lax.broadcasted_iota(jnp.int32, (t, t), 0)
    cc = lax.broadcasted_iota(jnp.int32, (t, t), 1)
    visible = (rr // CHUNK) <= (cc // CHUNK)

    m_ref[...] = jnp.full_like(m_ref, -jnp.inf)
    l_ref[...] = jnp.zeros_like(l_ref)
    acc_ref[...] = jnp.zeros_like(acc_ref)

    def produce(dst, j, heads=range(MLA_HEADS)):
        krows = pl.ds(pl.multiple_of(j * t, t), t)
        for hd in heads:
            sl = slice(hd * HEAD_PAD, (hd + 1) * HEAD_PAD)
            dst[hd] = _nt_dot(k_ref[krows, sl], q_ref[:, sl])

    def consume(src, j, masked, heads=range(MLA_HEADS)):
        for hd in heads:
            s = src[hd]
            if masked:
                s = jnp.where(visible, s, -jnp.inf)
            m_old = m_ref[hd]
            m_new = jnp.maximum(m_old, jnp.max(s, axis=0, keepdims=True))
            alpha = jnp.exp2(m_old - m_new)
            p = jnp.exp2(s - m_new)
            l_ref[hd] = alpha * l_ref[hd] + jnp.sum(p, axis=0, keepdims=True)
            rows = slice(hd * MLA_V, (hd + 1) * MLA_V)
            acc_ref[rows, :] = alpha * acc_ref[rows, :] + _dot(v_ref[j, rows, :], p.astype(BF16))
            m_ref[hd] = m_new

    def step(src, dst, j):
        for hd in range(MLA_HEADS):
            consume(src, j, False, [hd])
            produce(dst, j + 1, [hd])

    produce(sa_ref, 0)

    def pair(jj, c):
        step(sa_ref, sb_ref, 2 * jj)
        step(sb_ref, sa_ref, 2 * jj + 1)
        return c

    lax.fori_loop(0, i // 2, pair, 0)
    odd = i % 2

    @pl.when(odd == 1)
    def _():
        step(sa_ref, sb_ref, i - 1)
        consume(sb_ref, i, True)

    @pl.when(odd == 0)
    def _():
        consume(sa_ref, i, True)

    for hd in range(MLA_HEADS):
        rows = slice(hd * MLA_V, (hd + 1) * MLA_V)
        o_ref[rows, :] = (acc_ref[rows, :] / l_ref[hd]).astype(o_ref.dtype)


def _mla_call(qm, km, vm, bsz, seq):
    t = ATTN_TILE
    nq = seq // t
    width = MLA_HEADS * HEAD_PAD
    return pl.pallas_call(
        _mla_kernel,
        grid=(bsz, nq),
        in_specs=[pl.BlockSpec((t, width), lambda b, i: (b * nq + i, 0)),
                  pl.BlockSpec((seq, width), lambda b, i: (b, 0)),
                  pl.BlockSpec((None, nq, MLA_HEADS * MLA_V, t), lambda b, i: (b, 0, 0, 0))],
        out_specs=pl.BlockSpec((None, MLA_HEADS * MLA_V, t), lambda b, i: (b, 0, i)),
        out_shape=jax.ShapeDtypeStruct((bsz, MLA_HEADS * MLA_V, seq), BF16),
        scratch_shapes=[pltpu.VMEM((MLA_HEADS, 1, t), F32), pltpu.VMEM((MLA_HEADS, 1, t), F32),
                        pltpu.VMEM((MLA_HEADS * MLA_V, t), F32),
                        pltpu.VMEM((MLA_HEADS, t, t), F32), pltpu.VMEM((MLA_HEADS, t, t), F32)],
        compiler_params=pltpu.CompilerParams(dimension_semantics=("arbitrary",) * 2, vmem_limit_bytes=VMEM_LIMIT),
        name="mla_attn",
    )(qm, km, vm)


def _post_kernel(osb_ref, omla_ref, gate_ref, x_ref, wosb_ref, womla_ref, wout_ref, gffn_ref, wrh_ref, wrl_ref,
                 br_ref, x1_ref, h2_ref, topw_ref, tope_ref, pos_ref, run_ref):
    tm = TOKEN_TILE
    d = x_ref.shape[1]

    @pl.when(pl.program_id(0) == 0)
    def _():
        run_ref[...] = jnp.zeros_like(run_ref)

    def project(cols):
        tn = (((0,), (0,)), ((), ()))
        a = lax.dot_general(osb_ref[:, cols], wosb_ref[...], tn, preferred_element_type=F32)
        b = lax.dot_general(omla_ref[:, cols], womla_ref[...], tn, preferred_element_type=F32)
        mixed = gate_ref[cols, :d].astype(F32) * a + gate_ref[cols, d:].astype(F32) * b
        x1 = x_ref[cols, :] + _dot(mixed.astype(BF16), wout_ref[...])
        x1_ref[cols, :] = x1
        return x1

    def route(cols, x1):
        h2 = _rms(x1, gffn_ref[...])
        h2_ref[cols, :] = h2

        h_hi = h2.astype(BF16)
        h_lo = (h2 - h_hi.astype(F32)).astype(BF16)
        logits = (_nt_dot(wrh_ref[...], h_hi) + _nt_dot(wrh_ref[...], h_lo) + _nt_dot(wrl_ref[...], h_hi)
                  + br_ref[...])

        eidx = lax.broadcasted_iota(jnp.int32, (N_EXPERTS, tm), 0)
        work = logits
        tops, sels = [], []
        for _ in range(TOP_K):
            mk = jnp.max(work, axis=0, keepdims=True)
            ik = jnp.min(jnp.where(work == mk, eidx, N_EXPERTS), axis=0, keepdims=True)
            sel = eidx == ik
            work = jnp.where(sel, -jnp.inf, work)
            tops.append(mk)
            sels.append(sel)
        exps = [jnp.exp(mk - tops[0]) for mk in tops]
        denom = exps[0] + exps[1] + exps[2] + exps[3]

        chosen = jnp.zeros((N_EXPERTS, tm), F32)
        for sel in sels:
            chosen = chosen + sel.astype(F32)
        rr = lax.broadcasted_iota(jnp.int32, (tm, tm), 0)
        cc = lax.broadcasted_iota(jnp.int32, (tm, tm), 1)
        before = (rr < cc).astype(BF16)
        rank = _dot(chosen.astype(BF16), before) + run_ref[...]
        for k in range(TOP_K):
            topw_ref[k:k + 1, cols] = exps[k] / denom
            tope_ref[k:k + 1, cols] = jnp.sum(jnp.where(sels[k], eidx, 0), axis=0, keepdims=True)
            pos_ref[k:k + 1, cols] = jnp.sum(jnp.where(sels[k], rank, 0.0), axis=0, keepdims=True).astype(jnp.int32)
        run_ref[...] = run_ref[...] + jnp.sum(chosen, axis=1, keepdims=True)

    subs = [slice(j * tm, (j + 1) * tm) for j in range(POST_TILE // tm)]
    x1s = {}
    for j in range(len(subs) + 1):
        if j < len(subs):
            x1s[j] = project(subs[j])
        if j >= 1:
            route(subs[j - 1], x1s.pop(j - 1))


def _post_call(osb, omla, gates, x2, wosb, womla, wout, gffn, wrh, wrl, br):
    n, d = x2.shape
    tm = POST_TILE
    nt = osb.shape[2] // tm
    const = lambda i: (0, 0)
    full = lambda a: pl.BlockSpec(a.shape, const)
    row = lambda w: pl.BlockSpec((tm, w), lambda i: (i, 0))
    col = lambda: pl.BlockSpec((TOP_K, tm), lambda i: (0, i))
    feat = lambda a: pl.BlockSpec((None, a.shape[1], tm), lambda i: (i // nt, 0, i % nt))
    return pl.pallas_call(
        _post_kernel,
        grid=(n // tm,),
        in_specs=[feat(osb), feat(omla), row(2 * d), row(d), full(wosb), full(womla), full(wout),
                  full(gffn), full(wrh), full(wrl), full(br)],
        out_specs=[row(d), row(d), col(), col(), col()],
        out_shape=[jax.ShapeDtypeStruct((n, d), F32), jax.ShapeDtypeStruct((n, d), F32),
                   jax.ShapeDtypeStruct((TOP_K, n), F32), jax.ShapeDtypeStruct((TOP_K, n), jnp.int32),
                   jax.ShapeDtypeStruct((TOP_K, n), jnp.int32)],
        scratch_shapes=[pltpu.VMEM((N_EXPERTS, 1), F32)],
        compiler_params=pltpu.CompilerParams(dimension_semantics=("arbitrary",), vmem_limit_bytes=VMEM_LIMIT),
        name="post",
    )(osb, omla, gates, x2, wosb, womla, wout, gffn, wrh, wrl, br)


SEG = 8
MAX_SLABS = TOKEN_TILE * TOP_K // SEG + N_EXPERTS
SORT_ROWS = MAX_SLABS * SEG


def _for_each(count, fn, unroll=4):
    shift = unroll.bit_length() - 1
    assert unroll == 1 << shift
    groups = lax.shift_right_logical(count, shift)

    def group(i, c):
        for u in range(unroll):
            fn(i * unroll + u)
        return c
    lax.fori_loop(0, groups, group, 0)

    def single(q, c):
        fn(q)
        return c
    lax.fori_loop(lax.shift_left(groups, shift), count, single, 0)


def _sorted_rows(n):
    nt = n // TOKEN_TILE
    return -(-(n * TOP_K + nt * N_EXPERTS * (SEG - 1)) // EXPERT_TILE) * EXPERT_TILE


def _route(top_e, pos, n):
    tm = TOKEN_TILE
    nt = n // tm
    e_ids = jnp.arange(N_EXPERTS, dtype=jnp.int32)
    onehot = top_e[:, :, None] == e_ids[None, None, :]
    c_tile = jnp.sum(onehot.reshape(TOP_K, nt, tm, N_EXPERTS).astype(jnp.int32), axis=(0, 2))
    slabs = (c_tile + SEG - 1) // SEG
    rank_base = jnp.cumsum(c_tile, axis=0) - c_tile
    row_base = (jnp.cumsum(slabs, axis=0) - slabs) * SEG
    padded = jnp.sum(slabs, axis=0) * SEG
    starts = jnp.cumsum(padded) - padded
    seg_global = starts[None, :] + row_base
    slab_end = jnp.cumsum(slabs, axis=1)
    seg_local = (slab_end - slabs) * SEG

    per_token = lambda tab: jnp.broadcast_to(tab[:, None, :], (nt, tm, N_EXPERTS)).reshape(n, N_EXPERTS)
    local = jnp.sum(jnp.where(onehot, per_token(seg_local - rank_base)[None], 0), axis=2) + pos

    q = jnp.arange(MAX_SLABS, dtype=jnp.int32)
    e_q = jnp.minimum(jnp.sum((slab_end[:, None, :] <= q[None, :, None]).astype(jnp.int32), axis=2), N_EXPERTS - 1)
    hit = e_q[:, :, None] == e_ids[None, None, :]
    pick = lambda tab: jnp.sum(jnp.where(hit, tab[:, None, :], 0), axis=2)
    within = (q[None, :] - pick(slab_end - slabs)) * SEG
    n_slabs = slab_end[:, -1]
    live = q[None, :] < n_slabs[:, None]
    slab_local = jnp.where(live, pick(seg_local) + within, 0).astype(jnp.int32)
    slab_global = jnp.where(live, pick(seg_global) + within, 0).astype(jnp.int32)
    return (padded.astype(jnp.int32), local.astype(jnp.int32), n_slabs.astype(jnp.int32),
            slab_local.reshape(nt, 1, MAX_SLABS), slab_global.reshape(nt, 1, MAX_SLABS))


def _disp_kernel(ns_ref, used_ref, sl_ref, sg_ref, local_ref, h2_ref, xs_hbm, xsort, zeros, sem):
    tm = TOKEN_TILE
    g = pl.program_id(0)
    nt = pl.num_programs(0)
    slot = g % 2

    m_iota = lax.broadcasted_iota(jnp.int32, (SORT_ROWS, tm), 0)
    place = jnp.zeros((SORT_ROWS, tm), F32)
    for k in range(TOP_K):
        place = jnp.where(m_iota == local_ref[k:k + 1, :], 1.0, place)
    xsort[slot] = _dot(place.astype(BF16), h2_ref[...].astype(BF16))

    def slab(q, s):
        return pltpu.make_async_copy(xsort.at[s, pl.ds(pl.multiple_of(sl_ref[0, 0, q], SEG), SEG)],
                                     xs_hbm.at[pl.ds(pl.multiple_of(sg_ref[0, 0, q], SEG), SEG)], sem.at[s])

    _for_each(ns_ref[g], lambda q: slab(q, slot).start())

    def drain(count, s):
        _for_each(count, lambda q: pltpu.make_async_copy(
            xsort.at[s, pl.ds(0, SEG)], xs_hbm.at[pl.ds(0, SEG)], sem.at[s]).wait())

    @pl.when(g > 0)
    def _():
        drain(ns_ref[g - 1], 1 - slot)

    @pl.when(g == nt - 1)
    def _():
        drain(ns_ref[g], slot)
        bt = EXPERT_TILE
        zeros[...] = jnp.zeros_like(zeros)
        used = used_ref[0]
        boundary = lax.shift_left(lax.shift_right_logical(used + (bt - 1), bt.bit_length() - 1), bt.bit_length() - 1)
        n_small = lax.shift_right_logical(boundary - used, SEG.bit_length() - 1)
        n_big = lax.shift_right_logical(xs_hbm.shape[0] - boundary, bt.bit_length() - 1)

        def small(q):
            row = pl.multiple_of(used + q * SEG, SEG)
            return pltpu.make_async_copy(zeros.at[pl.ds(0, SEG)], xs_hbm.at[pl.ds(row, SEG)], sem.at[2])

        def big(q):
            row = pl.multiple_of(boundary + q * bt, bt)
            return pltpu.make_async_copy(zeros, xs_hbm.at[pl.ds(row, bt)], sem.at[2])

        for piece, count in ((small, n_small), (big, n_big)):
            lax.fori_loop(0, count, lambda q, c, piece=piece: (piece(q).start(), c)[1], 0)
        for piece, count in ((small, n_small), (big, n_big)):
            lax.fori_loop(0, count, lambda q, c, piece=piece: (piece(q).wait(), c)[1], 0)


def _disp_call(n_slabs, used_rows, slab_local, slab_global, local, h2):
    n, d = h2.shape
    tm = TOKEN_TILE
    assert EXPERT_TILE & (EXPERT_TILE - 1) == 0 and SEG & (SEG - 1) == 0
    lists = pl.BlockSpec((1, 1, MAX_SLABS), lambda g, ns, used: (g, 0, 0), memory_space=pltpu.SMEM)
    grid_spec = pltpu.PrefetchScalarGridSpec(
        num_scalar_prefetch=2,
        grid=(n // tm,),
        in_specs=[lists, lists,
                  pl.BlockSpec((TOP_K, tm), lambda g, ns, used: (0, g)),
                  pl.BlockSpec((tm, d), lambda g, ns, used: (g, 0))],
        out_specs=pl.BlockSpec(memory_space=pl.ANY),
        scratch_shapes=[pltpu.VMEM((2, SORT_ROWS, d), F32), pltpu.VMEM((EXPERT_TILE, d), F32),
                        pltpu.SemaphoreType.DMA((3,))],
    )
    return pl.pallas_call(
        _disp_kernel,
        grid_spec=grid_spec,
        out_shape=jax.ShapeDtypeStruct((_sorted_rows(n), d), F32),
        compiler_params=pltpu.CompilerParams(dimension_semantics=("arbitrary",), vmem_limit_bytes=VMEM_LIMIT),
        name="disp",
    )(n_slabs, used_rows, slab_local, slab_global, local, h2)


def _moe_kernel(blk_ref, xblk_ref, exp_ref, lo_ref, hi_ref, first_ref, new_ref, nxt_ref, slot_ref,
                xs_ref, wup_hbm, bup_ref, wdn_hbm, bdn_ref, y_ref, wupf, wdnf, wupb, wdnp, wdnb, sem):
    t = EXPERT_TILE
    w = pl.program_id(0)
    lo = lo_ref[w]
    hi = hi_ref[w]
    half = wdnb.shape[0]
    d = wdnb.shape[1]

    def weight_copies(e, s):
        return (pltpu.make_async_copy(wup_hbm.at[e], wupf.at[s], sem.at[0, s]),
                pltpu.make_async_copy(wdn_hbm.at[e], wdnf.at[s], sem.at[1, s]))

    @pl.when(w == 0)
    def _():
        for c in weight_copies(exp_ref[0], 0):
            c.start(priority=1)

    @pl.when(new_ref[w] == 1)
    def _():
        s = slot_ref[w]
        for c in weight_copies(exp_ref[w], s):
            c.wait()

        @pl.when(nxt_ref[w] >= 0)
        def _():
            for c in weight_copies(nxt_ref[w], 1 - s):
                c.start(priority=1)

        wupb[...] = wupf[s].astype(BF16)
        for c in range(d // LANES):
            sl = slice(c * LANES, (c + 1) * LANES)
            wdnp[c, pl.ds(0, half // 2, stride=2), :] = wdnf[s, :half // 2, sl]
            wdnp[c, pl.ds(1, half // 2, stride=2), :] = wdnf[s, half // 2:, sl]
            wdnb[:, sl] = wdnp[c].astype(BF16)

    @pl.when(hi > lo)
    def _():
        x = xs_ref[...].astype(BF16)
        gu = _dot(x, wupb[...]) + bup_ref[...]
        lane = lax.broadcasted_iota(jnp.int32, (1, LANES), 1)
        even = (lane % 2) == 0
        acts = []
        for c in range(half // LANES):
            a = gu[:, c * LANES:(c + 1) * LANES]
            b = gu[:, half + c * LANES:half + (c + 1) * LANES]
            x_glu = jnp.minimum(jnp.where(even, a, pltpu.roll(b, 1, 1)), SWIGLU_LIMIT)
            x_lin = jnp.clip(jnp.where(even, pltpu.roll(a, LANES - 1, 1), b), -SWIGLU_LIMIT, SWIGLU_LIMIT)
            acts.append((x_glu * jax.nn.sigmoid(SWIGLU_ALPHA * x_glu) * (x_lin + 1.0)).astype(BF16))
        act = jnp.concatenate(acts, axis=1)
        y = _dot(act, wdnb[...]) + bdn_ref[...]
        row = lax.broadcasted_iota(jnp.int32, (t, 1), 0)
        mine = jnp.logical_and(row >= lo, row < hi)
        keep = jnp.where(first_ref[w] == 1, jnp.zeros_like(y), y_ref[...])
        y_ref[...] = jnp.where(mine, y, keep)

    @pl.when(jnp.logical_and(hi <= lo, first_ref[w] == 1))
    def _():
        y_ref[...] = jnp.zeros_like(y_ref)


def _moe_call(items, xs, wup, bup, wdn, bdn):
    t = EXPERT_TILE
    n_rows, d = xs.shape
    ff = wdn.shape[1]
    n_items = items[0].shape[0]
    by_blk = lambda w, blk, xblk, ex, *_: (blk[w], 0)
    by_xblk = lambda w, blk, xblk, ex, *_: (xblk[w], 0)
    by_exp = lambda w, blk, xblk, ex, *_: (ex[w], 0, 0)
    grid_spec = pltpu.PrefetchScalarGridSpec(
        num_scalar_prefetch=len(items),
        grid=(n_items,),
        in_specs=[pl.BlockSpec((t, d), by_xblk),
                  pl.BlockSpec(memory_space=pl.ANY),
                  pl.BlockSpec((None, 1, 2 * ff), by_exp),
                  pl.BlockSpec(memory_space=pl.ANY),
                  pl.BlockSpec((None, 1, d), by_exp)],
        out_specs=pl.BlockSpec((t, d), by_blk),
        scratch_shapes=[pltpu.VMEM((2, d, 2 * ff), F32),
                        pltpu.VMEM((2, ff, d), F32),
                        pltpu.VMEM((d, 2 * ff), BF16),
                        pltpu.VMEM((d // LANES, ff, LANES), F32),
                        pltpu.VMEM((ff, d), BF16),
                        pltpu.SemaphoreType.DMA((2, 2))],
    )
    return pl.pallas_call(
        _moe_kernel,
        grid_spec=grid_spec,
        out_shape=jax.ShapeDtypeStruct((n_rows, d), F32),
        compiler_params=pltpu.CompilerParams(dimension_semantics=("arbitrary",), vmem_limit_bytes=MOE_VMEM_LIMIT),
        name="moe",
    )(*items, xs, wup, bup, wdn, bdn)


def _moe_items(counts, n_rows):
    t = EXPERT_TILE
    nb = n_rows // t
    n_items = nb + N_EXPERTS - 1
    ends = jnp.cumsum(counts)
    starts = ends - counts
    first_blk = starts // t
    tiles = jnp.where(counts > 0, (ends - 1) // t - first_blk + 1, 0)
    item_end = jnp.cumsum(tiles)
    item_start = item_end - tiles
    total = item_end[-1]
    w = jnp.arange(n_items, dtype=jnp.int32)
    wc = jnp.minimum(w, total - 1)
    ex = jnp.sum((item_end[None, :] <= wc[:, None]).astype(jnp.int32), axis=1)
    onehot = (ex[:, None] == jnp.arange(N_EXPERTS, dtype=jnp.int32)[None, :]).astype(jnp.int32)
    pick = lambda v: jnp.sum(onehot * v[None, :], axis=1)
    xblk = pick(first_blk) + wc - pick(item_start)
    lo = jnp.clip(pick(starts) - xblk * t, 0, t)
    hi = jnp.where(w < total, jnp.clip(pick(ends) - xblk * t, 0, t), lo)
    used_blocks = (ends[-1] + t - 1) // t
    blk = jnp.where(w < total, xblk, jnp.minimum(used_blocks + w - total, nb - 1))
    first = jnp.concatenate([jnp.ones((1,), jnp.int32), (blk[1:] != blk[:-1]).astype(jnp.int32)])
    new = jnp.logical_and(w < total, w == pick(item_start)).astype(jnp.int32)
    e_ids = jnp.arange(N_EXPERTS, dtype=jnp.int32)
    later = jnp.logical_and(e_ids[None, :] > e_ids[:, None], counts[None, :] > 0)
    next_e = jnp.min(jnp.where(later, e_ids[None, :], N_EXPERTS), axis=1)
    next_e = jnp.where(next_e == N_EXPERTS, -1, next_e)
    ordinal = jnp.cumsum((counts > 0).astype(jnp.int32)) - 1
    as_i32 = lambda v: v.astype(jnp.int32)
    return tuple(map(as_i32, (blk, xblk, ex, lo, hi, first, new, pick(next_e), pick(ordinal) % 2)))


def _comb_kernel(ns_ref, sl_ref, sg_ref, nsl_ref, nsg_ref, local_ref, w_ref, y_hbm, x1_ref, o_ref, ysort, sem):
    tm = TOKEN_TILE
    g = pl.program_id(0)
    nt = pl.num_programs(0)
    slot = g % 2

    def slab(lref, gref, q, s):
        return pltpu.make_async_copy(y_hbm.at[pl.ds(pl.multiple_of(gref[0, 0, q], SEG), SEG)],
                                     ysort.at[s, pl.ds(pl.multiple_of(lref[0, 0, q], SEG), SEG)], sem.at[s])

    def fetch(lref, gref, count, s):
        _for_each(count, lambda q: slab(lref, gref, q, s).start())

    @pl.when(g == 0)
    def _():
        ysort[...] = jnp.zeros_like(ysort)
        fetch(sl_ref, sg_ref, ns_ref[0], 0)

    @pl.when(g + 1 < nt)
    def _():
        fetch(nsl_ref, nsg_ref, ns_ref[jnp.minimum(g + 1, nt - 1)], 1 - slot)

    _for_each(ns_ref[g], lambda q: pltpu.make_async_copy(
        y_hbm.at[pl.ds(0, SEG)], ysort.at[slot, pl.ds(0, SEG)], sem.at[slot]).wait())

    m_iota = lax.broadcasted_iota(jnp.int32, (tm, SORT_ROWS), 1)
    mix = jnp.zeros((tm, SORT_ROWS), F32)
    for k in range(TOP_K):
        mix = jnp.where(m_iota == local_ref[:, k:k + 1], w_ref[:, k:k + 1], mix)
    mix = mix.astype(BF16)
    o_ref[...] = x1_ref[...] + _dot(mix, ysort[slot].astype(BF16))


def _comb_call(n_slabs, slab_local, slab_global, local_t, w_t, y, x1):
    n, d = x1.shape
    tm = TOKEN_TILE
    nt = n // tm
    cur = lambda g, ns: (g, 0, 0)
    nxt = lambda g, ns: (jnp.minimum(g + 1, nt - 1), 0, 0)
    lists = lambda index_map: pl.BlockSpec((1, 1, MAX_SLABS), index_map, memory_space=pltpu.SMEM)
    grid_spec = pltpu.PrefetchScalarGridSpec(
        num_scalar_prefetch=1,
        grid=(nt,),
        in_specs=[lists(cur), lists(cur), lists(nxt), lists(nxt),
                  pl.BlockSpec((tm, TOP_K), lambda g, ns: (g, 0)),
                  pl.BlockSpec((tm, TOP_K), lambda g, ns: (g, 0)),
                  pl.BlockSpec(memory_space=pl.ANY),
                  pl.BlockSpec((tm, d), lambda g, ns: (g, 0))],
        out_specs=pl.BlockSpec((tm, d), lambda g, ns: (g, 0)),
        scratch_shapes=[pltpu.VMEM((2, SORT_ROWS, d), F32), pltpu.SemaphoreType.DMA((2,))],
    )
    return pl.pallas_call(
        _comb_kernel,
        grid_spec=grid_spec,
        out_shape=jax.ShapeDtypeStruct((n, d), F32),
        compiler_params=pltpu.CompilerParams(dimension_semantics=("arbitrary",), vmem_limit_bytes=VMEM_LIMIT),
        name="comb",
    )(n_slabs, slab_local, slab_global, slab_local, slab_global, local_t, w_t, y, x1)


def _layer(x2, pos2, bsz, seq, g_mix, w_in, g_cq, w_uq, g_ckv, w_ukv, g_qnorm, g_knorm, w_o_sb, w_o_mla, w_out,
           g_ffn, w_router, b_router, w_up, b_up, w_down, b_down):
    n, d = x2.shape
    c_sb = 3 * SB_WIDTH
    c_q = c_sb + MLA_Q_LORA
    c_kv = c_q + MLA_KV_LORA
    c_kr = c_kv + MLA_ROPE
    w1 = w_in[:, :2 * SB_WIDTH].astype(BF16)
    w1v = w_in[:, 2 * SB_WIDTH:c_sb].T.astype(BF16)
    w_kr = jnp.pad(w_in[:, c_kv:c_kr], ((0, 0), (MLA_NOPE, HEAD_PAD - MLA_QK)))
    w2 = jnp.concatenate([w_in[:, c_sb:c_kv], w_kr], axis=1).astype(BF16)
    w3 = w_in[:, c_kr:].astype(BF16)
    wuq = jnp.pad(w_uq.reshape(MLA_Q_LORA, MLA_HEADS, MLA_QK), ((0, 0), (0, 0), (0, HEAD_PAD - MLA_QK)))
    wuq = wuq.reshape(MLA_Q_LORA, MLA_HEADS * HEAD_PAD).astype(BF16)
    wukv = w_ukv.reshape(MLA_KV_LORA, MLA_HEADS, MLA_NOPE + MLA_V)
    wuk = jnp.pad(wukv[:, :, :MLA_NOPE], ((0, 0), (0, 0), (0, HEAD_PAD - MLA_NOPE)))
    wuk = wuk.reshape(MLA_KV_LORA, MLA_HEADS * HEAD_PAD).astype(BF16)
    wuv = wukv[:, :, MLA_NOPE:].reshape(MLA_KV_LORA, MLA_HEADS * MLA_V).T.astype(BF16)
    gq = jnp.tile(jnp.pad(g_qnorm, (0, HEAD_PAD - MLA_QK)), MLA_HEADS)[None, :]
    gk = jnp.tile(jnp.pad(g_knorm, (0, HEAD_PAD - MLA_QK)), MLA_HEADS)[None, :]
    half = MLA_ROPE // 2
    inv_freq = ROPE_THETA ** (-jnp.arange(half, dtype=F32) / half)
    invf = jnp.pad(jnp.concatenate([inv_freq, inv_freq]), (MLA_NOPE, HEAD_PAD - MLA_QK))[None, :]
    sgn = jnp.pad(jnp.concatenate([-jnp.ones((half,), F32), jnp.ones((half,), F32)]),
                  (MLA_NOPE, HEAD_PAD - MLA_QK))[None, :]

    sb, sbv, qm, km, vm, gates = _pre_call(x2, pos2, bsz, seq, g_mix[None, :], w1, w1v, w2, w3, g_cq[None, :],
                                           g_ckv[None, :], wuq, wuk, wuv, gq, gk, invf, sgn)
    o_sb = _sb_call(sb, sbv, bsz, seq)
    o_mla = _mla_call(qm, km, vm, bsz, seq)

    wr_t = w_router.T
    wr_hi = wr_t.astype(BF16)
    wr_lo = (wr_t - wr_hi.astype(F32)).astype(BF16)
    x1, h2, top_w, top_e, pos = _post_call(
        o_sb, o_mla, gates, x2, w_o_sb.astype(BF16), w_o_mla.astype(BF16), w_out.astype(BF16), g_ffn[None, :],
        wr_hi, wr_lo, b_router[:, None])

    padded, local, n_slabs, slab_local, slab_global = _route(top_e, pos, n)
    xs = _disp_call(n_slabs, jnp.sum(padded)[None], slab_local, slab_global, local, h2)
    y = _moe_call(_moe_items(padded, xs.shape[0]), xs, w_up, b_up[:, None, :], w_down, b_down[:, None, :])
    return _comb_call(n_slabs, slab_local, slab_global, local.T, top_w.T, y, x1)


def kernel(x, positions, g_mix, w_in, g_cq, w_uq, g_ckv, w_ukv, g_qnorm, g_knorm, w_o_sb, w_o_mla, w_out, g_ffn,
           w_router, b_router, w_up, b_up, w_down, b_down):
    bsz, seq, d = x.shape
    x2 = x.reshape(bsz * seq, d)
    pos2 = positions.reshape(bsz * seq, 1)
    for l in range(g_mix.shape[0]):
        x2 = _layer(x2, pos2, bsz, seq, g_mix[l], w_in[l], g_cq[l], w_uq[l], g_ckv[l], w_ukv[l], g_qnorm[l],
                    g_knorm[l], w_o_sb[l], w_o_mla[l], w_out[l], g_ffn[l], w_router[l], b_router[l], w_up[l],
                    b_up[l], w_down[l], b_down[l])
    return x2.reshape(bsz, seq, d)
```

```python
import math

import jax
import jax.numpy as jnp
from jax import lax
from jax.experimental import pallas as pl
from jax.experimental.pallas import tpu as pltpu

F32 = jnp.float32
BF16 = jnp.bfloat16

EPS = 1e-6
CHUNK = 64
SB_HEADS = 8
SB_DIM = 64
SB_WIDTH = SB_HEADS * SB_DIM
MLA_HEADS = 8
MLA_NOPE = 64
MLA_ROPE = 32
MLA_QK = MLA_NOPE + MLA_ROPE
MLA_V = 64
MLA_Q_LORA = 384
MLA_KV_LORA = 256
ROPE_THETA = 10000.0
N_EXPERTS = 32
TOP_K = 4
SWIGLU_LIMIT = 7.0
SWIGLU_ALPHA = 1.702

LANES = 128
HEAD_PAD = LANES
TOKEN_TILE = 256
PRE_TILE = 512
POST_TILE = 512
ATTN_TILE = 256
EXPERT_TILE = 512
SB_SKEW = 1
LOG2E = math.log2(math.e)
SB_DEAD = -160.0
VMEM_LIMIT = 48 * 1024 * 1024
MOE_VMEM_LIMIT = 56 * 1024 * 1024


def _nt_dot(a, b):
    return lax.dot_general(a, b, (((1,), (1,)), ((), ())), preferred_element_type=F32)


def _dot(a, b):
    return jnp.dot(a, b, preferred_element_type=F32)


def _rms(t, g):
    return t * lax.rsqrt(jnp.mean(t * t, axis=-1, keepdims=True) + EPS) * g


def _pre_kernel(x_ref, pos_ref, gmix_ref, w1_ref, w1v_ref, w2_ref, w3_ref, gcq_ref, gckv_ref, wuq_ref, wuk_ref,
                wuv_ref, gq_ref, gk_ref, invf_ref, sgn_ref,
                sb_ref, sbv_ref, qm_ref, km_ref, vm_ref, gate_ref):
    h = _rms(x_ref[...], gmix_ref[...]).astype(BF16)

    lat = _dot(h, w2_ref[...])
    cqn = _rms(lat[:, :MLA_Q_LORA], gcq_ref[...]).astype(BF16)
    ckvn = _rms(lat[:, MLA_Q_LORA:MLA_Q_LORA + MLA_KV_LORA], gckv_ref[...]).astype(BF16)
    kr = lat[:, MLA_Q_LORA + MLA_KV_LORA:]
    qf = _dot(cqn, wuq_ref[...])
    kf = _dot(ckvn, wuk_ref[...])
    key_tiles = [slice(j * ATTN_TILE, (j + 1) * ATTN_TILE) for j in range(PRE_TILE // ATTN_TILE)]
    for j, rows in enumerate(key_tiles):
        vm_ref[j] = _nt_dot(wuv_ref[...], ckvn[rows, :]).astype(BF16)

    sb_ref[:, :SB_WIDTH] = (_dot(h, w1_ref[:, :SB_WIDTH]) * (LOG2E / math.sqrt(SB_DIM))).astype(BF16)
    sb_ref[:, SB_WIDTH:] = _dot(h, w1_ref[:, SB_WIDTH:]).astype(BF16)
    for j, rows in enumerate(key_tiles):
        sbv_ref[j] = _nt_dot(w1v_ref[...], h[rows, :]).astype(BF16)

    gate_ref[...] = jax.nn.sigmoid(_dot(h, w3_ref[...])).astype(BF16)

    ang = pos_ref[...].astype(F32) * invf_ref[...]
    cos = jnp.cos(ang)
    sin = jnp.sin(ang) * sgn_ref[...]
    lane = lax.broadcasted_iota(jnp.int32, (1, LANES), 1)
    first_half = lane < MLA_NOPE + MLA_ROPE // 2

    def rope(t):
        partner = jnp.where(first_half, pltpu.roll(t, LANES - MLA_ROPE // 2, 1), pltpu.roll(t, MLA_ROPE // 2, 1))
        return t * cos + partner * sin

    def head_norm(t, g):
        return t * lax.rsqrt(jnp.sum(t * t, axis=-1, keepdims=True) * (1.0 / MLA_QK) + EPS) * g

    mla_scale = LOG2E / math.sqrt(MLA_QK)
    for hd in range(MLA_HEADS):
        sl = slice(hd * HEAD_PAD, (hd + 1) * HEAD_PAD)
        qm_ref[:, sl] = (rope(head_norm(qf[:, sl], gq_ref[:, sl])) * mla_scale).astype(BF16)
        km_ref[:, sl] = rope(head_norm(kf[:, sl] + kr, gk_ref[:, sl])).astype(BF16)


def _pre_call(x2, pos2, bsz, seq, gmix, w1, w1v, w2, w3, gcq, gckv, wuq, wuk, wuv, gq, gk, invf, sgn):
    n, d = x2.shape
    tm = PRE_TILE
    ta = ATTN_TILE
    assert tm % ta == 0 and seq % tm == 0
    nt = seq // ta
    steps = seq // tm
    const = lambda i: (0, 0)
    full = lambda a: pl.BlockSpec(a.shape, const)
    row = lambda w: pl.BlockSpec((tm, w), lambda i: (i, 0))
    vt_spec = lambda w: pl.BlockSpec((None, tm // ta, w, ta), lambda i: (i // steps, i % steps, 0, 0))
    return pl.pallas_call(
        _pre_kernel,
        grid=(n // tm,),
        in_specs=[row(d), row(1), full(gmix), full(w1), full(w1v), full(w2), full(w3), full(gcq), full(gckv),
                  full(wuq), full(wuk), full(wuv), full(gq), full(gk), full(invf), full(sgn)],
        out_specs=[row(2 * SB_WIDTH), vt_spec(SB_WIDTH), row(MLA_HEADS * HEAD_PAD), row(MLA_HEADS * HEAD_PAD),
                   vt_spec(MLA_HEADS * MLA_V), row(2 * d)],
        out_shape=[jax.ShapeDtypeStruct((n, 2 * SB_WIDTH), BF16),
                   jax.ShapeDtypeStruct((bsz, nt, SB_WIDTH, ta), BF16),
                   jax.ShapeDtypeStruct((n, MLA_HEADS * HEAD_PAD), BF16),
                   jax.ShapeDtypeStruct((n, MLA_HEADS * HEAD_PAD), BF16),
                   jax.ShapeDtypeStruct((bsz, nt, MLA_HEADS * MLA_V, ta), BF16),
                   jax.ShapeDtypeStruct((n, 2 * d), BF16)],
        compiler_params=pltpu.CompilerParams(dimension_semantics=("arbitrary",), vmem_limit_bytes=VMEM_LIMIT),
        name="pre",
    )(x2, pos2, gmix, w1, w1v, w2, w3, gcq, gckv, wuq, wuk, wuv, gq, gk, invf, sgn)


def _sb_kernel(q_ref, k_ref, v_ref, o_ref, acc_ref, car_ref, za_ref, zb_ref):
    t = ATTN_TILE
    i = pl.program_id(1)
    lane = lax.broadcasted_iota(jnp.int32, (1, LANES), 1)
    lo = lane < SB_DIM
    rr = lax.broadcasted_iota(jnp.int32, (t, t), 0)
    cc = lax.broadcasted_iota(jnp.int32, (t, t), 1)
    later_sum = (cc > rr).astype(BF16)
    strict = rr < cc

    acc_ref[...] = jnp.zeros_like(acc_ref)
    car_ref[...] = jnp.zeros_like(car_ref)

    def produce(dst, j, hd):
        krows = pl.ds(pl.multiple_of(jnp.maximum(j, 0) * t, t), t)
        pair = slice((hd // 2) * LANES, (hd // 2 + 1) * LANES)
        q2 = q_ref[:, pair]
        zero = jnp.zeros_like(q2)
        qh = jnp.where(lo, q2, zero) if hd % 2 == 0 else jnp.where(lo, zero, q2)
        dst[hd] = _nt_dot(k_ref[krows, pair], qh)

    def step(src, dst, j, masked):
        def stay(z):
            neg_abs = pltpu.bitcast(pltpu.bitcast(z, jnp.uint32) | jnp.uint32(0x80000000), F32)
            sp = jnp.log2(1.0 + jnp.exp2(neg_abs))
            log_beta = jnp.minimum(z, 0.0) - sp
            log_stay = log_beta - z
            if masked:
                log_stay = jnp.where(strict, log_stay, 0.0)
            later = _dot(later_sum, log_stay.astype(BF16))
            return log_beta, later, jnp.sum(log_stay, axis=0, keepdims=True)

        def weigh(hd, log_beta, later, total):
            carry = car_ref[hd]
            a = jnp.exp2(log_beta + later + carry)
            if masked:
                a = jnp.where(strict, a, 0.0)
            rows = slice(hd * SB_DIM, (hd + 1) * SB_DIM)
            acc_ref[rows, :] += _dot(v_ref[j, rows, :], a.astype(BF16))
            car_ref[hd] = carry + total

        mids = {}
        for s in range(SB_HEADS + SB_SKEW):
            if s < SB_HEADS:
                mids[s] = stay(src[s])
                produce(dst, j - 1, s)
            hd = s - SB_SKEW
            if 0 <= hd < SB_HEADS:
                weigh(hd, *mids.pop(hd))

    for hd in range(SB_HEADS):
        produce(za_ref, i, hd)
    step(za_ref, zb_ref, i, True)

    def cond(c):
        j, alive = c
        return jnp.logical_and(j >= 0, alive > SB_DEAD)

    def body(c):
        j, _ = c
        step(zb_ref, za_ref, j, False)

        @pl.when(jnp.logical_and(j >= 1, jnp.max(car_ref[...]) > SB_DEAD))
        def _():
            step(za_ref, zb_ref, j - 1, False)

        return j - 2, jnp.max(car_ref[...])

    lax.while_loop(cond, body, (i - 1, jnp.max(car_ref[...])))
    o_ref[...] = acc_ref[...].astype(o_ref.dtype)


def _sb_call(sb, sbv, bsz, seq):
    t = ATTN_TILE
    nq = seq // t
    return pl.pallas_call(
        _sb_kernel,
        grid=(bsz, nq),
        in_specs=[pl.BlockSpec((t, SB_WIDTH), lambda b, i: (b * nq + i, 0)),
                  pl.BlockSpec((seq, SB_WIDTH), lambda b, i: (b, 1)),
                  pl.BlockSpec((None, nq, SB_WIDTH, t), lambda b, i: (b, 0, 0, 0))],
        out_specs=pl.BlockSpec((None, SB_WIDTH, t), lambda b, i: (b, 0, i)),
        out_shape=jax.ShapeDtypeStruct((bsz, SB_WIDTH, seq), BF16),
        scratch_shapes=[pltpu.VMEM((SB_WIDTH, t), F32), pltpu.VMEM((SB_HEADS, 1, t), F32),
                        pltpu.VMEM((SB_HEADS, t, t), F32), pltpu.VMEM((SB_HEADS, t, t), F32)],
        compiler_params=pltpu.CompilerParams(dimension_semantics=("arbitrary",) * 2, vmem_limit_bytes=VMEM_LIMIT),
        name="sb_attn",
    )(sb, sb, sbv)


def _mla_kernel(q_ref, k_ref, v_ref, o_ref, m_ref, l_ref, acc_ref, sa_ref, sb_ref):
    t = ATTN_TILE
    i = pl.program_id(1)
    rr = lax.broadcasted_iota(jnp.int32, (t, t), 0)
    cc = lax.broadcasted_iota(jnp.int32, (t, t), 1)
    visible = (rr // CHUNK) <= (cc // CHUNK)

    m_ref[...] = jnp.full_like(m_ref, -jnp.inf)
    l_ref[...] = jnp.zeros_like(l_ref)
    acc_ref[...] = jnp.zeros_like(acc_ref)

    def produce(dst, j, heads=range(MLA_HEADS)):
        krows = pl.ds(pl.multiple_of(j * t, t), t)
        for hd in heads:
            sl = slice(hd * HEAD_PAD, (hd + 1) * HEAD_PAD)
            dst[hd] = _nt_dot(k_ref[krows, sl], q_ref[:, sl])

    def consume(src, j, masked, heads=range(MLA_HEADS)):
        for hd in heads:
            s = src[hd]
            if masked:
                s = jnp.where(visible, s, -jnp.inf)
            m_old = m_ref[hd]
            m_new = jnp.maximum(m_old, jnp.max(s, axis=0, keepdims=True))
            alpha = jnp.exp2(m_old - m_new)
            p = jnp.exp2(s - m_new)
            l_ref[hd] = alpha * l_ref[hd] + jnp.sum(p, axis=0, keepdims=True)
            rows = slice(hd * MLA_V, (hd + 1) * MLA_V)
            acc_ref[rows, :] = alpha * acc_ref[rows, :] + _dot(v_ref[j, rows, :], p.astype(BF16))
            m_ref[hd] = m_new

    def step(src, dst, j):
        for hd in range(MLA_HEADS):
            consume(src, j, False, [hd])
            produce(dst, j + 1, [hd])

    produce(sa_ref, 0)

    def pair(jj, c):
        step(sa_ref, sb_ref, 2 * jj)
        step(sb_ref, sa_ref, 2 * jj + 1)
        return c

    lax.fori_loop(0, i // 2, pair, 0)
    odd = i % 2

    @pl.when(odd == 1)
    def _():
        step(sa_ref, sb_ref, i - 1)
        consume(sb_ref, i, True)

    @pl.when(odd == 0)
    def _():
        consume(sa_ref, i, True)

    for hd in range(MLA_HEADS):
        rows = slice(hd * MLA_V, (hd + 1) * MLA_V)
        o_ref[rows, :] = (acc_ref[rows, :] / l_ref[hd]).astype(o_ref.dtype)


def _mla_call(qm, km, vm, bsz, seq):
    t = ATTN_TILE
    nq = seq // t
    width = MLA_HEADS * HEAD_PAD
    return pl.pallas_call(
        _mla_kernel,
        grid=(bsz, nq),
        in_specs=[pl.BlockSpec((t, width), lambda b, i: (b * nq + i, 0)),
                  pl.BlockSpec((seq, width), lambda b, i: (b, 0)),
                  pl.BlockSpec((None, nq, MLA_HEADS * MLA_V, t), lambda b, i: (b, 0, 0, 0))],
        out_specs=pl.BlockSpec((None, MLA_HEADS * MLA_V, t), lambda b, i: (b, 0, i)),
        out_shape=jax.ShapeDtypeStruct((bsz, MLA_HEADS * MLA_V, seq), BF16),
        scratch_shapes=[pltpu.VMEM((MLA_HEADS, 1, t), F32), pltpu.VMEM((MLA_HEADS, 1, t), F32),
                        pltpu.VMEM((MLA_HEADS * MLA_V, t), F32),
                        pltpu.VMEM((MLA_HEADS, t, t), F32), pltpu.VMEM((MLA_HEADS, t, t), F32)],
        compiler_params=pltpu.CompilerParams(dimension_semantics=("arbitrary",) * 2, vmem_limit_bytes=VMEM_LIMIT),
        name="mla_attn",
    )(qm, km, vm)


def _post_kernel(osb_ref, omla_ref, gate_ref, x_ref, wosb_ref, womla_ref, wout_ref, gffn_ref, wrh_ref, wrl_ref,
                 br_ref, x1_ref, h2_ref, topw_ref, tope_ref, pos_ref, run_ref):
    tm = TOKEN_TILE
    d = x_ref.shape[1]

    @pl.when(pl.program_id(0) == 0)
    def _():
        run_ref[...] = jnp.zeros_like(run_ref)

    def project(cols):
        tn = (((0,), (0,)), ((), ()))
        a = lax.dot_general(osb_ref[:, cols], wosb_ref[...], tn, preferred_element_type=F32)
        b = lax.dot_general(omla_ref[:, cols], womla_ref[...], tn, preferred_element_type=F32)
        mixed = gate_ref[cols, :d].astype(F32) * a + gate_ref[cols, d:].astype(F32) * b
        x1 = x_ref[cols, :] + _dot(mixed.astype(BF16), wout_ref[...])
        x1_ref[cols, :] = x1
        return x1

    def route(cols, x1):
        h2 = _rms(x1, gffn_ref[...])
        h2_ref[cols, :] = h2

        h_hi = h2.astype(BF16)
        h_lo = (h2 - h_hi.astype(F32)).astype(BF16)
        logits = (_nt_dot(wrh_ref[...], h_hi) + _nt_dot(wrh_ref[...], h_lo) + _nt_dot(wrl_ref[...], h_hi)
                  + br_ref[...])

        eidx = lax.broadcasted_iota(jnp.int32, (N_EXPERTS, tm), 0)
        work = logits
        tops, sels = [], []
        for _ in range(TOP_K):
            mk = jnp.max(work, axis=0, keepdims=True)
            ik = jnp.min(jnp.where(work == mk, eidx, N_EXPERTS), axis=0, keepdims=True)
            sel = eidx == ik
            work = jnp.where(sel, -jnp.inf, work)
            tops.append(mk)
            sels.append(sel)
        exps = [jnp.exp(mk - tops[0]) for mk in tops]
        denom = exps[0] + exps[1] + exps[2] + exps[3]

        chosen = jnp.zeros((N_EXPERTS, tm), F32)
        for sel in sels:
            chosen = chosen + sel.astype(F32)
        rr = lax.broadcasted_iota(jnp.int32, (tm, tm), 0)
        cc = lax.broadcasted_iota(jnp.int32, (tm, tm), 1)
        before = (rr < cc).astype(BF16)
        rank = _dot(chosen.astype(BF16), before) + run_ref[...]
        for k in range(TOP_K):
            topw_ref[k:k + 1, cols] = exps[k] / denom
            tope_ref[k:k + 1, cols] = jnp.sum(jnp.where(sels[k], eidx, 0), axis=0, keepdims=True)
            pos_ref[k:k + 1, cols] = jnp.sum(jnp.where(sels[k], rank, 0.0), axis=0, keepdims=True).astype(jnp.int32)
        run_ref[...] = run_ref[...] + jnp.sum(chosen, axis=1, keepdims=True)

    subs = [slice(j * tm, (j + 1) * tm) for j in range(POST_TILE // tm)]
    x1s = {}
    for j in range(len(subs) + 1):
        if j < len(subs):
            x1s[j] = project(subs[j])
        if j >= 1:
            route(subs[j - 1], x1s.pop(j - 1))


def _post_call(osb, omla, gates, x2, wosb, womla, wout, gffn, wrh, wrl, br):
    n, d = x2.shape
    tm = POST_TILE
    nt = osb.shape[2] // tm
    const = lambda i: (0, 0)
    full = lambda a: pl.BlockSpec(a.shape, const)
    row = lambda w: pl.BlockSpec((tm, w), lambda i: (i, 0))
    col = lambda: pl.BlockSpec((TOP_K, tm), lambda i: (0, i))
    feat = lambda a: pl.BlockSpec((None, a.shape[1], tm), lambda i: (i // nt, 0, i % nt))
    return pl.pallas_call(
        _post_kernel,
        grid=(n // tm,),
        in_specs=[feat(osb), feat(omla), row(2 * d), row(d), full(wosb), full(womla), full(wout),
                  full(gffn), full(wrh), full(wrl), full(br)],
        out_specs=[row(d), row(d), col(), col(), col()],
        out_shape=[jax.ShapeDtypeStruct((n, d), F32), jax.ShapeDtypeStruct((n, d), F32),
                   jax.ShapeDtypeStruct((TOP_K, n), F32), jax.ShapeDtypeStruct((TOP_K, n), jnp.int32),
                   jax.ShapeDtypeStruct((TOP_K, n), jnp.int32)],
        scratch_shapes=[pltpu.VMEM((N_EXPERTS, 1), F32)],
        compiler_params=pltpu.CompilerParams(dimension_semantics=("arbitrary",), vmem_limit_bytes=VMEM_LIMIT),
        name="post",
    )(osb, omla, gates, x2, wosb, womla, wout, gffn, wrh, wrl, br)


SEG = 8
MAX_SLABS = TOKEN_TILE * TOP_K // SEG + N_EXPERTS
SORT_ROWS = MAX_SLABS * SEG


def _for_each(count, fn, unroll=4):
    shift = unroll.bit_length() - 1
    assert unroll == 1 << shift
    groups = lax.shift_right_logical(count, shift)

    def group(i, c):
        for u in range(unroll):
            fn(i * unroll + u)
        return c
    lax.fori_loop(0, groups, group, 0)

    def single(q, c):
        fn(q)
        return c
    lax.fori_loop(lax.shift_left(groups, shift), count, single, 0)


def _sorted_rows(n):
    nt = n // TOKEN_TILE
    return -(-(n * TOP_K + nt * N_EXPERTS * (SEG - 1)) // EXPERT_TILE) * EXPERT_TILE


def _route(top_e, pos, n):
    tm = TOKEN_TILE
    nt = n // tm
    e_ids = jnp.arange(N_EXPERTS, dtype=jnp.int32)
    onehot = top_e[:, :, None] == e_ids[None, None, :]
    c_tile = jnp.sum(onehot.reshape(TOP_K, nt, tm, N_EXPERTS).astype(jnp.int32), axis=(0, 2))
    slabs = (c_tile + SEG - 1) // SEG
    rank_base = jnp.cumsum(c_tile, axis=0) - c_tile
    row_base = (jnp.cumsum(slabs, axis=0) - slabs) * SEG
    padded = jnp.sum(slabs, axis=0) * SEG
    starts = jnp.cumsum(padded) - padded
    seg_global = starts[None, :] + row_base
    slab_end = jnp.cumsum(slabs, axis=1)
    seg_local = (slab_end - slabs) * SEG

    per_token = lambda tab: jnp.broadcast_to(tab[:, None, :], (nt, tm, N_EXPERTS)).reshape(n, N_EXPERTS)
    local = jnp.sum(jnp.where(onehot, per_token(seg_local - rank_base)[None], 0), axis=2) + pos

    q = jnp.arange(MAX_SLABS, dtype=jnp.int32)
    e_q = jnp.minimum(jnp.sum((slab_end[:, None, :] <= q[None, :, None]).astype(jnp.int32), axis=2), N_EXPERTS - 1)
    hit = e_q[:, :, None] == e_ids[None, None, :]
    pick = lambda tab: jnp.sum(jnp.where(hit, tab[:, None, :], 0), axis=2)
    within = (q[None, :] - pick(slab_end - slabs)) * SEG
    n_slabs = slab_end[:, -1]
    live = q[None, :] < n_slabs[:, None]
    slab_local = jnp.where(live, pick(seg_local) + within, 0).astype(jnp.int32)
    slab_global = jnp.where(live, pick(seg_global) + within, 0).astype(jnp.int32)
    return (padded.astype(jnp.int32), local.astype(jnp.int32), n_slabs.astype(jnp.int32),
            slab_local.reshape(nt, 1, MAX_SLABS), slab_global.reshape(nt, 1, MAX_SLABS))


def _disp_kernel(ns_ref, used_ref, sl_ref, sg_ref, local_ref, h2_ref, xs_hbm, xsort, zeros, sem):
    tm = TOKEN_TILE
    g = pl.program_id(0)
    nt = pl.num_programs(0)
    slot = g % 2

    m_iota = lax.broadcasted_iota(jnp.int32, (SORT_ROWS, tm), 0)
    place = jnp.zeros((SORT_ROWS, tm), F32)
    for k in range(TOP_K):
        place = jnp.where(m_iota == local_ref[k:k + 1, :], 1.0, place)
    xsort[slot] = _dot(place.astype(BF16), h2_ref[...].astype(BF16))

    def slab(q, s):
        return pltpu.make_async_copy(xsort.at[s, pl.ds(pl.multiple_of(sl_ref[0, 0, q], SEG), SEG)],
                                     xs_hbm.at[pl.ds(pl.multiple_of(sg_ref[0, 0, q], SEG), SEG)], sem.at[s])

    _for_each(ns_ref[g], lambda q: slab(q, slot).start())

    def drain(count, s):
        _for_each(count, lambda q: pltpu.make_async_copy(
            xsort.at[s, pl.ds(0, SEG)], xs_hbm.at[pl.ds(0, SEG)], sem.at[s]).wait())

    @pl.when(g > 0)
    def _():
        drain(ns_ref[g - 1], 1 - slot)

    @pl.when(g == nt - 1)
    def _():
        drain(ns_ref[g], slot)
        bt = EXPERT_TILE
        zeros[...] = jnp.zeros_like(zeros)
        used = used_ref[0]
        boundary = lax.shift_left(lax.shift_right_logical(used + (bt - 1), bt.bit_length() - 1), bt.bit_length() - 1)
        n_small = lax.shift_right_logical(boundary - used, SEG.bit_length() - 1)
        n_big = lax.shift_right_logical(xs_hbm.shape[0] - boundary, bt.bit_length() - 1)

        def small(q):
            row = pl.multiple_of(used + q * SEG, SEG)
            return pltpu.make_async_copy(zeros.at[pl.ds(0, SEG)], xs_hbm.at[pl.ds(row, SEG)], sem.at[2])

        def big(q):
            row = pl.multiple_of(boundary + q * bt, bt)
            return pltpu.make_async_copy(zeros, xs_hbm.at[pl.ds(row, bt)], sem.at[2])

        for piece, count in ((small, n_small), (big, n_big)):
            lax.fori_loop(0, count, lambda q, c, piece=piece: (piece(q).start(), c)[1], 0)
        for piece, count in ((small, n_small), (big, n_big)):
            lax.fori_loop(0, count, lambda q, c, piece=piece: (piece(q).wait(), c)[1], 0)


def _disp_call(n_slabs, used_rows, slab_local, slab_global, local, h2):
    n, d = h2.shape
    tm = TOKEN_TILE
    assert EXPERT_TILE & (EXPERT_TILE - 1) == 0 and SEG & (SEG - 1) == 0
    lists = pl.BlockSpec((1, 1, MAX_SLABS), lambda g, ns, used: (g, 0, 0), memory_space=pltpu.SMEM)
    grid_spec = pltpu.PrefetchScalarGridSpec(
        num_scalar_prefetch=2,
        grid=(n // tm,),
        in_specs=[lists, lists,
                  pl.BlockSpec((TOP_K, tm), lambda g, ns, used: (0, g)),
                  pl.BlockSpec((tm, d), lambda g, ns, used: (g, 0))],
        out_specs=pl.BlockSpec(memory_space=pl.ANY),
        scratch_shapes=[pltpu.VMEM((2, SORT_ROWS, d), F32), pltpu.VMEM((EXPERT_TILE, d), F32),
                        pltpu.SemaphoreType.DMA((3,))],
    )
    return pl.pallas_call(
        _disp_kernel,
        grid_spec=grid_spec,
        out_shape=jax.ShapeDtypeStruct((_sorted_rows(n), d), F32),
        compiler_params=pltpu.CompilerParams(dimension_semantics=("arbitrary",), vmem_limit_bytes=VMEM_LIMIT),
        name="disp",
    )(n_slabs, used_rows, slab_local, slab_global, local, h2)


def _moe_kernel(blk_ref, xblk_ref, exp_ref, lo_ref, hi_ref, first_ref, new_ref, nxt_ref, slot_ref,
                xs_ref, wup_hbm, bup_ref, wdn_hbm, bdn_ref, y_ref, wupf, wdnf, wupb, wdnp, wdnb, sem):
    t = EXPERT_TILE
    w = pl.program_id(0)
    lo = lo_ref[w]
    hi = hi_ref[w]
    half = wdnb.shape[0]
    d = wdnb.shape[1]

    def weight_copies(e, s):
        return (pltpu.make_async_copy(wup_hbm.at[e], wupf.at[s], sem.at[0, s]),
                pltpu.make_async_copy(wdn_hbm.at[e], wdnf.at[s], sem.at[1, s]))

    @pl.when(w == 0)
    def _():
        for c in weight_copies(exp_ref[0], 0):
            c.start(priority=1)

    @pl.when(new_ref[w] == 1)
    def _():
        s = slot_ref[w]
        for c in weight_copies(exp_ref[w], s):
            c.wait()

        @pl.when(nxt_ref[w] >= 0)
        def _():
            for c in weight_copies(nxt_ref[w], 1 - s):
                c.start(priority=1)

        wupb[...] = wupf[s].astype(BF16)
        for c in range(d // LANES):
            sl = slice(c * LANES, (c + 1) * LANES)
            wdnp[c, pl.ds(0, half // 2, stride=2), :] = wdnf[s, :half // 2, sl]
            wdnp[c, pl.ds(1, half // 2, stride=2), :] = wdnf[s, half // 2:, sl]
            wdnb[:, sl] = wdnp[c].astype(BF16)

    @pl.when(hi > lo)
    def _():
        x = xs_ref[...].astype(BF16)
        gu = _dot(x, wupb[...]) + bup_ref[...]
        lane = lax.broadcasted_iota(jnp.int32, (1, LANES), 1)
        even = (lane % 2) == 0
        acts = []
        for c in range(half // LANES):
            a = gu[:, c * LANES:(c + 1) * LANES]
            b = gu[:, half + c * LANES:half + (c + 1) * LANES]
            x_glu = jnp.minimum(jnp.where(even, a, pltpu.roll(b, 1, 1)), SWIGLU_LIMIT)
            x_lin = jnp.clip(jnp.where(even, pltpu.roll(a, LANES - 1, 1), b), -SWIGLU_LIMIT, SWIGLU_LIMIT)
            acts.append((x_glu * jax.nn.sigmoid(SWIGLU_ALPHA * x_glu) * (x_lin + 1.0)).astype(BF16))
        act = jnp.concatenate(acts, axis=1)
        y = _dot(act, wdnb[...]) + bdn_ref[...]
        row = lax.broadcasted_iota(jnp.int32, (t, 1), 0)
        mine = jnp.logical_and(row >= lo, row < hi)
        keep = jnp.where(first_ref[w] == 1, jnp.zeros_like(y), y_ref[...])
        y_ref[...] = jnp.where(mine, y, keep)

    @pl.when(jnp.logical_and(hi <= lo, first_ref[w] == 1))
    def _():
        y_ref[...] = jnp.zeros_like(y_ref)


def _moe_call(items, xs, wup, bup, wdn, bdn):
    t = EXPERT_TILE
    n_rows, d = xs.shape
    ff = wdn.shape[1]
    n_items = items[0].shape[0]
    by_blk = lambda w, blk, xblk, ex, *_: (blk[w], 0)
    by_xblk = lambda w, blk, xblk, ex, *_: (xblk[w], 0)
    by_exp = lambda w, blk, xblk, ex, *_: (ex[w], 0, 0)
    grid_spec = pltpu.PrefetchScalarGridSpec(
        num_scalar_prefetch=len(items),
        grid=(n_items,),
        in_specs=[pl.BlockSpec((t, d), by_xblk),
                  pl.BlockSpec(memory_space=pl.ANY),
                  pl.BlockSpec((None, 1, 2 * ff), by_exp),
                  pl.BlockSpec(memory_space=pl.ANY),
                  pl.BlockSpec((None, 1, d), by_exp)],
        out_specs=pl.BlockSpec((t, d), by_blk),
        scratch_shapes=[pltpu.VMEM((2, d, 2 * ff), F32),
                        pltpu.VMEM((2, ff, d), F32),
                        pltpu.VMEM((d, 2 * ff), BF16),
                        pltpu.VMEM((d // LANES, ff, LANES), F32),
                        pltpu.VMEM((ff, d), BF16),
                        pltpu.SemaphoreType.DMA((2, 2))],
    )
    return pl.pallas_call(
        _moe_kernel,
        grid_spec=grid_spec,
        out_shape=jax.ShapeDtypeStruct((n_rows, d), F32),
        compiler_params=pltpu.CompilerParams(dimension_semantics=("arbitrary",), vmem_limit_bytes=MOE_VMEM_LIMIT),
        name="moe",
    )(*items, xs, wup, bup, wdn, bdn)


def _moe_items(counts, n_rows):
    t = EXPERT_TILE
    nb = n_rows // t
    n_items = nb + N_EXPERTS - 1
    ends = jnp.cumsum(counts)
    starts = ends - counts
    first_blk = starts // t
    tiles = jnp.where(counts > 0, (ends - 1) // t - first_blk + 1, 0)
    item_end = jnp.cumsum(tiles)
    item_start = item_end - tiles
    total = item_end[-1]
    w = jnp.arange(n_items, dtype=jnp.int32)
    wc = jnp.minimum(w, total - 1)
    ex = jnp.sum((item_end[None, :] <= wc[:, None]).astype(jnp.int32), axis=1)
    onehot = (ex[:, None] == jnp.arange(N_EXPERTS, dtype=jnp.int32)[None, :]).astype(jnp.int32)
    pick = lambda v: jnp.sum(onehot * v[None, :], axis=1)
    xblk = pick(first_blk) + wc - pick(item_start)
    lo = jnp.clip(pick(starts) - xblk * t, 0, t)
    hi = jnp.where(w < total, jnp.clip(pick(ends) - xblk * t, 0, t), lo)
    used_blocks = (ends[-1] + t - 1) // t
    blk = jnp.where(w < total, xblk, jnp.minimum(used_blocks + w - total, nb - 1))
    first = jnp.concatenate([jnp.ones((1,), jnp.int32), (blk[1:] != blk[:-1]).astype(jnp.int32)])
    new = jnp.logical_and(w < total, w == pick(item_start)).astype(jnp.int32)
    e_ids = jnp.arange(N_EXPERTS, dtype=jnp.int32)
    later = jnp.logical_and(e_ids[None, :] > e_ids[:, None], counts[None, :] > 0)
    next_e = jnp.min(jnp.where(later, e_ids[None, :], N_EXPERTS), axis=1)
    next_e = jnp.where(next_e == N_EXPERTS, -1, next_e)
    ordinal = jnp.cumsum((counts > 0).astype(jnp.int32)) - 1
    as_i32 = lambda v: v.astype(jnp.int32)
    return tuple(map(as_i32, (blk, xblk, ex, lo, hi, first, new, pick(next_e), pick(ordinal) % 2)))


def _comb_kernel(ns_ref, sl_ref, sg_ref, nsl_ref, nsg_ref, local_ref, w_ref, y_hbm, x1_ref, o_ref, ysort, sem):
    tm = TOKEN_TILE
    g = pl.program_id(0)
    nt = pl.num_programs(0)
    slot = g % 2

    def slab(lref, gref, q, s):
        return pltpu.make_async_copy(y_hbm.at[pl.ds(pl.multiple_of(gref[0, 0, q], SEG), SEG)],
                                     ysort.at[s, pl.ds(pl.multiple_of(lref[0, 0, q], SEG), SEG)], sem.at[s])

    def fetch(lref, gref, count, s):
        _for_each(count, lambda q: slab(lref, gref, q, s).start())

    @pl.when(g == 0)
    def _():
        ysort[...] = jnp.zeros_like(ysort)
        fetch(sl_ref, sg_ref, ns_ref[0], 0)

    @pl.when(g + 1 < nt)
    def _():
        fetch(nsl_ref, nsg_ref, ns_ref[jnp.minimum(g + 1, nt - 1)], 1 - slot)

    _for_each(ns_ref[g], lambda q: pltpu.make_async_copy(
        y_hbm.at[pl.ds(0, SEG)], ysort.at[slot, pl.ds(0, SEG)], sem.at[slot]).wait())

    m_iota = lax.broadcasted_iota(jnp.int32, (tm, SORT_ROWS), 1)
    mix = jnp.zeros((tm, SORT_ROWS), F32)
    for k in range(TOP_K):
        mix = jnp.where(m_iota == local_ref[:, k:k + 1], w_ref[:, k:k + 1], mix)
    mix = mix.astype(BF16)
    o_ref[...] = x1_ref[...] + _dot(mix, ysort[slot].astype(BF16))


def _comb_call(n_slabs, slab_local, slab_global, local_t, w_t, y, x1):
    n, d = x1.shape
    tm = TOKEN_TILE
    nt = n // tm
    cur = lambda g, ns: (g, 0, 0)
    nxt = lambda g, ns: (jnp.minimum(g + 1, nt - 1), 0, 0)
    lists = lambda index_map: pl.BlockSpec((1, 1, MAX_SLABS), index_map, memory_space=pltpu.SMEM)
    grid_spec = pltpu.PrefetchScalarGridSpec(
        num_scalar_prefetch=1,
        grid=(nt,),
        in_specs=[lists(cur), lists(cur), lists(nxt), lists(nxt),
                  pl.BlockSpec((tm, TOP_K), lambda g, ns: (g, 0)),
                  pl.BlockSpec((tm, TOP_K), lambda g, ns: (g, 0)),
                  pl.BlockSpec(memory_space=pl.ANY),
                  pl.BlockSpec((tm, d), lambda g, ns: (g, 0))],
        out_specs=pl.BlockSpec((tm, d), lambda g, ns: (g, 0)),
        scratch_shapes=[pltpu.VMEM((2, SORT_ROWS, d), F32), pltpu.SemaphoreType.DMA((2,))],
    )
    return pl.pallas_call(
        _comb_kernel,
        grid_spec=grid_spec,
        out_shape=jax.ShapeDtypeStruct((n, d), F32),
        compiler_params=pltpu.CompilerParams(dimension_semantics=("arbitrary",), vmem_limit_bytes=VMEM_LIMIT),
        name="comb",
    )(n_slabs, slab_local, slab_global, slab_local, slab_global, local_t, w_t, y, x1)


def _layer(x2, pos2, bsz, seq, g_mix, w_in, g_cq, w_uq, g_ckv, w_ukv, g_qnorm, g_knorm, w_o_sb, w_o_mla, w_out,
           g_ffn, w_router, b_router, w_up, b_up, w_down, b_down):
    n, d = x2.shape
    c_sb = 3 * SB_WIDTH
    c_q = c_sb + MLA_Q_LORA
    c_kv = c_q + MLA_KV_LORA
    c_kr = c_kv + MLA_ROPE
    w1 = w_in[:, :2 * SB_WIDTH].astype(BF16)
    w1v = w_in[:, 2 * SB_WIDTH:c_sb].T.astype(BF16)
    w_kr = jnp.pad(w_in[:, c_kv:c_kr], ((0, 0), (MLA_NOPE, HEAD_PAD - MLA_QK)))
    w2 = jnp.concatenate([w_in[:, c_sb:c_kv], w_kr], axis=1).astype(BF16)
    w3 = w_in[:, c_kr:].astype(BF16)
    wuq = jnp.pad(w_uq.reshape(MLA_Q_LORA, MLA_HEADS, MLA_QK), ((0, 0), (0, 0), (0, HEAD_PAD - MLA_QK)))
    wuq = wuq.reshape(MLA_Q_LORA, MLA_HEADS * HEAD_PAD).astype(BF16)
    wukv = w_ukv.reshape(MLA_KV_LORA, MLA_HEADS, MLA_NOPE + MLA_V)
    wuk = jnp.pad(wukv[:, :, :MLA_NOPE], ((0, 0), (0, 0), (0, HEAD_PAD - MLA_NOPE)))
    wuk = wuk.reshape(MLA_KV_LORA, MLA_HEADS * HEAD_PAD).astype(BF16)
    wuv = wukv[:, :, MLA_NOPE:].reshape(MLA_KV_LORA, MLA_HEADS * MLA_V).T.astype(BF16)
    gq = jnp.tile(jnp.pad(g_qnorm, (0, HEAD_PAD - MLA_QK)), MLA_HEADS)[None, :]
    gk = jnp.tile(jnp.pad(g_knorm, (0, HEAD_PAD - MLA_QK)), MLA_HEADS)[None, :]
    half = MLA_ROPE // 2
    inv_freq = ROPE_THETA ** (-jnp.arange(half, dtype=F32) / half)
    invf = jnp.pad(jnp.concatenate([inv_freq, inv_freq]), (MLA_NOPE, HEAD_PAD - MLA_QK))[None, :]
    sgn = jnp.pad(jnp.concatenate([-jnp.ones((half,), F32), jnp.ones((half,), F32)]),
                  (MLA_NOPE, HEAD_PAD - MLA_QK))[None, :]

    sb, sbv, qm, km, vm, gates = _pre_call(x2, pos2, bsz, seq, g_mix[None, :], w1, w1v, w2, w3, g_cq[None, :],
                                           g_ckv[None, :], wuq, wuk, wuv, gq, gk, invf, sgn)
    o_sb = _sb_call(sb, sbv, bsz, seq)
    o_mla = _mla_call(qm, km, vm, bsz, seq)

    wr_t = w_router.T
    wr_hi = wr_t.astype(BF16)
    wr_lo = (wr_t - wr_hi.astype(F32)).astype(BF16)
    x1, h2, top_w, top_e, pos = _post_call(
        o_sb, o_mla, gates, x2, w_o_sb.astype(BF16), w_o_mla.astype(BF16), w_out.astype(BF16), g_ffn[None, :],
        wr_hi, wr_lo, b_router[:, None])

    padded, local, n_slabs, slab_local, slab_global = _route(top_e, pos, n)
    xs = _disp_call(n_slabs, jnp.sum(padded)[None], slab_local, slab_global, local, h2)
    y = _moe_call(_moe_items(padded, xs.shape[0]), xs, w_up, b_up[:, None, :], w_down, b_down[:, None, :])
    return _comb_call(n_slabs, slab_local, slab_global, local.T, top_w.T, y, x1)


def kernel(x, positions, g_mix, w_in, g_cq, w_uq, g_ckv, w_ukv, g_qnorm, g_knorm, w_o_sb, w_o_mla, w_out, g_ffn,
           w_router, b_router, w_up, b_up, w_down, b_down):
    bsz, seq, d = x.shape
    x2 = x.reshape(bsz * seq, d)
    pos2 = positions.reshape(bsz * seq, 1)
    for l in range(g_mix.shape[0]):
        x2 = _layer(x2, pos2, bsz, seq, g_mix[l], w_in[l], g_cq[l], w_uq[l], g_ckv[l], w_ukv[l], g_qnorm[l],
                    g_knorm[l], w_o_sb[l], w_o_mla[l], w_out[l], g_ffn[l], w_router[l], b_router[l], w_up[l],
                    b_up[l], w_down[l], b_down[l])
    return x2.reshape(bsz, seq, d)
```

```python
import math

import jax
import jax.numpy as jnp
from jax import lax
from jax.experimental import pallas as pl
from jax.experimental.pallas import tpu as pltpu

F32 = jnp.float32
BF16 = jnp.bfloat16

EPS = 1e-6
CHUNK = 64
SB_HEADS = 8
SB_DIM = 64
SB_WIDTH = SB_HEADS * SB_DIM
MLA_HEADS = 8
MLA_NOPE = 64
MLA_ROPE = 32
MLA_QK = MLA_NOPE + MLA_ROPE
MLA_V = 64
MLA_Q_LORA = 384
MLA_KV_LORA = 256
ROPE_THETA = 10000.0
N_EXPERTS = 32
TOP_K = 4
SWIGLU_LIMIT = 7.0
SWIGLU_ALPHA = 1.702

LANES = 128
HEAD_PAD = LANES
TOKEN_TILE = 256
PRE_TILE = 512
POST_TILE = 512
ATTN_TILE = 256
EXPERT_TILE = 256
SB_SKEW = 1
LOG2E = math.log2(math.e)
SB_DEAD = -160.0
VMEM_LIMIT = 48 * 1024 * 1024
MOE_VMEM_LIMIT = 56 * 1024 * 1024


def _nt_dot(a, b):
    return lax.dot_general(a, b, (((1,), (1,)), ((), ())), preferred_element_type=F32)


def _dot(a, b):
    return jnp.dot(a, b, preferred_element_type=F32)


def _rms(t, g):
    return t * lax.rsqrt(jnp.mean(t * t, axis=-1, keepdims=True) + EPS) * g


def _pre_kernel(x_ref, pos_ref, gmix_ref, w1_ref, w1v_ref, w2_ref, w3_ref, gcq_ref, gckv_ref, wuq_ref, wuk_ref,
                wuv_ref, gq_ref, gk_ref, invf_ref, sgn_ref,
                sb_ref, sbv_ref, qm_ref, km_ref, vm_ref, gate_ref):
    h = _rms(x_ref[...], gmix_ref[...]).astype(BF16)

    lat = _dot(h, w2_ref[...])
    cqn = _rms(lat[:, :MLA_Q_LORA], gcq_ref[...]).astype(BF16)
    ckvn = _rms(lat[:, MLA_Q_LORA:MLA_Q_LORA + MLA_KV_LORA], gckv_ref[...]).astype(BF16)
    kr = lat[:, MLA_Q_LORA + MLA_KV_LORA:]
    qf = _dot(cqn, wuq_ref[...])
    kf = _dot(ckvn, wuk_ref[...])
    key_tiles = [slice(j * ATTN_TILE, (j + 1) * ATTN_TILE) for j in range(PRE_TILE // ATTN_TILE)]
    for j, rows in enumerate(key_tiles):
        vm_ref[j] = _nt_dot(wuv_ref[...], ckvn[rows, :]).astype(BF16)

    sb_ref[:, :SB_WIDTH] = (_dot(h, w1_ref[:, :SB_WIDTH]) * (LOG2E / math.sqrt(SB_DIM))).astype(BF16)
    sb_ref[:, SB_WIDTH:] = _dot(h, w1_ref[:, SB_WIDTH:]).astype(BF16)
    for j, rows in enumerate(key_tiles):
        sbv_ref[j] = _nt_dot(w1v_ref[...], h[rows, :]).astype(BF16)

    gate_ref[...] = jax.nn.sigmoid(_dot(h, w3_ref[...])).astype(BF16)

    ang = pos_ref[...].astype(F32) * invf_ref[...]
    cos = jnp.cos(ang)
    sin = jnp.sin(ang) * sgn_ref[...]
    lane = lax.broadcasted_iota(jnp.int32, (1, LANES), 1)
    first_half = lane < MLA_NOPE + MLA_ROPE // 2

    def rope(t):
        partner = jnp.where(first_half, pltpu.roll(t, LANES - MLA_ROPE // 2, 1), pltpu.roll(t, MLA_ROPE // 2, 1))
        return t * cos + partner * sin

    def head_norm(t, g):
        return t * lax.rsqrt(jnp.sum(t * t, axis=-1, keepdims=True) * (1.0 / MLA_QK) + EPS) * g

    mla_scale = LOG2E / math.sqrt(MLA_QK)
    for hd in range(MLA_HEADS):
        sl = slice(hd * HEAD_PAD, (hd + 1) * HEAD_PAD)
        qm_ref[:, sl] = (rope(head_norm(qf[:, sl], gq_ref[:, sl])) * mla_scale).astype(BF16)
        km_ref[:, sl] = rope(head_norm(kf[:, sl] + kr, gk_ref[:, sl])).astype(BF16)


def _pre_call(x2, pos2, bsz, seq, gmix, w1, w1v, w2, w3, gcq, gckv, wuq, wuk, wuv, gq, gk, invf, sgn):
    n, d = x2.shape
    tm = PRE_TILE
    ta = ATTN_TILE
    assert tm % ta == 0 and seq % tm == 0
    nt = seq // ta
    steps = seq // tm
    const = lambda i: (0, 0)
    full = lambda a: pl.BlockSpec(a.shape, const)
    row = lambda w: pl.BlockSpec((tm, w), lambda i: (i, 0))
    vt_spec = lambda w: pl.BlockSpec((None, tm // ta, w, ta), lambda i: (i // steps, i % steps, 0, 0))
    return pl.pallas_call(
        _pre_kernel,
        grid=(n // tm,),
        in_specs=[row(d), row(1), full(gmix), full(w1), full(w1v), full(w2), full(w3), full(gcq), full(gckv),
                  full(wuq), full(wuk), full(wuv), full(gq), full(gk), full(invf), full(sgn)],
        out_specs=[row(2 * SB_WIDTH), vt_spec(SB_WIDTH), row(MLA_HEADS * HEAD_PAD), row(MLA_HEADS * HEAD_PAD),
                   vt_spec(MLA_HEADS * MLA_V), row(2 * d)],
        out_shape=[jax.ShapeDtypeStruct((n, 2 * SB_WIDTH), BF16),
                   jax.ShapeDtypeStruct((bsz, nt, SB_WIDTH, ta), BF16),
                   jax.ShapeDtypeStruct((n, MLA_HEADS * HEAD_PAD), BF16),
                   jax.ShapeDtypeStruct((n, MLA_HEADS * HEAD_PAD), BF16),
                   jax.ShapeDtypeStruct((bsz, nt, MLA_HEADS * MLA_V, ta), BF16),
                   jax.ShapeDtypeStruct((n, 2 * d), BF16)],
        compiler_params=pltpu.CompilerParams(dimension_semantics=("arbitrary",), vmem_limit_bytes=VMEM_LIMIT),
        name="pre",
    )(x2, pos2, gmix, w1, w1v, w2, w3, gcq, gckv, wuq, wuk, wuv, gq, gk, invf, sgn)


def _sb_kernel(q_ref, k_ref, v_ref, o_ref, acc_ref, car_ref, za_ref, zb_ref):
    t = ATTN_TILE
    i = pl.program_id(1)
    lane = lax.broadcasted_iota(jnp.int32, (1, LANES), 1)
    lo = lane < SB_DIM
    rr = lax.broadcasted_iota(jnp.int32, (t, t), 0)
    cc = lax.broadcasted_iota(jnp.int32, (t, t), 1)
    later_sum = (cc > rr).astype(BF16)
    strict = rr < cc

    acc_ref[...] = jnp.zeros_like(acc_ref)
    car_ref[...] = jnp.zeros_like(car_ref)

    def produce(dst, j, hd):
        krows = pl.ds(pl.multiple_of(jnp.maximum(j, 0) * t, t), t)
        pair = slice((hd // 2) * LANES, (hd // 2 + 1) * LANES)
        q2 = q_ref[:, pair]
        zero = jnp.zeros_like(q2)
        qh = jnp.where(lo, q2, zero) if hd % 2 == 0 else jnp.where(lo, zero, q2)
        dst[hd] = _nt_dot(k_ref[krows, pair], qh)

    def step(src, dst, j, masked):
        def stay(z):
            neg_abs = pltpu.bitcast(pltpu.bitcast(z, jnp.uint32) | jnp.uint32(0x80000000), F32)
            sp = jnp.log2(1.0 + jnp.exp2(neg_abs))
            log_beta = jnp.minimum(z, 0.0) - sp
            log_stay = log_beta - z
            if masked:
                log_stay = jnp.where(strict, log_stay, 0.0)
            later = _dot(later_sum, log_stay.astype(BF16))
            return log_beta, later, jnp.sum(log_stay, axis=0, keepdims=True)

        def weigh(hd, log_beta, later, total):
            carry = car_ref[hd]
            a = jnp.exp2(log_beta + later + carry)
            if masked:
                a = jnp.where(strict, a, 0.0)
            rows = slice(hd * SB_DIM, (hd + 1) * SB_DIM)
            acc_ref[rows, :] += _dot(v_ref[j, rows, :], a.astype(BF16))
            car_ref[hd] = carry + total

        mids = {}
        for s in range(SB_HEADS + SB_SKEW):
            if s < SB_HEADS:
                mids[s] = stay(src[s])
                produce(dst, j - 1, s)
            hd = s - SB_SKEW
            if 0 <= hd < SB_HEADS:
                weigh(hd, *mids.pop(hd))

    for hd in range(SB_HEADS):
        produce(za_ref, i, hd)
    step(za_ref, zb_ref, i, True)

    def cond(c):
        j, alive = c
        return jnp.logical_and(j >= 0, alive > SB_DEAD)

    def body(c):
        j, _ = c
        step(zb_ref, za_ref, j, False)

        @pl.when(jnp.logical_and(j >= 1, jnp.max(car_ref[...]) > SB_DEAD))
        def _():
            step(za_ref, zb_ref, j - 1, False)

        return j - 2, jnp.max(car_ref[...])

    lax.while_loop(cond, body, (i - 1, jnp.max(car_ref[...])))
    o_ref[...] = acc_ref[...].astype(o_ref.dtype)


def _sb_call(sb, sbv, bsz, seq):
    t = ATTN_TILE
    nq = seq // t
    return pl.pallas_call(
        _sb_kernel,
        grid=(bsz, nq),
        in_specs=[pl.BlockSpec((t, SB_WIDTH), lambda b, i: (b * nq + i, 0)),
                  pl.BlockSpec((seq, SB_WIDTH), lambda b, i: (b, 1)),
                  pl.BlockSpec((None, nq, SB_WIDTH, t), lambda b, i: (b, 0, 0, 0))],
        out_specs=pl.BlockSpec((None, SB_WIDTH, t), lambda b, i: (b, 0, i)),
        out_shape=jax.ShapeDtypeStruct((bsz, SB_WIDTH, seq), BF16),
        scratch_shapes=[pltpu.VMEM((SB_WIDTH, t), F32), pltpu.VMEM((SB_HEADS, 1, t), F32),
                        pltpu.VMEM((SB_HEADS, t, t), F32), pltpu.VMEM((SB_HEADS, t, t), F32)],
        compiler_params=pltpu.CompilerParams(dimension_semantics=("arbitrary",) * 2, vmem_limit_bytes=VMEM_LIMIT),
        name="sb_attn",
    )(sb, sb, sbv)


def _mla_kernel(q_ref, k_ref, v_ref, o_ref, m_ref, l_ref, acc_ref, sa_ref, sb_ref):
    t = ATTN_TILE
    i = pl.program_id(1)
    rr = lax.broadcasted_iota(jnp.int32, (t, t), 0)
    cc = lax.broadcasted_iota(jnp.int32, (t, t), 1)
    visible = (rr // CHUNK) <= (cc // CHUNK)

    m_ref[...] = jnp.full_like(m_ref, -jnp.inf)
    l_ref[...] = jnp.zeros_like(l_ref)
    acc_ref[...] = jnp.zeros_like(acc_ref)

    def produce(dst, j, heads=range(MLA_HEADS)):
        krows = pl.ds(pl.multiple_of(j * t, t), t)
        for hd in heads:
            sl = slice(hd * HEAD_PAD, (hd + 1) * HEAD_PAD)
            dst[hd] = _nt_dot(k_ref[krows, sl], q_ref[:, sl])

    def consume(src, j, masked, heads=range(MLA_HEADS)):
        for hd in heads:
            s = src[hd]
            if masked:
                s = jnp.where(visible, s, -jnp.inf)
            m_old = m_ref[hd]
            m_new = jnp.maximum(m_old, jnp.max(s, axis=0, keepdims=True))
            alpha = jnp.exp2(m_old - m_new)
            p = jnp.exp2(s - m_new)
            l_ref[hd] = alpha * l_ref[hd] + jnp.sum(p, axis=0, keepdims=True)
            rows = slice(hd * MLA_V, (hd + 1) * MLA_V)
            acc_ref[rows, :] = alpha * acc_ref[rows, :] + _dot(v_ref[j, rows, :], p.astype(BF16))
            m_ref[hd] = m_new

    def step(src, dst, j):
        for hd in range(MLA_HEADS):
            consume(src, j, False, [hd])
            produce(dst, j + 1, [hd])

    produce(sa_ref, 0)

    def pair(jj, c):
        step(sa_ref, sb_ref, 2 * jj)
        step(sb_ref, sa_ref, 2 * jj + 1)
        return c

    lax.fori_loop(0, i // 2, pair, 0)
    odd = i % 2

    @pl.when(odd == 1)
    def _():
        step(sa_ref, sb_ref, i - 1)
        consume(sb_ref, i, True)

    @pl.when(odd == 0)
    def _():
        consume(sa_ref, i, True)

    for hd in range(MLA_HEADS):
        rows = slice(hd * MLA_V, (hd + 1) * MLA_V)
        o_ref[rows, :] = (acc_ref[rows, :] / l_ref[hd]).astype(o_ref.dtype)


def _mla_call(qm, km, vm, bsz, seq):
    t = ATTN_TILE
    nq = seq // t
    width = MLA_HEADS * HEAD_PAD
    return pl.pallas_call(
        _mla_kernel,
        grid=(bsz, nq),
        in_specs=[pl.BlockSpec((t, width), lambda b, i: (b * nq + i, 0)),
                  pl.BlockSpec((seq, width), lambda b, i: (b, 0)),
                  pl.BlockSpec((None, nq, MLA_HEADS * MLA_V, t), lambda b, i: (b, 0, 0, 0))],
        out_specs=pl.BlockSpec((None, MLA_HEADS * MLA_V, t), lambda b, i: (b, 0, i)),
        out_shape=jax.ShapeDtypeStruct((bsz, MLA_HEADS * MLA_V, seq), BF16),
        scratch_shapes=[pltpu.VMEM((MLA_HEADS, 1, t), F32), pltpu.VMEM((MLA_HEADS, 1, t), F32),
                        pltpu.VMEM((MLA_HEADS * MLA_V, t), F32),
                        pltpu.VMEM((MLA_HEADS, t, t), F32), pltpu.VMEM((MLA_HEADS, t, t), F32)],
        compiler_params=pltpu.CompilerParams(dimension_semantics=("arbitrary",) * 2, vmem_limit_bytes=VMEM_LIMIT),
        name="mla_attn",
    )(qm, km, vm)


def _post_kernel(osb_ref, omla_ref, gate_ref, x_ref, wosb_ref, womla_ref, wout_ref, gffn_ref, wrh_ref, wrl_ref,
                 br_ref, x1_ref, h2_ref, topw_ref, tope_ref, pos_ref, run_ref):
    tm = TOKEN_TILE
    d = x_ref.shape[1]

    @pl.when(pl.program_id(0) == 0)
    def _():
        run_ref[...] = jnp.zeros_like(run_ref)

    def project(cols):
        tn = (((0,), (0,)), ((), ()))
        a = lax.dot_general(osb_ref[:, cols], wosb_ref[...], tn, preferred_element_type=F32)
        b = lax.dot_general(omla_ref[:, cols], womla_ref[...], tn, preferred_element_type=F32)
        mixed = gate_ref[cols, :d].astype(F32) * a + gate_ref[cols, d:].astype(F32) * b
        x1 = x_ref[cols, :] + _dot(mixed.astype(BF16), wout_ref[...])
        x1_ref[cols, :] = x1
        return x1

    def route(cols, x1):
        h2 = _rms(x1, gffn_ref[...])
        h2_ref[cols, :] = h2

        h_hi = h2.astype(BF16)
        h_lo = (h2 - h_hi.astype(F32)).astype(BF16)
        logits = (_nt_dot(wrh_ref[...], h_hi) + _nt_dot(wrh_ref[...], h_lo) + _nt_dot(wrl_ref[...], h_hi)
                  + br_ref[...])

        eidx = lax.broadcasted_iota(jnp.int32, (N_EXPERTS, tm), 0)
        work = logits
        tops, sels = [], []
        for _ in range(TOP_K):
            mk = jnp.max(work, axis=0, keepdims=True)
            ik = jnp.min(jnp.where(work == mk, eidx, N_EXPERTS), axis=0, keepdims=True)
            sel = eidx == ik
            work = jnp.where(sel, -jnp.inf, work)
            tops.append(mk)
            sels.append(sel)
        exps = [jnp.exp(mk - tops[0]) for mk in tops]
        denom = exps[0] + exps[1] + exps[2] + exps[3]

        chosen = jnp.zeros((N_EXPERTS, tm), F32)
        for sel in sels:
            chosen = chosen + sel.astype(F32)
        rr = lax.broadcasted_iota(jnp.int32, (tm, tm), 0)
        cc = lax.broadcasted_iota(jnp.int32, (tm, tm), 1)
        before = (rr < cc).astype(BF16)
        rank = _dot(chosen.astype(BF16), before) + run_ref[...]
        for k in range(TOP_K):
            topw_ref[k:k + 1, cols] = exps[k] / denom
            tope_ref[k:k + 1, cols] = jnp.sum(jnp.where(sels[k], eidx, 0), axis=0, keepdims=True)
            pos_ref[k:k + 1, cols] = jnp.sum(jnp.where(sels[k], rank, 0.0), axis=0, keepdims=True).astype(jnp.int32)
        run_ref[...] = run_ref[...] + jnp.sum(chosen, axis=1, keepdims=True)

    subs = [slice(j * tm, (j + 1) * tm) for j in range(POST_TILE // tm)]
    x1s = {}
    for j in range(len(subs) + 1):
        if j < len(subs):
            x1s[j] = project(subs[j])
        if j >= 1:
            route(subs[j - 1], x1s.pop(j - 1))


def _post_call(osb, omla, gates, x2, wosb, womla, wout, gffn, wrh, wrl, br):
    n, d = x2.shape
    tm = POST_TILE
    nt = osb.shape[2] // tm
    const = lambda i: (0, 0)
    full = lambda a: pl.BlockSpec(a.shape, const)
    row = lambda w: pl.BlockSpec((tm, w), lambda i: (i, 0))
    col = lambda: pl.BlockSpec((TOP_K, tm), lambda i: (0, i))
    feat = lambda a: pl.BlockSpec((None, a.shape[1], tm), lambda i: (i // nt, 0, i % nt))
    return pl.pallas_call(
        _post_kernel,
        grid=(n // tm,),
        in_specs=[feat(osb), feat(omla), row(2 * d), row(d), full(wosb), full(womla), full(wout),
                  full(gffn), full(wrh), full(wrl), full(br)],
        out_specs=[row(d), row(d), col(), col(), col()],
        out_shape=[jax.ShapeDtypeStruct((n, d), F32), jax.ShapeDtypeStruct((n, d), F32),
                   jax.ShapeDtypeStruct((TOP_K, n), F32), jax.ShapeDtypeStruct((TOP_K, n), jnp.int32),
                   jax.ShapeDtypeStruct((TOP_K, n), jnp.int32)],
        scratch_shapes=[pltpu.VMEM((N_EXPERTS, 1), F32)],
        compiler_params=pltpu.CompilerParams(dimension_semantics=("arbitrary",), vmem_limit_bytes=VMEM_LIMIT),
        name="post",
    )(osb, omla, gates, x2, wosb, womla, wout, gffn, wrh, wrl, br)


SEG = 8
MAX_SLABS = TOKEN_TILE * TOP_K // SEG + N_EXPERTS
SORT_ROWS = MAX_SLABS * SEG


def _for_each(count, fn, unroll=4):
    shift = unroll.bit_length() - 1
    assert unroll == 1 << shift
    groups = lax.shift_right_logical(count, shift)

    def group(i, c):
        for u in range(unroll):
            fn(i * unroll + u)
        return c
    lax.fori_loop(0, groups, group, 0)

    def single(q, c):
        fn(q)
        return c
    lax.fori_loop(lax.shift_left(groups, shift), count, single, 0)


def _sorted_rows(n):
    nt = n // TOKEN_TILE
    return -(-(n * TOP_K + nt * N_EXPERTS * (SEG - 1)) // EXPERT_TILE) * EXPERT_TILE


def _route(top_e, pos, n):
    tm = TOKEN_TILE
    nt = n // tm
    e_ids = jnp.arange(N_EXPERTS, dtype=jnp.int32)
    onehot = top_e[:, :, None] == e_ids[None, None, :]
    c_tile = jnp.sum(onehot.reshape(TOP_K, nt, tm, N_EXPERTS).astype(jnp.int32), axis=(0, 2))
    slabs = (c_tile + SEG - 1) // SEG
    rank_base = jnp.cumsum(c_tile, axis=0) - c_tile
    row_base = (jnp.cumsum(slabs, axis=0) - slabs) * SEG
    padded = jnp.sum(slabs, axis=0) * SEG
    starts = jnp.cumsum(padded) - padded
    seg_global = starts[None, :] + row_base
    slab_end = jnp.cumsum(slabs, axis=1)
    seg_local = (slab_end - slabs) * SEG

    per_token = lambda tab: jnp.broadcast_to(tab[:, None, :], (nt, tm, N_EXPERTS)).reshape(n, N_EXPERTS)
    local = jnp.sum(jnp.where(onehot, per_token(seg_local - rank_base)[None], 0), axis=2) + pos

    q = jnp.arange(MAX_SLABS, dtype=jnp.int32)
    e_q = jnp.minimum(jnp.sum((slab_end[:, None, :] <= q[None, :, None]).astype(jnp.int32), axis=2), N_EXPERTS - 1)
    hit = e_q[:, :, None] == e_ids[None, None, :]
    pick = lambda tab: jnp.sum(jnp.where(hit, tab[:, None, :], 0), axis=2)
    within = (q[None, :] - pick(slab_end - slabs)) * SEG
    n_slabs = slab_end[:, -1]
    live = q[None, :] < n_slabs[:, None]
    slab_local = jnp.where(live, pick(seg_local) + within, 0).astype(jnp.int32)
    slab_global = jnp.where(live, pick(seg_global) + within, 0).astype(jnp.int32)
    return (padded.astype(jnp.int32), local.astype(jnp.int32), n_slabs.astype(jnp.int32),
            slab_local.reshape(nt, 1, MAX_SLABS), slab_global.reshape(nt, 1, MAX_SLABS))


def _disp_kernel(ns_ref, used_ref, sl_ref, sg_ref, local_ref, h2_ref, xs_hbm, xsort, zeros, sem):
    tm = TOKEN_TILE
    g = pl.program_id(0)
    nt = pl.num_programs(0)
    slot = g % 2

    m_iota = lax.broadcasted_iota(jnp.int32, (SORT_ROWS, tm), 0)
    place = jnp.zeros((SORT_ROWS, tm), F32)
    for k in range(TOP_K):
        place = jnp.where(m_iota == local_ref[k:k + 1, :], 1.0, place)
    xsort[slot] = _dot(place.astype(BF16), h2_ref[...].astype(BF16))

    def slab(q, s):
        return pltpu.make_async_copy(xsort.at[s, pl.ds(pl.multiple_of(sl_ref[0, 0, q], SEG), SEG)],
                                     xs_hbm.at[pl.ds(pl.multiple_of(sg_ref[0, 0, q], SEG), SEG)], sem.at[s])

    _for_each(ns_ref[g], lambda q: slab(q, slot).start())

    def drain(count, s):
        _for_each(count, lambda q: pltpu.make_async_copy(
            xsort.at[s, pl.ds(0, SEG)], xs_hbm.at[pl.ds(0, SEG)], sem.at[s]).wait())

    @pl.when(g > 0)
    def _():
        drain(ns_ref[g - 1], 1 - slot)

    @pl.when(g == nt - 1)
    def _():
        drain(ns_ref[g], slot)
        bt = EXPERT_TILE
        zeros[...] = jnp.zeros_like(zeros)
        used = used_ref[0]
        boundary = lax.shift_left(lax.shift_right_logical(used + (bt - 1), bt.bit_length() - 1), bt.bit_length() - 1)
        n_small = lax.shift_right_logical(boundary - used, SEG.bit_length() - 1)
        n_big = lax.shift_right_logical(xs_hbm.shape[0] - boundary, bt.bit_length() - 1)

        def small(q):
            row = pl.multiple_of(used + q * SEG, SEG)
            return pltpu.make_async_copy(zeros.at[pl.ds(0, SEG)], xs_hbm.at[pl.ds(row, SEG)], sem.at[2])

        def big(q):
            row = pl.multiple_of(boundary + q * bt, bt)
            return pltpu.make_async_copy(zeros, xs_hbm.at[pl.ds(row, bt)], sem.at[2])

        for piece, count in ((small, n_small), (big, n_big)):
            lax.fori_loop(0, count, lambda q, c, piece=piece: (piece(q).start(), c)[1], 0)
        for piece, count in ((small, n_small), (big, n_big)):
            lax.fori_loop(0, count, lambda q, c, piece=piece: (piece(q).wait(), c)[1], 0)


def _disp_call(n_slabs, used_rows, slab_local, slab_global, local, h2):
    n, d = h2.shape
    tm = TOKEN_TILE
    assert EXPERT_TILE & (EXPERT_TILE - 1) == 0 and SEG & (SEG - 1) == 0
    lists = pl.BlockSpec((1, 1, MAX_SLABS), lambda g, ns, used: (g, 0, 0), memory_space=pltpu.SMEM)
    grid_spec = pltpu.PrefetchScalarGridSpec(
        num_scalar_prefetch=2,
        grid=(n // tm,),
        in_specs=[lists, lists,
                  pl.BlockSpec((TOP_K, tm), lambda g, ns, used: (0, g)),
                  pl.BlockSpec((tm, d), lambda g, ns, used: (g, 0))],
        out_specs=pl.BlockSpec(memory_space=pl.ANY),
        scratch_shapes=[pltpu.VMEM((2, SORT_ROWS, d), F32), pltpu.VMEM((EXPERT_TILE, d), F32),
                        pltpu.SemaphoreType.DMA((3,))],
    )
    return pl.pallas_call(
        _disp_kernel,
        grid_spec=grid_spec,
        out_shape=jax.ShapeDtypeStruct((_sorted_rows(n), d), F32),
        compiler_params=pltpu.CompilerParams(dimension_semantics=("arbitrary",), vmem_limit_bytes=VMEM_LIMIT),
        name="disp",
    )(n_slabs, used_rows, slab_local, slab_global, local, h2)


def _moe_kernel(blk_ref, xblk_ref, exp_ref, lo_ref, hi_ref, first_ref, new_ref, nxt_ref, slot_ref,
                xs_ref, wup_hbm, bup_ref, wdn_hbm, bdn_ref, y_ref, wupf, wdnf, wupb, wdnp, wdnb, sem):
    t = EXPERT_TILE
    w = pl.program_id(0)
    lo = lo_ref[w]
    hi = hi_ref[w]
    half = wdnb.shape[0]
    d = wdnb.shape[1]

    def weight_copies(e, s):
        return (pltpu.make_async_copy(wup_hbm.at[e], wupf.at[s], sem.at[0, s]),
                pltpu.make_async_copy(wdn_hbm.at[e], wdnf.at[s], sem.at[1, s]))

    @pl.when(w == 0)
    def _():
        for c in weight_copies(exp_ref[0], 0):
            c.start(priority=1)

    @pl.when(new_ref[w] == 1)
    def _():
        s = slot_ref[w]
        for c in weight_copies(exp_ref[w], s):
            c.wait()

        @pl.when(nxt_ref[w] >= 0)
        def _():
            for c in weight_copies(nxt_ref[w], 1 - s):
                c.start(priority=1)

        wupb[...] = wupf[s].astype(BF16)
        for c in range(d // LANES):
            sl = slice(c * LANES, (c + 1) * LANES)
            wdnp[c, pl.ds(0, half // 2, stride=2), :] = wdnf[s, :half // 2, sl]
            wdnp[c, pl.ds(1, half // 2, stride=2), :] = wdnf[s, half // 2:, sl]
            wdnb[:, sl] = wdnp[c].astype(BF16)

    @pl.when(hi > lo)
    def _():
        x = xs_ref[...].astype(BF16)
        lane = lax.broadcasted_iota(jnp.int32, (1, LANES), 1)
        even = (lane % 2) == 0
        group = 2 * LANES

        def up(p):
            lo_cols = slice(p * group, (p + 1) * group)
            hi_cols = slice(half + p * group, half + (p + 1) * group)
            return (_dot(x, wupb[:, lo_cols]) + bup_ref[:, lo_cols], _dot(x, wupb[:, hi_cols]) + bup_ref[:, hi_cols])

        def activate(gu_a, gu_b):
            acts = []
            for c in range(group // LANES):
                a = gu_a[:, c * LANES:(c + 1) * LANES]
                b = gu_b[:, c * LANES:(c + 1) * LANES]
                x_glu = jnp.minimum(jnp.where(even, a, pltpu.roll(b, 1, 1)), SWIGLU_LIMIT)
                x_lin = jnp.clip(jnp.where(even, pltpu.roll(a, LANES - 1, 1), b), -SWIGLU_LIMIT, SWIGLU_LIMIT)
                acts.append((x_glu * jax.nn.sigmoid(SWIGLU_ALPHA * x_glu) * (x_lin + 1.0)).astype(BF16))
            return jnp.concatenate(acts, axis=1)

        rounds = half // group
        y = bdn_ref[...]
        pending = up(0)
        for p in range(rounds):
            nxt = up(p + 1) if p + 1 < rounds else None
            y = y + _dot(activate(*pending), wdnb[p * group:(p + 1) * group, :])
            pending = nxt
        row = lax.broadcasted_iota(jnp.int32, (t, 1), 0)
        mine = jnp.logical_and(row >= lo, row < hi)
        keep = jnp.where(first_ref[w] == 1, jnp.zeros_like(y), y_ref[...])
        y_ref[...] = jnp.where(mine, y, keep)

    @pl.when(jnp.logical_and(hi <= lo, first_ref[w] == 1))
    def _():
        y_ref[...] = jnp.zeros_like(y_ref)


def _moe_call(items, xs, wup, bup, wdn, bdn):
    t = EXPERT_TILE
    n_rows, d = xs.shape
    ff = wdn.shape[1]
    n_items = items[0].shape[0]
    by_blk = lambda w, blk, xblk, ex, *_: (blk[w], 0)
    by_xblk = lambda w, blk, xblk, ex, *_: (xblk[w], 0)
    by_exp = lambda w, blk, xblk, ex, *_: (ex[w], 0, 0)
    grid_spec = pltpu.PrefetchScalarGridSpec(
        num_scalar_prefetch=len(items),
        grid=(n_items,),
        in_specs=[pl.BlockSpec((t, d), by_xblk),
                  pl.BlockSpec(memory_space=pl.ANY),
                  pl.BlockSpec((None, 1, 2 * ff), by_exp),
                  pl.BlockSpec(memory_space=pl.ANY),
                  pl.BlockSpec((None, 1, d), by_exp)],
        out_specs=pl.BlockSpec((t, d), by_blk),
        scratch_shapes=[pltpu.VMEM((2, d, 2 * ff), F32),
                        pltpu.VMEM((2, ff, d), F32),
                        pltpu.VMEM((d, 2 * ff), BF16),
                        pltpu.VMEM((d // LANES, ff, LANES), F32),
                        pltpu.VMEM((ff, d), BF16),
                        pltpu.SemaphoreType.DMA((2, 2))],
    )
    return pl.pallas_call(
        _moe_kernel,
        grid_spec=grid_spec,
        out_shape=jax.ShapeDtypeStruct((n_rows, d), F32),
        compiler_params=pltpu.CompilerParams(dimension_semantics=("arbitrary",), vmem_limit_bytes=MOE_VMEM_LIMIT),
        name="moe",
    )(*items, xs, wup, bup, wdn, bdn)


def _moe_items(counts, n_rows):
    t = EXPERT_TILE
    nb = n_rows // t
    n_items = nb + N_EXPERTS - 1
    ends = jnp.cumsum(counts)
    starts = ends - counts
    first_blk = starts // t
    tiles = jnp.where(counts > 0, (ends - 1) // t - first_blk + 1, 0)
    item_end = jnp.cumsum(tiles)
    item_start = item_end - tiles
    total = item_end[-1]
    w = jnp.arange(n_items, dtype=jnp.int32)
    wc = jnp.minimum(w, total - 1)
    ex = jnp.sum((item_end[None, :] <= wc[:, None]).astype(jnp.int32), axis=1)
    onehot = (ex[:, None] == jnp.arange(N_EXPERTS, dtype=jnp.int32)[None, :]).astype(jnp.int32)
    pick = lambda v: jnp.sum(onehot * v[None, :], axis=1)
    xblk = pick(first_blk) + wc - pick(item_start)
    lo = jnp.clip(pick(starts) - xblk * t, 0, t)
    hi = jnp.where(w < total, jnp.clip(pick(ends) - xblk * t, 0, t), lo)
    used_blocks = (ends[-1] + t - 1) // t
    blk = jnp.where(w < total, xblk, jnp.minimum(used_blocks + w - total, nb - 1))
    first = jnp.concatenate([jnp.ones((1,), jnp.int32), (blk[1:] != blk[:-1]).astype(jnp.int32)])
    new = jnp.logical_and(w < total, w == pick(item_start)).astype(jnp.int32)
    e_ids = jnp.arange(N_EXPERTS, dtype=jnp.int32)
    later = jnp.logical_and(e_ids[None, :] > e_ids[:, None], counts[None, :] > 0)
    next_e = jnp.min(jnp.where(later, e_ids[None, :], N_EXPERTS), axis=1)
    next_e = jnp.where(next_e == N_EXPERTS, -1, next_e)
    ordinal = jnp.cumsum((counts > 0).astype(jnp.int32)) - 1
    as_i32 = lambda v: v.astype(jnp.int32)
    return tuple(map(as_i32, (blk, xblk, ex, lo, hi, first, new, pick(next_e), pick(ordinal) % 2)))


def _comb_kernel(ns_ref, sl_ref, sg_ref, nsl_ref, nsg_ref, local_ref, w_ref, y_hbm, x1_ref, o_ref, ysort, sem):
    tm = TOKEN_TILE
    g = pl.program_id(0)
    nt = pl.num_programs(0)
    slot = g % 2

    def slab(lref, gref, q, s):
        return pltpu.make_async_copy(y_hbm.at[pl.ds(pl.multiple_of(gref[0, 0, q], SEG), SEG)],
                                     ysort.at[s, pl.ds(pl.multiple_of(lref[0, 0, q], SEG), SEG)], sem.at[s])

    def fetch(lref, gref, count, s):
        _for_each(count, lambda q: slab(lref, gref, q, s).start())

    @pl.when(g == 0)
    def _():
        ysort[...] = jnp.zeros_like(ysort)
        fetch(sl_ref, sg_ref, ns_ref[0], 0)

    @pl.when(g + 1 < nt)
    def _():
        fetch(nsl_ref, nsg_ref, ns_ref[jnp.minimum(g + 1, nt - 1)], 1 - slot)

    _for_each(ns_ref[g], lambda q: pltpu.make_async_copy(
        y_hbm.at[pl.ds(0, SEG)], ysort.at[slot, pl.ds(0, SEG)], sem.at[slot]).wait())

    m_iota = lax.broadcasted_iota(jnp.int32, (tm, SORT_ROWS), 1)
    mix = jnp.zeros((tm, SORT_ROWS), F32)
    for k in range(TOP_K):
        mix = jnp.where(m_iota == local_ref[:, k:k + 1], w_ref[:, k:k + 1], mix)
    mix = mix.astype(BF16)
    o_ref[...] = x1_ref[...] + _dot(mix, ysort[slot].astype(BF16))


def _comb_call(n_slabs, slab_local, slab_global, local_t, w_t, y, x1):
    n, d = x1.shape
    tm = TOKEN_TILE
    nt = n // tm
    cur = lambda g, ns: (g, 0, 0)
    nxt = lambda g, ns: (jnp.minimum(g + 1, nt - 1), 0, 0)
    lists = lambda index_map: pl.BlockSpec((1, 1, MAX_SLABS), index_map, memory_space=pltpu.SMEM)
    grid_spec = pltpu.PrefetchScalarGridSpec(
        num_scalar_prefetch=1,
        grid=(nt,),
        in_specs=[lists(cur), lists(cur), lists(nxt), lists(nxt),
                  pl.BlockSpec((tm, TOP_K), lambda g, ns: (g, 0)),
                  pl.BlockSpec((tm, TOP_K), lambda g, ns: (g, 0)),
                  pl.BlockSpec(memory_space=pl.ANY),
                  pl.BlockSpec((tm, d), lambda g, ns: (g, 0))],
        out_specs=pl.BlockSpec((tm, d), lambda g, ns: (g, 0)),
        scratch_shapes=[pltpu.VMEM((2, SORT_ROWS, d), F32), pltpu.SemaphoreType.DMA((2,))],
    )
    return pl.pallas_call(
        _comb_kernel,
        grid_spec=grid_spec,
        out_shape=jax.ShapeDtypeStruct((n, d), F32),
        compiler_params=pltpu.CompilerParams(dimension_semantics=("arbitrary",), vmem_limit_bytes=VMEM_LIMIT),
        name="comb",
    )(n_slabs, slab_local, slab_global, slab_local, slab_global, local_t, w_t, y, x1)


def _layer(x2, pos2, bsz, seq, g_mix, w_in, g_cq, w_uq, g_ckv, w_ukv, g_qnorm, g_knorm, w_o_sb, w_o_mla, w_out,
           g_ffn, w_router, b_router, w_up, b_up, w_down, b_down):
    n, d = x2.shape
    c_sb = 3 * SB_WIDTH
    c_q = c_sb + MLA_Q_LORA
    c_kv = c_q + MLA_KV_LORA
    c_kr = c_kv + MLA_ROPE
    w1 = w_in[:, :2 * SB_WIDTH].astype(BF16)
    w1v = w_in[:, 2 * SB_WIDTH:c_sb].T.astype(BF16)
    w_kr = jnp.pad(w_in[:, c_kv:c_kr], ((0, 0), (MLA_NOPE, HEAD_PAD - MLA_QK)))
    w2 = jnp.concatenate([w_in[:, c_sb:c_kv], w_kr], axis=1).astype(BF16)
    w3 = w_in[:, c_kr:].astype(BF16)
    wuq = jnp.pad(w_uq.reshape(MLA_Q_LORA, MLA_HEADS, MLA_QK), ((0, 0), (0, 0), (0, HEAD_PAD - MLA_QK)))
    wuq = wuq.reshape(MLA_Q_LORA, MLA_HEADS * HEAD_PAD).astype(BF16)
    wukv = w_ukv.reshape(MLA_KV_LORA, MLA_HEADS, MLA_NOPE + MLA_V)
    wuk = jnp.pad(wukv[:, :, :MLA_NOPE], ((0, 0), (0, 0), (0, HEAD_PAD - MLA_NOPE)))
    wuk = wuk.reshape(MLA_KV_LORA, MLA_HEADS * HEAD_PAD).astype(BF16)
    wuv = wukv[:, :, MLA_NOPE:].reshape(MLA_KV_LORA, MLA_HEADS * MLA_V).T.astype(BF16)
    gq = jnp.tile(jnp.pad(g_qnorm, (0, HEAD_PAD - MLA_QK)), MLA_HEADS)[None, :]
    gk = jnp.tile(jnp.pad(g_knorm, (0, HEAD_PAD - MLA_QK)), MLA_HEADS)[None, :]
    half = MLA_ROPE // 2
    inv_freq = ROPE_THETA ** (-jnp.arange(half, dtype=F32) / half)
    invf = jnp.pad(jnp.concatenate([inv_freq, inv_freq]), (MLA_NOPE, HEAD_PAD - MLA_QK))[None, :]
    sgn = jnp.pad(jnp.concatenate([-jnp.ones((half,), F32), jnp.ones((half,), F32)]),
                  (MLA_NOPE, HEAD_PAD - MLA_QK))[None, :]

    sb, sbv, qm, km, vm, gates = _pre_call(x2, pos2, bsz, seq, g_mix[None, :], w1, w1v, w2, w3, g_cq[None, :],
                                           g_ckv[None, :], wuq, wuk, wuv, gq, gk, invf, sgn)
    o_sb = _sb_call(sb, sbv, bsz, seq)
    o_mla = _mla_call(qm, km, vm, bsz, seq)

    wr_t = w_router.T
    wr_hi = wr_t.astype(BF16)
    wr_lo = (wr_t - wr_hi.astype(F32)).astype(BF16)
    x1, h2, top_w, top_e, pos = _post_call(
        o_sb, o_mla, gates, x2, w_o_sb.astype(BF16), w_o_mla.astype(BF16), w_out.astype(BF16), g_ffn[None, :],
        wr_hi, wr_lo, b_router[:, None])

    padded, local, n_slabs, slab_local, slab_global = _route(top_e, pos, n)
    xs = _disp_call(n_slabs, jnp.sum(padded)[None], slab_local, slab_global, local, h2)
    y = _moe_call(_moe_items(padded, xs.shape[0]), xs, w_up, b_up[:, None, :], w_down, b_down[:, None, :])
    return _comb_call(n_slabs, slab_local, slab_global, local.T, top_w.T, y, x1)


def kernel(x, positions, g_mix, w_in, g_cq, w_uq, g_ckv, w_ukv, g_qnorm, g_knorm, w_o_sb, w_o_mla, w_out, g_ffn,
           w_router, b_router, w_up, b_up, w_down, b_down):
    bsz, seq, d = x.shape
    x2 = x.reshape(bsz * seq, d)
    pos2 = positions.reshape(bsz * seq, 1)
    for l in range(g_mix.shape[0]):
        x2 = _layer(x2, pos2, bsz, seq, g_mix[l], w_in[l], g_cq[l], w_uq[l], g_ckv[l], w_ukv[l], g_qnorm[l],
                    g_knorm[l], w_o_sb[l], w_o_mla[l], w_out[l], g_ffn[l], w_router[l], b_router[l], w_up[l],
                    b_up[l], w_down[l], b_down[l])
    return x2.reshape(bsz, seq, d)
```

```python
import math

import jax
import jax.numpy as jnp
from jax import lax
from jax.experimental import pallas as pl
from jax.experimental.pallas import tpu as pltpu

F32 = jnp.float32
BF16 = jnp.bfloat16

EPS = 1e-6
CHUNK = 64
SB_HEADS = 8
SB_DIM = 64
SB_WIDTH = SB_HEADS * SB_DIM
MLA_HEADS = 8
MLA_NOPE = 64
MLA_ROPE = 32
MLA_QK = MLA_NOPE + MLA_ROPE
MLA_V = 64
MLA_Q_LORA = 384
MLA_KV_LORA = 256
ROPE_THETA = 10000.0
N_EXPERTS = 32
TOP_K = 4
SWIGLU_LIMIT = 7.0
SWIGLU_ALPHA = 1.702

LANES = 128
HEAD_PAD = LANES
TOKEN_TILE = 256
PRE_TILE = 512
POST_TILE = 512
ATTN_TILE = 256
EXPERT_TILE = 256
SB_SKEW = 1
LOG2E = math.log2(math.e)
SB_DEAD = -160.0
VMEM_LIMIT = 48 * 1024 * 1024
MOE_VMEM_LIMIT = 56 * 1024 * 1024


def _nt_dot(a, b):
    return lax.dot_general(a, b, (((1,), (1,)), ((), ())), preferred_element_type=F32)


def _dot(a, b):
    return jnp.dot(a, b, preferred_element_type=F32)


def _rms(t, g):
    return t * lax.rsqrt(jnp.mean(t * t, axis=-1, keepdims=True) + EPS) * g


def _pre_kernel(x_ref, pos_ref, gmix_ref, w1_ref, w1v_ref, w2_ref, w3_ref, gcq_ref, gckv_ref, wuq_ref, wuk_ref,
                wuv_ref, gq_ref, gk_ref, invf_ref, sgn_ref,
                sb_ref, sbv_ref, qm_ref, km_ref, vm_ref, gate_ref):
    h = _rms(x_ref[...], gmix_ref[...]).astype(BF16)

    lat = _dot(h, w2_ref[...])
    cqn = _rms(lat[:, :MLA_Q_LORA], gcq_ref[...]).astype(BF16)
    ckvn = _rms(lat[:, MLA_Q_LORA:MLA_Q_LORA + MLA_KV_LORA], gckv_ref[...]).astype(BF16)
    kr = lat[:, MLA_Q_LORA + MLA_KV_LORA:]
    qf = _dot(cqn, wuq_ref[...])
    kf = _dot(ckvn, wuk_ref[...])
    key_tiles = [slice(j * ATTN_TILE, (j + 1) * ATTN_TILE) for j in range(PRE_TILE // ATTN_TILE)]
    for j, rows in enumerate(key_tiles):
        vm_ref[j] = _nt_dot(wuv_ref[...], ckvn[rows, :]).astype(BF16)

    sb_ref[:, :SB_WIDTH] = (_dot(h, w1_ref[:, :SB_WIDTH]) * (LOG2E / math.sqrt(SB_DIM))).astype(BF16)
    sb_ref[:, SB_WIDTH:] = _dot(h, w1_ref[:, SB_WIDTH:]).astype(BF16)
    for j, rows in enumerate(key_tiles):
        sbv_ref[j] = _nt_dot(w1v_ref[...], h[rows, :]).astype(BF16)

    gate_ref[...] = jax.nn.sigmoid(_dot(h, w3_ref[...])).astype(BF16)

    ang = pos_ref[...].astype(F32) * invf_ref[...]
    cos = jnp.cos(ang)
    sin = jnp.sin(ang) * sgn_ref[...]
    lane = lax.broadcasted_iota(jnp.int32, (1, LANES), 1)
    first_half = lane < MLA_NOPE + MLA_ROPE // 2

    def rope(t):
        partner = jnp.where(first_half, pltpu.roll(t, LANES - MLA_ROPE // 2, 1), pltpu.roll(t, MLA_ROPE // 2, 1))
        return t * cos + partner * sin

    def head_norm(t, g):
        return t * lax.rsqrt(jnp.sum(t * t, axis=-1, keepdims=True) * (1.0 / MLA_QK) + EPS) * g

    mla_scale = LOG2E / math.sqrt(MLA_QK)
    for hd in range(MLA_HEADS):
        sl = slice(hd * HEAD_PAD, (hd + 1) * HEAD_PAD)
        qm_ref[:, sl] = (rope(head_norm(qf[:, sl], gq_ref[:, sl])) * mla_scale).astype(BF16)
        km_ref[:, sl] = rope(head_norm(kf[:, sl] + kr, gk_ref[:, sl])).astype(BF16)


def _pre_call(x2, pos2, bsz, seq, gmix, w1, w1v, w2, w3, gcq, gckv, wuq, wuk, wuv, gq, gk, invf, sgn):
    n, d = x2.shape
    tm = PRE_TILE
    ta = ATTN_TILE
    assert tm % ta == 0 and seq % tm == 0
    nt = seq // ta
    steps = seq // tm
    const = lambda i: (0, 0)
    full = lambda a: pl.BlockSpec(a.shape, const)
    row = lambda w: pl.BlockSpec((tm, w), lambda i: (i, 0))
    vt_spec = lambda w: pl.BlockSpec((None, tm // ta, w, ta), lambda i: (i // steps, i % steps, 0, 0))
    return pl.pallas_call(
        _pre_kernel,
        grid=(n // tm,),
        in_specs=[row(d), row(1), full(gmix), full(w1), full(w1v), full(w2), full(w3), full(gcq), full(gckv),
                  full(wuq), full(wuk), full(wuv), full(gq), full(gk), full(invf), full(sgn)],
        out_specs=[row(2 * SB_WIDTH), vt_spec(SB_WIDTH), row(MLA_HEADS * HEAD_PAD), row(MLA_HEADS * HEAD_PAD),
                   vt_spec(MLA_HEADS * MLA_V), row(2 * d)],
        out_shape=[jax.ShapeDtypeStruct((n, 2 * SB_WIDTH), BF16),
                   jax.ShapeDtypeStruct((bsz, nt, SB_WIDTH, ta), BF16),
                   jax.ShapeDtypeStruct((n, MLA_HEADS * HEAD_PAD), BF16),
                   jax.ShapeDtypeStruct((n, MLA_HEADS * HEAD_PAD), BF16),
                   jax.ShapeDtypeStruct((bsz, nt, MLA_HEADS * MLA_V, ta), BF16),
                   jax.ShapeDtypeStruct((n, 2 * d), BF16)],
        compiler_params=pltpu.CompilerParams(dimension_semantics=("arbitrary",), vmem_limit_bytes=VMEM_LIMIT),
        name="pre",
    )(x2, pos2, gmix, w1, w1v, w2, w3, gcq, gckv, wuq, wuk, wuv, gq, gk, invf, sgn)


def _sb_kernel(q_ref, k_ref, v_ref, o_ref, acc_ref, car_ref, za_ref, zb_ref):
    t = ATTN_TILE
    i = pl.program_id(1)
    lane = lax.broadcasted_iota(jnp.int32, (1, LANES), 1)
    lo = lane < SB_DIM
    rr = lax.broadcasted_iota(jnp.int32, (t, t), 0)
    cc = lax.broadcasted_iota(jnp.int32, (t, t), 1)
    later_sum = (cc > rr).astype(BF16)
    strict = rr < cc

    acc_ref[...] = jnp.zeros_like(acc_ref)
    car_ref[...] = jnp.zeros_like(car_ref)

    def produce(dst, j, hd):
        krows = pl.ds(pl.multiple_of(jnp.maximum(j, 0) * t, t), t)
        pair = slice((hd // 2) * LANES, (hd // 2 + 1) * LANES)
        q2 = q_ref[:, pair]
        zero = jnp.zeros_like(q2)
        qh = jnp.where(lo, q2, zero) if hd % 2 == 0 else jnp.where(lo, zero, q2)
        dst[hd] = _nt_dot(k_ref[krows, pair], qh)

    def step(src, dst, j, masked):
        def stay(z):
            neg_abs = pltpu.bitcast(pltpu.bitcast(z, jnp.uint32) | jnp.uint32(0x80000000), F32)
            sp = jnp.log2(1.0 + jnp.exp2(neg_abs))
            log_beta = jnp.minimum(z, 0.0) - sp
            log_stay = log_beta - z
            if masked:
                log_stay = jnp.where(strict, log_stay, 0.0)
            later = _dot(later_sum, log_stay.astype(BF16))
            return log_beta, later, jnp.sum(log_stay, axis=0, keepdims=True)

        def weigh(hd, log_beta, later, total):
            carry = car_ref[hd]
            a = jnp.exp2(log_beta + later + carry)
            if masked:
                a = jnp.where(strict, a, 0.0)
            rows = slice(hd * SB_DIM, (hd + 1) * SB_DIM)
            acc_ref[rows, :] += _dot(v_ref[j, rows, :], a.astype(BF16))
            car_ref[hd] = carry + total

        mids = {}
        for s in range(SB_HEADS + SB_SKEW):
            if s < SB_HEADS:
                mids[s] = stay(src[s])
                produce(dst, j - 1, s)
            hd = s - SB_SKEW
            if 0 <= hd < SB_HEADS:
                weigh(hd, *mids.pop(hd))

    for hd in range(SB_HEADS):
        produce(za_ref, i, hd)
    step(za_ref, zb_ref, i, True)

    def cond(c):
        j, alive = c
        return jnp.logical_and(j >= 0, alive > SB_DEAD)

    def body(c):
        j, _ = c
        step(zb_ref, za_ref, j, False)

        @pl.when(jnp.logical_and(j >= 1, jnp.max(car_ref[...]) > SB_DEAD))
        def _():
            step(za_ref, zb_ref, j - 1, False)

        return j - 2, jnp.max(car_ref[...])

    lax.while_loop(cond, body, (i - 1, jnp.max(car_ref[...])))
    o_ref[...] = acc_ref[...].astype(o_ref.dtype)


def _sb_call(sb, sbv, bsz, seq):
    t = ATTN_TILE
    nq = seq // t
    return pl.pallas_call(
        _sb_kernel,
        grid=(bsz, nq),
        in_specs=[pl.BlockSpec((t, SB_WIDTH), lambda b, i: (b * nq + i, 0)),
                  pl.BlockSpec((seq, SB_WIDTH), lambda b, i: (b, 1)),
                  pl.BlockSpec((None, nq, SB_WIDTH, t), lambda b, i: (b, 0, 0, 0))],
        out_specs=pl.BlockSpec((None, SB_WIDTH, t), lambda b, i: (b, 0, i)),
        out_shape=jax.ShapeDtypeStruct((bsz, SB_WIDTH, seq), BF16),
        scratch_shapes=[pltpu.VMEM((SB_WIDTH, t), F32), pltpu.VMEM((SB_HEADS, 1, t), F32),
                        pltpu.VMEM((SB_HEADS, t, t), F32), pltpu.VMEM((SB_HEADS, t, t), F32)],
        compiler_params=pltpu.CompilerParams(dimension_semantics=("arbitrary",) * 2, vmem_limit_bytes=VMEM_LIMIT),
        name="sb_attn",
    )(sb, sb, sbv)


def _mla_kernel(q_ref, k_ref, v_ref, o_ref, m_ref, l_ref, acc_ref, sa_ref, sb_ref):
    t = ATTN_TILE
    i = pl.program_id(1)
    rr = lax.broadcasted_iota(jnp.int32, (t, t), 0)
    cc = lax.broadcasted_iota(jnp.int32, (t, t), 1)
    visible = (rr // CHUNK) <= (cc // CHUNK)

    m_ref[...] = jnp.full_like(m_ref, -jnp.inf)
    l_ref[...] = jnp.zeros_like(l_ref)
    acc_ref[...] = jnp.zeros_like(acc_ref)

    def produce(dst, j, heads=range(MLA_HEADS)):
        krows = pl.ds(pl.multiple_of(j * t, t), t)
        for hd in heads:
            sl = slice(hd * HEAD_PAD, (hd + 1) * HEAD_PAD)
            dst[hd] = _nt_dot(k_ref[krows, sl], q_ref[:, sl])

    def consume(src, j, masked, heads=range(MLA_HEADS)):
        for hd in heads:
            s = src[hd]
            if masked:
                s = jnp.where(visible, s, -jnp.inf)
            m_old = m_ref[hd]
            m_new = jnp.maximum(m_old, jnp.max(s, axis=0, keepdims=True))
            alpha = jnp.exp2(m_old - m_new)
            p = jnp.exp2(s - m_new)
            l_ref[hd] = alpha * l_ref[hd] + jnp.sum(p, axis=0, keepdims=True)
            rows = slice(hd * MLA_V, (hd + 1) * MLA_V)
            acc_ref[rows, :] = alpha * acc_ref[rows, :] + _dot(v_ref[j, rows, :], p.astype(BF16))
            m_ref[hd] = m_new

    def step(src, dst, j):
        for hd in range(MLA_HEADS):
            consume(src, j, False, [hd])
            produce(dst, j + 1, [hd])

    produce(sa_ref, 0)

    def pair(jj, c):
        step(sa_ref, sb_ref, 2 * jj)
        step(sb_ref, sa_ref, 2 * jj + 1)
        return c

    lax.fori_loop(0, i // 2, pair, 0)
    odd = i % 2

    @pl.when(odd == 1)
    def _():
        step(sa_ref, sb_ref, i - 1)
        consume(sb_ref, i, True)

    @pl.when(odd == 0)
    def _():
        consume(sa_ref, i, True)

    for hd in range(MLA_HEADS):
        rows = slice(hd * MLA_V, (hd + 1) * MLA_V)
        o_ref[rows, :] = (acc_ref[rows, :] / l_ref[hd]).astype(o_ref.dtype)


def _mla_call(qm, km, vm, bsz, seq):
    t = ATTN_TILE
    nq = seq // t
    width = MLA_HEADS * HEAD_PAD
    return pl.pallas_call(
        _mla_kernel,
        grid=(bsz, nq),
        in_specs=[pl.BlockSpec((t, width), lambda b, i: (b * nq + i, 0)),
                  pl.BlockSpec((seq, width), lambda b, i: (b, 0)),
                  pl.BlockSpec((None, nq, MLA_HEADS * MLA_V, t), lambda b, i: (b, 0, 0, 0))],
        out_specs=pl.BlockSpec((None, MLA_HEADS * MLA_V, t), lambda b, i: (b, 0, i)),
        out_shape=jax.ShapeDtypeStruct((bsz, MLA_HEADS * MLA_V, seq), BF16),
        scratch_shapes=[pltpu.VMEM((MLA_HEADS, 1, t), F32), pltpu.VMEM((MLA_HEADS, 1, t), F32),
                        pltpu.VMEM((MLA_HEADS * MLA_V, t), F32),
                        pltpu.VMEM((MLA_HEADS, t, t), F32), pltpu.VMEM((MLA_HEADS, t, t), F32)],
        compiler_params=pltpu.CompilerParams(dimension_semantics=("arbitrary",) * 2, vmem_limit_bytes=VMEM_LIMIT),
        name="mla_attn",
    )(qm, km, vm)


def _post_kernel(osb_ref, omla_ref, gate_ref, x_ref, wosb_ref, womla_ref, wout_ref, gffn_ref, wrh_ref, wrl_ref,
                 br_ref, x1_ref, h2_ref, topw_ref, tope_ref, pos_ref, run_ref):
    tm = TOKEN_TILE
    d = x_ref.shape[1]

    @pl.when(pl.program_id(0) == 0)
    def _():
        run_ref[...] = jnp.zeros_like(run_ref)

    def project(cols):
        tn = (((0,), (0,)), ((), ()))
        a = lax.dot_general(osb_ref[:, cols], wosb_ref[...], tn, preferred_element_type=F32)
        b = lax.dot_general(omla_ref[:, cols], womla_ref[...], tn, preferred_element_type=F32)
        mixed = gate_ref[cols, :d].astype(F32) * a + gate_ref[cols, d:].astype(F32) * b
        x1 = x_ref[cols, :] + _dot(mixed.astype(BF16), wout_ref[...])
        x1_ref[cols, :] = x1
        return x1

    def route(cols, x1):
        h2 = _rms(x1, gffn_ref[...])
        h2_ref[cols, :] = h2

        h_hi = h2.astype(BF16)
        h_lo = (h2 - h_hi.astype(F32)).astype(BF16)
        logits = (_nt_dot(wrh_ref[...], h_hi) + _nt_dot(wrh_ref[...], h_lo) + _nt_dot(wrl_ref[...], h_hi)
                  + br_ref[...])

        eidx = lax.broadcasted_iota(jnp.int32, (N_EXPERTS, tm), 0)
        work = logits
        tops, sels = [], []
        for _ in range(TOP_K):
            mk = jnp.max(work, axis=0, keepdims=True)
            ik = jnp.min(jnp.where(work == mk, eidx, N_EXPERTS), axis=0, keepdims=True)
            sel = eidx == ik
            work = jnp.where(sel, -jnp.inf, work)
            tops.append(mk)
            sels.append(sel)
        exps = [jnp.exp(mk - tops[0]) for mk in tops]
        denom = exps[0] + exps[1] + exps[2] + exps[3]

        chosen = jnp.zeros((N_EXPERTS, tm), F32)
        for sel in sels:
            chosen = chosen + sel.astype(F32)
        rr = lax.broadcasted_iota(jnp.int32, (tm, tm), 0)
        cc = lax.broadcasted_iota(jnp.int32, (tm, tm), 1)
        before = (rr < cc).astype(BF16)
        rank = _dot(chosen.astype(BF16), before) + run_ref[...]
        for k in range(TOP_K):
            topw_ref[k:k + 1, cols] = exps[k] / denom
            tope_ref[k:k + 1, cols] = jnp.sum(jnp.where(sels[k], eidx, 0), axis=0, keepdims=True)
            pos_ref[k:k + 1, cols] = jnp.sum(jnp.where(sels[k], rank, 0.0), axis=0, keepdims=True).astype(jnp.int32)
        run_ref[...] = run_ref[...] + jnp.sum(chosen, axis=1, keepdims=True)

    subs = [slice(j * tm, (j + 1) * tm) for j in range(POST_TILE // tm)]
    x1s = {}
    for j in range(len(subs) + 1):
        if j < len(subs):
            x1s[j] = project(subs[j])
        if j >= 1:
            route(subs[j - 1], x1s.pop(j - 1))


def _post_call(osb, omla, gates, x2, wosb, womla, wout, gffn, wrh, wrl, br):
    n, d = x2.shape
    tm = POST_TILE
    nt = osb.shape[2] // tm
    const = lambda i: (0, 0)
    full = lambda a: pl.BlockSpec(a.shape, const)
    row = lambda w: pl.BlockSpec((tm, w), lambda i: (i, 0))
    col = lambda: pl.BlockSpec((TOP_K, tm), lambda i: (0, i))
    feat = lambda a: pl.BlockSpec((None, a.shape[1], tm), lambda i: (i // nt, 0, i % nt))
    return pl.pallas_call(
        _post_kernel,
        grid=(n // tm,),
        in_specs=[feat(osb), feat(omla), row(2 * d), row(d), full(wosb), full(womla), full(wout),
                  full(gffn), full(wrh), full(wrl), full(br)],
        out_specs=[row(d), row(d), col(), col(), col()],
        out_shape=[jax.ShapeDtypeStruct((n, d), F32), jax.ShapeDtypeStruct((n, d), F32),
                   jax.ShapeDtypeStruct((TOP_K, n), F32), jax.ShapeDtypeStruct((TOP_K, n), jnp.int32),
                   jax.ShapeDtypeStruct((TOP_K, n), jnp.int32)],
        scratch_shapes=[pltpu.VMEM((N_EXPERTS, 1), F32)],
        compiler_params=pltpu.CompilerParams(dimension_semantics=("arbitrary",), vmem_limit_bytes=VMEM_LIMIT),
        name="post",
    )(osb, omla, gates, x2, wosb, womla, wout, gffn, wrh, wrl, br)


SEG = 8
MAX_SLABS = TOKEN_TILE * TOP_K // SEG + N_EXPERTS
SORT_ROWS = MAX_SLABS * SEG
SORT_CHUNK = 256


def _for_each(count, fn, unroll=4):
    shift = unroll.bit_length() - 1
    assert unroll == 1 << shift
    groups = lax.shift_right_logical(count, shift)

    def group(i, c):
        for u in range(unroll):
            fn(i * unroll + u)
        return c
    lax.fori_loop(0, groups, group, 0)

    def single(q, c):
        fn(q)
        return c
    lax.fori_loop(lax.shift_left(groups, shift), count, single, 0)


def _sorted_rows(n):
    nt = n // TOKEN_TILE
    return -(-(n * TOP_K + nt * N_EXPERTS * (SEG - 1)) // EXPERT_TILE) * EXPERT_TILE


def _route(top_e, pos, n):
    tm = TOKEN_TILE
    nt = n // tm
    e_ids = jnp.arange(N_EXPERTS, dtype=jnp.int32)
    onehot = top_e[:, :, None] == e_ids[None, None, :]
    c_tile = jnp.sum(onehot.reshape(TOP_K, nt, tm, N_EXPERTS).astype(jnp.int32), axis=(0, 2))
    slabs = (c_tile + SEG - 1) // SEG
    rank_base = jnp.cumsum(c_tile, axis=0) - c_tile
    row_base = (jnp.cumsum(slabs, axis=0) - slabs) * SEG
    padded = jnp.sum(slabs, axis=0) * SEG
    starts = jnp.cumsum(padded) - padded
    seg_global = starts[None, :] + row_base
    slab_end = jnp.cumsum(slabs, axis=1)
    seg_local = (slab_end - slabs) * SEG

    per_token = lambda tab: jnp.broadcast_to(tab[:, None, :], (nt, tm, N_EXPERTS)).reshape(n, N_EXPERTS)
    local = jnp.sum(jnp.where(onehot, per_token(seg_local - rank_base)[None], 0), axis=2) + pos

    q = jnp.arange(MAX_SLABS, dtype=jnp.int32)
    e_q = jnp.minimum(jnp.sum((slab_end[:, None, :] <= q[None, :, None]).astype(jnp.int32), axis=2), N_EXPERTS - 1)
    hit = e_q[:, :, None] == e_ids[None, None, :]
    pick = lambda tab: jnp.sum(jnp.where(hit, tab[:, None, :], 0), axis=2)
    within = (q[None, :] - pick(slab_end - slabs)) * SEG
    n_slabs = slab_end[:, -1]
    live = q[None, :] < n_slabs[:, None]
    slab_local = jnp.where(live, pick(seg_local) + within, 0).astype(jnp.int32)
    slab_global = jnp.where(live, pick(seg_global) + within, 0).astype(jnp.int32)
    return (padded.astype(jnp.int32), local.astype(jnp.int32), n_slabs.astype(jnp.int32),
            slab_local.reshape(nt, 1, MAX_SLABS), slab_global.reshape(nt, 1, MAX_SLABS))


def _disp_kernel(ns_ref, used_ref, sl_ref, sg_ref, local_ref, h2_ref, xs_hbm, xsort, zeros, sem):
    tm = TOKEN_TILE
    g = pl.program_id(0)
    nt = pl.num_programs(0)
    slot = g % 2

    m_iota = lax.broadcasted_iota(jnp.int32, (SORT_ROWS, tm), 0)
    place = jnp.zeros((SORT_ROWS, tm), F32)
    for k in range(TOP_K):
        place = jnp.where(m_iota == local_ref[k:k + 1, :], 1.0, place)
    xsort[slot] = _dot(place.astype(BF16), h2_ref[...].astype(BF16))

    def slab(q, s):
        return pltpu.make_async_copy(xsort.at[s, pl.ds(pl.multiple_of(sl_ref[0, 0, q], SEG), SEG)],
                                     xs_hbm.at[pl.ds(pl.multiple_of(sg_ref[0, 0, q], SEG), SEG)], sem.at[s])

    _for_each(ns_ref[g], lambda q: slab(q, slot).start())

    def drain(count, s):
        _for_each(count, lambda q: pltpu.make_async_copy(
            xsort.at[s, pl.ds(0, SEG)], xs_hbm.at[pl.ds(0, SEG)], sem.at[s]).wait())

    @pl.when(g > 0)
    def _():
        drain(ns_ref[g - 1], 1 - slot)

    @pl.when(g == nt - 1)
    def _():
        drain(ns_ref[g], slot)
        bt = EXPERT_TILE
        zeros[...] = jnp.zeros_like(zeros)
        used = used_ref[0]
        boundary = lax.shift_left(lax.shift_right_logical(used + (bt - 1), bt.bit_length() - 1), bt.bit_length() - 1)
        n_small = lax.shift_right_logical(boundary - used, SEG.bit_length() - 1)
        n_big = lax.shift_right_logical(xs_hbm.shape[0] - boundary, bt.bit_length() - 1)

        def small(q):
            row = pl.multiple_of(used + q * SEG, SEG)
            return pltpu.make_async_copy(zeros.at[pl.ds(0, SEG)], xs_hbm.at[pl.ds(row, SEG)], sem.at[2])

        def big(q):
            row = pl.multiple_of(boundary + q * bt, bt)
            return pltpu.make_async_copy(zeros, xs_hbm.at[pl.ds(row, bt)], sem.at[2])

        for piece, count in ((small, n_small), (big, n_big)):
            lax.fori_loop(0, count, lambda q, c, piece=piece: (piece(q).start(), c)[1], 0)
        for piece, count in ((small, n_small), (big, n_big)):
            lax.fori_loop(0, count, lambda q, c, piece=piece: (piece(q).wait(), c)[1], 0)


def _disp_call(n_slabs, used_rows, slab_local, slab_global, local, h2):
    n, d = h2.shape
    tm = TOKEN_TILE
    assert EXPERT_TILE & (EXPERT_TILE - 1) == 0 and SEG & (SEG - 1) == 0
    lists = pl.BlockSpec((1, 1, MAX_SLABS), lambda g, ns, used: (g, 0, 0), memory_space=pltpu.SMEM)
    grid_spec = pltpu.PrefetchScalarGridSpec(
        num_scalar_prefetch=2,
        grid=(n // tm,),
        in_specs=[lists, lists,
                  pl.BlockSpec((TOP_K, tm), lambda g, ns, used: (0, g)),
                  pl.BlockSpec((tm, d), lambda g, ns, used: (g, 0))],
        out_specs=pl.BlockSpec(memory_space=pl.ANY),
        scratch_shapes=[pltpu.VMEM((2, SORT_ROWS, d), F32), pltpu.VMEM((EXPERT_TILE, d), F32),
                        pltpu.SemaphoreType.DMA((3,))],
    )
    return pl.pallas_call(
        _disp_kernel,
        grid_spec=grid_spec,
        out_shape=jax.ShapeDtypeStruct((_sorted_rows(n), d), F32),
        compiler_params=pltpu.CompilerParams(dimension_semantics=("arbitrary",), vmem_limit_bytes=VMEM_LIMIT),
        name="disp",
    )(n_slabs, used_rows, slab_local, slab_global, local, h2)


def _moe_kernel(blk_ref, xblk_ref, exp_ref, lo_ref, hi_ref, first_ref, new_ref, nxt_ref, slot_ref,
                xs_ref, wup_hbm, bup_ref, wdn_hbm, bdn_ref, y_ref, wupf, wdnf, wupb, wdnp, wdnb, sem):
    t = EXPERT_TILE
    w = pl.program_id(0)
    lo = lo_ref[w]
    hi = hi_ref[w]
    half = wdnb.shape[0]
    d = wdnb.shape[1]

    def weight_copies(e, s):
        return (pltpu.make_async_copy(wup_hbm.at[e], wupf.at[s], sem.at[0, s]),
                pltpu.make_async_copy(wdn_hbm.at[e], wdnf.at[s], sem.at[1, s]))

    @pl.when(w == 0)
    def _():
        for c in weight_copies(exp_ref[0], 0):
            c.start(priority=1)

    @pl.when(new_ref[w] == 1)
    def _():
        s = slot_ref[w]
        for c in weight_copies(exp_ref[w], s):
            c.wait()

        @pl.when(nxt_ref[w] >= 0)
        def _():
            for c in weight_copies(nxt_ref[w], 1 - s):
                c.start(priority=1)

        wupb[...] = wupf[s].astype(BF16)
        for c in range(d // LANES):
            sl = slice(c * LANES, (c + 1) * LANES)
            wdnp[c, pl.ds(0, half // 2, stride=2), :] = wdnf[s, :half // 2, sl]
            wdnp[c, pl.ds(1, half // 2, stride=2), :] = wdnf[s, half // 2:, sl]
            wdnb[:, sl] = wdnp[c].astype(BF16)

    @pl.when(hi > lo)
    def _():
        x = xs_ref[...].astype(BF16)
        lane = lax.broadcasted_iota(jnp.int32, (1, LANES), 1)
        even = (lane % 2) == 0
        group = 2 * LANES

        def up(p):
            lo_cols = slice(p * group, (p + 1) * group)
            hi_cols = slice(half + p * group, half + (p + 1) * group)
            return (_dot(x, wupb[:, lo_cols]) + bup_ref[:, lo_cols], _dot(x, wupb[:, hi_cols]) + bup_ref[:, hi_cols])

        def activate(gu_a, gu_b):
            acts = []
            for c in range(group // LANES):
                a = gu_a[:, c * LANES:(c + 1) * LANES]
                b = gu_b[:, c * LANES:(c + 1) * LANES]
                x_glu = jnp.minimum(jnp.where(even, a, pltpu.roll(b, 1, 1)), SWIGLU_LIMIT)
                x_lin = jnp.clip(jnp.where(even, pltpu.roll(a, LANES - 1, 1), b), -SWIGLU_LIMIT, SWIGLU_LIMIT)
                acts.append((x_glu * jax.nn.sigmoid(SWIGLU_ALPHA * x_glu) * (x_lin + 1.0)).astype(BF16))
            return jnp.concatenate(acts, axis=1)

        rounds = half // group
        y = bdn_ref[...]
        pending = up(0)
        for p in range(rounds):
            nxt = up(p + 1) if p + 1 < rounds else None
            y = y + _dot(activate(*pending), wdnb[p * group:(p + 1) * group, :])
            pending = nxt
        row = lax.broadcasted_iota(jnp.int32, (t, 1), 0)
        mine = jnp.logical_and(row >= lo, row < hi)
        keep = jnp.where(first_ref[w] == 1, jnp.zeros_like(y), y_ref[...])
        y_ref[...] = jnp.where(mine, y, keep)

    @pl.when(jnp.logical_and(hi <= lo, first_ref[w] == 1))
    def _():
        y_ref[...] = jnp.zeros_like(y_ref)


def _moe_call(items, xs, wup, bup, wdn, bdn):
    t = EXPERT_TILE
    n_rows, d = xs.shape
    ff = wdn.shape[1]
    n_items = items[0].shape[0]
    by_blk = lambda w, blk, xblk, ex, *_: (blk[w], 0)
    by_xblk = lambda w, blk, xblk, ex, *_: (xblk[w], 0)
    by_exp = lambda w, blk, xblk, ex, *_: (ex[w], 0, 0)
    grid_spec = pltpu.PrefetchScalarGridSpec(
        num_scalar_prefetch=len(items),
        grid=(n_items,),
        in_specs=[pl.BlockSpec((t, d), by_xblk),
                  pl.BlockSpec(memory_space=pl.ANY),
                  pl.BlockSpec((None, 1, 2 * ff), by_exp),
                  pl.BlockSpec(memory_space=pl.ANY),
                  pl.BlockSpec((None, 1, d), by_exp)],
        out_specs=pl.BlockSpec((t, d), by_blk),
        scratch_shapes=[pltpu.VMEM((2, d, 2 * ff), F32),
                        pltpu.VMEM((2, ff, d), F32),
                        pltpu.VMEM((d, 2 * ff), BF16),
                        pltpu.VMEM((d // LANES, ff, LANES), F32),
                        pltpu.VMEM((ff, d), BF16),
                        pltpu.SemaphoreType.DMA((2, 2))],
    )
    return pl.pallas_call(
        _moe_kernel,
        grid_spec=grid_spec,
        out_shape=jax.ShapeDtypeStruct((n_rows, d), F32),
        compiler_params=pltpu.CompilerParams(dimension_semantics=("arbitrary",), vmem_limit_bytes=MOE_VMEM_LIMIT),
        name="moe",
    )(*items, xs, wup, bup, wdn, bdn)


def _moe_items(counts, n_rows):
    t = EXPERT_TILE
    nb = n_rows // t
    n_items = nb + N_EXPERTS - 1
    ends = jnp.cumsum(counts)
    starts = ends - counts
    first_blk = starts // t
    tiles = jnp.where(counts > 0, (ends - 1) // t - first_blk + 1, 0)
    item_end = jnp.cumsum(tiles)
    item_start = item_end - tiles
    total = item_end[-1]
    w = jnp.arange(n_items, dtype=jnp.int32)
    wc = jnp.minimum(w, total - 1)
    ex = jnp.sum((item_end[None, :] <= wc[:, None]).astype(jnp.int32), axis=1)
    onehot = (ex[:, None] == jnp.arange(N_EXPERTS, dtype=jnp.int32)[None, :]).astype(jnp.int32)
    pick = lambda v: jnp.sum(onehot * v[None, :], axis=1)
    xblk = pick(first_blk) + wc - pick(item_start)
    lo = jnp.clip(pick(starts) - xblk * t, 0, t)
    hi = jnp.where(w < total, jnp.clip(pick(ends) - xblk * t, 0, t), lo)
    used_blocks = (ends[-1] + t - 1) // t
    blk = jnp.where(w < total, xblk, jnp.minimum(used_blocks + w - total, nb - 1))
    first = jnp.concatenate([jnp.ones((1,), jnp.int32), (blk[1:] != blk[:-1]).astype(jnp.int32)])
    new = jnp.logical_and(w < total, w == pick(item_start)).astype(jnp.int32)
    e_ids = jnp.arange(N_EXPERTS, dtype=jnp.int32)
    later = jnp.logical_and(e_ids[None, :] > e_ids[:, None], counts[None, :] > 0)
    next_e = jnp.min(jnp.where(later, e_ids[None, :], N_EXPERTS), axis=1)
    next_e = jnp.where(next_e == N_EXPERTS, -1, next_e)
    ordinal = jnp.cumsum((counts > 0).astype(jnp.int32)) - 1
    as_i32 = lambda v: v.astype(jnp.int32)
    return tuple(map(as_i32, (blk, xblk, ex, lo, hi, first, new, pick(next_e), pick(ordinal) % 2)))


def _comb_kernel(ns_ref, sl_ref, sg_ref, nsl_ref, nsg_ref, local_ref, w_ref, y_hbm, x1_ref, o_ref, ysort, sem):
    tm = TOKEN_TILE
    g = pl.program_id(0)
    nt = pl.num_programs(0)
    slot = g % 2

    def slab(lref, gref, q, s):
        return pltpu.make_async_copy(y_hbm.at[pl.ds(pl.multiple_of(gref[0, 0, q], SEG), SEG)],
                                     ysort.at[s, pl.ds(pl.multiple_of(lref[0, 0, q], SEG), SEG)], sem.at[s])

    def fetch(lref, gref, count, s):
        _for_each(count, lambda q: slab(lref, gref, q, s).start())

    @pl.when(g == 0)
    def _():
        ysort[...] = jnp.zeros_like(ysort)
        fetch(sl_ref, sg_ref, ns_ref[0], 0)

    @pl.when(g + 1 < nt)
    def _():
        fetch(nsl_ref, nsg_ref, ns_ref[jnp.minimum(g + 1, nt - 1)], 1 - slot)

    _for_each(ns_ref[g], lambda q: pltpu.make_async_copy(
        y_hbm.at[pl.ds(0, SEG)], ysort.at[slot, pl.ds(0, SEG)], sem.at[slot]).wait())

    acc = x1_ref[...]
    for c in range(0, SORT_ROWS, SORT_CHUNK):
        m_iota = c + lax.broadcasted_iota(jnp.int32, (tm, SORT_CHUNK), 1)
        mix = jnp.zeros((tm, SORT_CHUNK), F32)
        for k in range(TOP_K):
            mix = jnp.where(m_iota == local_ref[:, k:k + 1], w_ref[:, k:k + 1], mix)
        acc = acc + _dot(mix.astype(BF16), ysort[slot, c:c + SORT_CHUNK, :].astype(BF16))
    o_ref[...] = acc


def _comb_call(n_slabs, slab_local, slab_global, local_t, w_t, y, x1):
    n, d = x1.shape
    tm = TOKEN_TILE
    nt = n // tm
    cur = lambda g, ns: (g, 0, 0)
    nxt = lambda g, ns: (jnp.minimum(g + 1, nt - 1), 0, 0)
    lists = lambda index_map: pl.BlockSpec((1, 1, MAX_SLABS), index_map, memory_space=pltpu.SMEM)
    grid_spec = pltpu.PrefetchScalarGridSpec(
        num_scalar_prefetch=1,
        grid=(nt,),
        in_specs=[lists(cur), lists(cur), lists(nxt), lists(nxt),
                  pl.BlockSpec((tm, TOP_K), lambda g, ns: (g, 0)),
                  pl.BlockSpec((tm, TOP_K), lambda g, ns: (g, 0)),
                  pl.BlockSpec(memory_space=pl.ANY),
                  pl.BlockSpec((tm, d), lambda g, ns: (g, 0))],
        out_specs=pl.BlockSpec((tm, d), lambda g, ns: (g, 0)),
        scratch_shapes=[pltpu.VMEM((2, SORT_ROWS, d), F32), pltpu.SemaphoreType.DMA((2,))],
    )
    return pl.pallas_call(
        _comb_kernel,
        grid_spec=grid_spec,
        out_shape=jax.ShapeDtypeStruct((n, d), F32),
        compiler_params=pltpu.CompilerParams(dimension_semantics=("arbitrary",), vmem_limit_bytes=VMEM_LIMIT),
        name="comb",
    )(n_slabs, slab_local, slab_global, slab_local, slab_global, local_t, w_t, y, x1)


def _layer(x2, pos2, bsz, seq, g_mix, w_in, g_cq, w_uq, g_ckv, w_ukv, g_qnorm, g_knorm, w_o_sb, w_o_mla, w_out,
           g_ffn, w_router, b_router, w_up, b_up, w_down, b_down):
    n, d = x2.shape
    c_sb = 3 * SB_WIDTH
    c_q = c_sb + MLA_Q_LORA
    c_kv = c_q + MLA_KV_LORA
    c_kr = c_kv + MLA_ROPE
    w1 = w_in[:, :2 * SB_WIDTH].astype(BF16)
    w1v = w_in[:, 2 * SB_WIDTH:c_sb].T.astype(BF16)
    w_kr = jnp.pad(w_in[:, c_kv:c_kr], ((0, 0), (MLA_NOPE, HEAD_PAD - MLA_QK)))
    w2 = jnp.concatenate([w_in[:, c_sb:c_kv], w_kr], axis=1).astype(BF16)
    w3 = w_in[:, c_kr:].astype(BF16)
    wuq = jnp.pad(w_uq.reshape(MLA_Q_LORA, MLA_HEADS, MLA_QK), ((0, 0), (0, 0), (0, HEAD_PAD - MLA_QK)))
    wuq = wuq.reshape(MLA_Q_LORA, MLA_HEADS * HEAD_PAD).astype(BF16)
    wukv = w_ukv.reshape(MLA_KV_LORA, MLA_HEADS, MLA_NOPE + MLA_V)
    wuk = jnp.pad(wukv[:, :, :MLA_NOPE], ((0, 0), (0, 0), (0, HEAD_PAD - MLA_NOPE)))
    wuk = wuk.reshape(MLA_KV_LORA, MLA_HEADS * HEAD_PAD).astype(BF16)
    wuv = wukv[:, :, MLA_NOPE:].reshape(MLA_KV_LORA, MLA_HEADS * MLA_V).T.astype(BF16)
    gq = jnp.tile(jnp.pad(g_qnorm, (0, HEAD_PAD - MLA_QK)), MLA_HEADS)[None, :]
    gk = jnp.tile(jnp.pad(g_knorm, (0, HEAD_PAD - MLA_QK)), MLA_HEADS)[None, :]
    half = MLA_ROPE // 2
    inv_freq = ROPE_THETA ** (-jnp.arange(half, dtype=F32) / half)
    invf = jnp.pad(jnp.concatenate([inv_freq, inv_freq]), (MLA_NOPE, HEAD_PAD - MLA_QK))[None, :]
    sgn = jnp.pad(jnp.concatenate([-jnp.ones((half,), F32), jnp.ones((half,), F32)]),
                  (MLA_NOPE, HEAD_PAD - MLA_QK))[None, :]

    sb, sbv, qm, km, vm, gates = _pre_call(x2, pos2, bsz, seq, g_mix[None, :], w1, w1v, w2, w3, g_cq[None, :],
                                           g_ckv[None, :], wuq, wuk, wuv, gq, gk, invf, sgn)
    o_sb = _sb_call(sb, sbv, bsz, seq)
    o_mla = _mla_call(qm, km, vm, bsz, seq)

    wr_t = w_router.T
    wr_hi = wr_t.astype(BF16)
    wr_lo = (wr_t - wr_hi.astype(F32)).astype(BF16)
    x1, h2, top_w, top_e, pos = _post_call(
        o_sb, o_mla, gates, x2, w_o_sb.astype(BF16), w_o_mla.astype(BF16), w_out.astype(BF16), g_ffn[None, :],
        wr_hi, wr_lo, b_router[:, None])

    padded, local, n_slabs, slab_local, slab_global = _route(top_e, pos, n)
    xs = _disp_call(n_slabs, jnp.sum(padded)[None], slab_local, slab_global, local, h2)
    y = _moe_call(_moe_items(padded, xs.shape[0]), xs, w_up, b_up[:, None, :], w_down, b_down[:, None, :])
    return _comb_call(n_slabs, slab_local, slab_global, local.T, top_w.T, y, x1)


def kernel(x, positions, g_mix, w_in, g_cq, w_uq, g_ckv, w_ukv, g_qnorm, g_knorm, w_o_sb, w_o_mla, w_out, g_ffn,
           w_router, b_router, w_up, b_up, w_down, b_down):
    bsz, seq, d = x.shape
    x2 = x.reshape(bsz * seq, d)
    pos2 = positions.reshape(bsz * seq, 1)
    for l in range(g_mix.shape[0]):
        x2 = _layer(x2, pos2, bsz, seq, g_mix[l], w_in[l], g_cq[l], w_uq[l], g_ckv[l], w_ukv[l], g_qnorm[l],
                    g_knorm[l], w_o_sb[l], w_o_mla[l], w_out[l], g_ffn[l], w_router[l], b_router[l], w_up[l],
                    b_up[l], w_down[l], b_down[l])
    return x2.reshape(bsz, seq, d)
```
